```python
import math
import jax, jax.numpy as jnp
from jax import lax
import numpy as np

D_MODEL = 1024
BATCH = 2
SEQ = 16384
DEPTH = 2

BLOCK = 128
WINDOW = 128
H_A = 8
KV_A = 2
G_A = H_A // KV_A
DH_A = 64
H_B = 4
DH_B = 64
N_BUCKETS = 32
MAX_EXACT = N_BUCKETS // 2
REL_MAX_DIST = 128
N_BIAS_HEADS = H_A + H_B
N_EXPERTS = 32
TOP_K = 4
D_FF = D_MODEL
SWIGLU_ALPHA = 1.702
SWIGLU_LIMIT = 7.0
MOE_BLOCK = 128
NORM_EPS = 1e-6
SUBLN_EPS = 1e-5
QA_W = H_A * DH_A
KA_W = KV_A * DH_A
VA_W = KV_A * DH_A
QB_W = H_B * 2 * DH_B
KB_W = H_B * 2 * DH_B
VB_W = H_B * 2 * DH_B
SPLIT_SIZES = (QA_W, KA_W, VA_W, QB_W, KB_W, VB_W, D_MODEL, D_MODEL)
IN_W = QA_W + KA_W + VA_W + QB_W + KB_W + VB_W + 2 * D_MODEL
MIX_W = QA_W + H_B * 2 * DH_B

kernel_name = "hybrid_swa_sink_diffattn_moe_block"


def rms_norm(x, g, eps=NORM_EPS):
    xf = x.astype(jnp.float32)
    y = xf * lax.rsqrt(jnp.mean(xf * xf, axis=-1, keepdims=True) + eps)
    return (y * g.astype(jnp.float32)).astype(x.dtype)


def t5_bucket(dist):
    n = jnp.maximum(dist, 0)
    nf = jnp.maximum(n, 1).astype(jnp.float32)
    large = MAX_EXACT + (jnp.log(nf / MAX_EXACT) / math.log(REL_MAX_DIST / MAX_EXACT)
                         * (N_BUCKETS - MAX_EXACT)).astype(jnp.int32)
    large = jnp.minimum(large, N_BUCKETS - 1)
    return jnp.where(n < MAX_EXACT, n, large)


def sliding_window_attention(q, k, v, sinks, bias_table):
    b, s = q.shape[:2]
    nb = s // BLOCK
    qb = (q * DH_A ** -0.5).reshape(b, nb, BLOCK, KV_A, G_A, DH_A)

    def band(t):
        tp = jnp.pad(t, ((0, 0), (BLOCK, 0), (0, 0), (0, 0))).reshape(b, nb + 1, BLOCK, KV_A, DH_A)
        return jnp.concatenate([tp[:, :-1], tp[:, 1:]], axis=2)

    kb, vb = band(k), band(v)
    sc = jnp.einsum('bnqhgd,bnkhd->bnhgqk', qb, kb, preferred_element_type=jnp.float32)
    qi = jnp.arange(BLOCK)[:, None]
    kj = jnp.arange(2 * BLOCK)[None, :]
    dist = qi + BLOCK - kj
    bias = bias_table[t5_bucket(dist)].astype(jnp.float32)
    bias = bias.reshape(BLOCK, 2 * BLOCK, KV_A, G_A).transpose(2, 3, 0, 1)
    key_pos = jnp.arange(nb)[:, None, None] * BLOCK - BLOCK + kj[None]
    valid = (dist >= 0) & (dist < WINDOW) & (key_pos >= 0)
    sc = jnp.where(valid[None, :, None, None], sc + bias, -jnp.inf)
    sink = sinks.astype(jnp.float32).reshape(KV_A, G_A)[:, :, None, None]
    m = jnp.maximum(jnp.max(sc, axis=-1, keepdims=True), sink)
    e = jnp.exp(sc - m)
    p = e / (jnp.sum(e, axis=-1, keepdims=True) + jnp.exp(sink - m))
    o = jnp.einsum('bnhgqk,bnkhd->bnqhgd', p.astype(v.dtype), vb)
    return o.reshape(b, s, H_A * DH_A)


def differential_attention(q, k, v, lam, subln_g, lam_init, bias_table):
    b, s = q.shape[:2]
    nb = s // BLOCK
    qb = (q * DH_B ** -0.5).reshape(b, nb, BLOCK, H_B, 2, DH_B).transpose(1, 0, 3, 4, 2, 5)
    kt = k.transpose(0, 2, 3, 1, 4)
    vt = v.transpose(0, 2, 1, 3)
    k_pos = jnp.arange(s)

    def one_block(args):
        qblk, n = args
        sc = jnp.einsum('bhmqd,bhmkd->bhmqk', qblk, kt, preferred_element_type=jnp.float32)
        dist = n * BLOCK + jnp.arange(BLOCK)[:, None] - k_pos[None, :]
        bias = bias_table[t5_bucket(dist)].astype(jnp.float32).transpose(2, 0, 1)
        sc = jnp.where(dist >= 0, sc + bias[None, :, None], -jnp.inf)
        p = jax.nn.softmax(sc, axis=-1)
        a = p[:, :, 0] - lam * p[:, :, 1]
        return jnp.einsum('bhqk,bhkd->bhqd', a.astype(vt.dtype), vt)

    o = lax.map(one_block, (qb, jnp.arange(nb)))
    o = rms_norm(o, subln_g, SUBLN_EPS) * (1.0 - lam_init)
    return o.transpose(1, 0, 3, 2, 4).reshape(b, s, H_B * 2 * DH_B)


def clamped_swiglu(gu):
    gate = jnp.minimum(gu[..., :D_FF], SWIGLU_LIMIT)
    up = jnp.clip(gu[..., D_FF:], -SWIGLU_LIMIT, SWIGLU_LIMIT)
    return gate * jax.nn.sigmoid(SWIGLU_ALPHA * gate) * (up + 1.0)


def moe_ffn(h, w_router, b_router, w_gate_up, b_gate_up, w_down, b_down):
    b, s, d = h.shape
    t = b * s
    ht = h.reshape(t, d)
    logits = (ht @ w_router + b_router).astype(jnp.float32)
    top_val, top_idx = lax.top_k(logits, TOP_K)
    top_w = jax.nn.softmax(top_val, axis=-1)
    n_slots = t * TOP_K
    flat_e = top_idx.reshape(-1)
    order = jnp.argsort(flat_e).astype(jnp.int32)
    sorted_e = flat_e[order]
    counts = jnp.bincount(flat_e, length=N_EXPERTS).astype(jnp.int32)
    padded = (counts + MOE_BLOCK - 1) // MOE_BLOCK * MOE_BLOCK
    starts = jnp.cumsum(counts) - counts
    pends = jnp.cumsum(padded)
    pstarts = pends - padded
    dest = pstarts[sorted_e] + (jnp.arange(n_slots, dtype=jnp.int32) - starts[sorted_e])
    n_rows = n_slots + N_EXPERTS * MOE_BLOCK
    n_blocks = n_rows // MOE_BLOCK
    row_slot = jnp.full((n_rows,), n_slots, jnp.int32).at[dest].set(order)
    row_tok = row_slot // TOP_K
    row_w = jnp.concatenate([top_w.reshape(-1), jnp.zeros((1,), jnp.float32)])[row_slot]
    block_e = jnp.minimum(
        jnp.searchsorted(pends, jnp.arange(n_blocks, dtype=jnp.int32) * MOE_BLOCK, side='right'),
        N_EXPERTS - 1)
    xs = jnp.concatenate([ht, jnp.zeros((1, d), ht.dtype)])[row_tok].reshape(n_blocks, MOE_BLOCK, d)

    def expert_block(args):
        xb, e = args
        gu = xb @ w_gate_up[e] + b_gate_up[e]
        return clamped_swiglu(gu) @ w_down[e] + b_down[e]

    ys = lax.map(expert_block, (xs, block_e)).reshape(n_rows, d)
    ys = ys * row_w[:, None].astype(ys.dtype)
    out = jnp.zeros((t + 1, d), ys.dtype).at[row_tok].add(ys)[:t]
    return out.reshape(b, s, d)


def setup_inputs(seed: int = 0) -> dict:
    key = jax.random.key(seed)
    ks = jax.random.split(key, 22)
    nrm = jax.random.normal
    f32 = jnp.float32
    L, D, E, F = DEPTH, D_MODEL, N_EXPERTS, D_FF
    return {
        "x": nrm(ks[0], (BATCH, SEQ, D), f32),
        "c": nrm(ks[1], (BATCH, D), f32),
        "rel_bias_table": 0.2 * nrm(ks[2], (N_BUCKETS, N_BIAS_HEADS), f32),
        "w_ada": 0.5 * D ** -0.5 * nrm(ks[3], (L, D, 6 * D), f32),
        "b_ada": 0.01 * nrm(ks[4], (L, 6 * D), f32),
        "g_mix_pre": 1.0 + 0.05 * nrm(ks[5], (L, D), f32),
        "g_mix_post": 1.0 + 0.05 * nrm(ks[6], (L, D), f32),
        "g_ffn_pre": 1.0 + 0.05 * nrm(ks[7], (L, D), f32),
        "g_ffn_post": 1.0 + 0.05 * nrm(ks[8], (L, D), f32),
        "w_in": D ** -0.5 * nrm(ks[9], (L, D, IN_W), f32),
        "w_out": MIX_W ** -0.5 * nrm(ks[10], (L, MIX_W, D), f32),
        "attn_sinks": nrm(ks[11], (L, H_A), f32),
        "diff_lambda": 0.1 * nrm(ks[12], (L, 4, DH_B), f32),
        "diff_subln_g": 1.0 + 0.05 * nrm(ks[13], (L, 2 * DH_B), f32),
        "w_router": D ** -0.5 * nrm(ks[14], (L, D, E), f32),
        "b_router": 0.01 * nrm(ks[15], (L, E), f32),
        "w_gate_up": D ** -0.5 * nrm(ks[16], (L, E, D, 2 * F), f32),
        "b_gate_up": 0.01 * nrm(ks[17], (L, E, 2 * F), f32),
        "w_down": F ** -0.5 * nrm(ks[18], (L, E, F, D), f32),
        "b_down": 0.01 * nrm(ks[19], (L, E, D), f32),
    }


def reference(x, c, rel_bias_table, w_ada, b_ada, g_mix_pre, g_mix_post, g_ffn_pre, g_ffn_post,
              w_in, w_out, attn_sinks, diff_lambda, diff_subln_g, w_router, b_router,
              w_gate_up, b_gate_up, w_down, b_down):
    b, s, _ = x.shape
    split_idx = [int(v) for v in np.cumsum(SPLIT_SIZES)[:-1]]
    bias_a = rel_bias_table[:, :H_A]
    bias_b = rel_bias_table[:, H_A:]
    c_act = jax.nn.silu(c)
    for l in range(DEPTH):
        mod = (c_act @ w_ada[l] + b_ada[l])[:, None, :]
        sh_m, sc_m, gt_m, sh_f, sc_f, gt_f = jnp.split(mod, 6, axis=-1)

        h = rms_norm(x, g_mix_pre[l]) * (1.0 + sc_m) + sh_m
        proj = h @ w_in[l]
        qa, ka, va, qb, kb, vb, gza, gzb = jnp.split(proj, split_idx, axis=-1)
        o_a = sliding_window_attention(qa.reshape(b, s, H_A, DH_A), ka.reshape(b, s, KV_A, DH_A),
                                       va.reshape(b, s, KV_A, DH_A), attn_sinks[l], bias_a)
        lam_init = 0.8 - 0.6 * math.exp(-0.3 * l)
        lp = diff_lambda[l].astype(jnp.float32)
        lam = jnp.exp(jnp.sum(lp[0] * lp[1])) - jnp.exp(jnp.sum(lp[2] * lp[3])) + lam_init
        o_b = differential_attention(qb.reshape(b, s, H_B, 2, DH_B), kb.reshape(b, s, H_B, 2, DH_B),
                                     vb.reshape(b, s, H_B, 2 * DH_B), lam, diff_subln_g[l],
                                     lam_init, bias_b)
        y = (jax.nn.sigmoid(gza) * (o_a @ w_out[l][:QA_W])
             + jax.nn.sigmoid(gzb) * (o_b @ w_out[l][QA_W:]))
        x = x + gt_m * rms_norm(y, g_mix_post[l])

        h = rms_norm(x, g_ffn_pre[l]) * (1.0 + sc_f) + sh_f
        y = moe_ffn(h, w_router[l], b_router[l], w_gate_up[l], b_gate_up[l], w_down[l], b_down[l])
        x = x + gt_f * rms_norm(y, g_ffn_post[l])
    return x
```

```python
import functools
import math

import numpy as np
import jax
import jax.numpy as jnp
from jax import lax
from jax.experimental import pallas as pl
from jax.experimental.pallas import tpu as pltpu

F32 = jnp.float32
BF16 = jnp.bfloat16
I32 = jnp.int32

D_MODEL = 1024
BLOCK = 128
WINDOW = 128
H_A, KV_A, DH_A = 8, 2, 64
G_A = H_A // KV_A
H_B, DH_B = 4, 64
DV_B = 2 * DH_B
N_BUCKETS = 32
MAX_EXACT = N_BUCKETS // 2
REL_MAX_DIST = 128
N_EXPERTS = 32
TOP_K = 4
D_FF = D_MODEL
SWIGLU_ALPHA = 1.702
SWIGLU_LIMIT = 7.0
NORM_EPS = 1e-6
SUBLN_EPS = 1e-5
QA_W = H_A * DH_A
KA_W = KV_A * DH_A
VA_W = KV_A * DH_A
QB_W = H_B * 2 * DH_B
KB_W = H_B * 2 * DH_B
VB_W = H_B * 2 * DH_B
SPLIT_SIZES = (QA_W, KA_W, VA_W, QB_W, KB_W, VB_W, D_MODEL, D_MODEL)

LANES = 128
SUBLANES = 8
ROW_CHUNKS = D_MODEL // LANES
MASK_VALUE = -1e30

TM_PROJ = 512
TB_DIFF = 256
TM_SWA = 512
BM_EXPERT = 256
TG_DISPATCH = 512
TC_COMBINE = 256
DMA_UNROLL = 8
VMEM_LIMIT = 48 * 1024 * 1024


def _cparams(n_axes):
    return pltpu.CompilerParams(dimension_semantics=("arbitrary",) * n_axes,
                                vmem_limit_bytes=VMEM_LIMIT)


def _sigmoid(x):
    return 1.0 / (1.0 + jnp.exp(-x))


def _t5_bucket_np(dist):
    n = np.maximum(dist, 0)
    nf = np.maximum(n, 1).astype(np.float32)
    large = MAX_EXACT + (np.log(nf / np.float32(MAX_EXACT)) / np.float32(math.log(REL_MAX_DIST / MAX_EXACT))
                         * np.float32(N_BUCKETS - MAX_EXACT)).astype(np.int32)
    large = np.minimum(large, N_BUCKETS - 1)
    return np.where(n < MAX_EXACT, n, large)


def _mod_kernel(c_ref, w_ref, b_ref, o_ref):
    c = c_ref[...]
    ca = c * _sigmoid(c)
    o_ref[0] = jnp.dot(ca, w_ref[0], preferred_element_type=F32,
                       precision=lax.Precision.HIGHEST) + b_ref[0]


def _modulation(c, w_ada, b_ada):
    depth, d, n = w_ada.shape
    bsz = c.shape[0]
    rows = -(-bsz // SUBLANES) * SUBLANES
    c_pad = jnp.zeros((rows, d), F32).at[:bsz].set(c)
    tn = 1536
    out = pl.pallas_call(
        _mod_kernel,
        grid=(depth, n // tn),
        in_specs=[pl.BlockSpec((rows, d), lambda l, j: (0, 0)),
                  pl.BlockSpec((1, d, tn), lambda l, j: (l, 0, j)),
                  pl.BlockSpec((1, 1, tn), lambda l, j: (l, 0, j))],
        out_specs=pl.BlockSpec((1, rows, tn), lambda l, j: (l, 0, j)),
        out_shape=jax.ShapeDtypeStruct((depth, rows, n), F32),
        name="adaln_modulation",
        compiler_params=_cparams(2),
    )(c_pad, w_ada, b_ada.reshape(depth, 1, n))
    return out[:, :bsz]


N_NAT = QA_W + 2 * KA_W + 2 * VA_W + KB_W + 2 * D_MODEL
O_QA, O_KA, O_VA, O_KB, O_GA, O_GB = 0, 512, 768, 1024, 1536, 2560


def _inproj_kernel(x_ref, sc_ref, sh_ref, g_ref, w_ref, wt_ref,
                   qa_ref, ka_ref, va_ref, kb_ref, ga_ref, gb_ref, qbt_ref, vbt_ref, *, tb):
    x = x_ref[...]
    ms = jnp.mean(x * x, axis=-1, keepdims=True)
    h = x * lax.rsqrt(ms + NORM_EPS) * g_ref[...]
    h = h * (1.0 + sc_ref[0]) + sh_ref[0]
    hb = h.astype(BF16)
    tm = x.shape[0]

    def proj(a, b):
        return jnp.dot(hb, w_ref[:, a:b], preferred_element_type=F32)

    qa_ref[...] = proj(O_QA, O_KA).astype(BF16)
    ka_ref[...] = proj(O_KA, O_VA).astype(BF16)
    va_ref[...] = proj(O_VA, O_KB).astype(BF16)
    kb_ref[...] = proj(O_KB, O_GA).astype(BF16)
    ga_ref[...] = _sigmoid(proj(O_GA, O_GB)).astype(BF16)
    gb_ref[...] = _sigmoid(proj(O_GB, N_NAT)).astype(BF16)
    for t_ref, base in ((qbt_ref, 0), (vbt_ref, QB_W)):
        for hh in range(H_B):
            rt = lax.dot_general(wt_ref[base + hh * DV_B: base + (hh + 1) * DV_B, :], hb,
                                 (((1,), (1,)), ((), ())), preferred_element_type=F32)
            for j in range(tm // tb):
                t_ref[0, hh, j] = rt[:, j * tb:(j + 1) * tb].astype(BF16)


def _in_projection(x2d, sc, sh, g, w_nat, w_t, bsz, seq):
    t, d = x2d.shape
    tm, tb = TM_PROJ, TB_DIFF
    spb = seq // tm
    nqb = seq // tb
    row = lambda i: (i, 0)
    bat = lambda i: (i // spb, 0, 0)
    tr = lambda i: (i // spb, 0, i % spb, 0, 0)
    outs = pl.pallas_call(
        functools.partial(_inproj_kernel, tb=tb),
        grid=(t // tm,),
        in_specs=[pl.BlockSpec((tm, d), row),
                  pl.BlockSpec((1, 1, d), bat),
                  pl.BlockSpec((1, 1, d), bat),
                  pl.BlockSpec((1, d), lambda i: (0, 0)),
                  pl.BlockSpec((d, N_NAT), lambda i: (0, 0)),
                  pl.BlockSpec((QB_W + VB_W, d), lambda i: (0, 0))],
        out_specs=[pl.BlockSpec((tm, QA_W), row),
                   pl.BlockSpec((tm, 2 * KA_W), row),
                   pl.BlockSpec((tm, 2 * VA_W), row),
                   pl.BlockSpec((tm, KB_W), row),
                   pl.BlockSpec((tm, d), row),
                   pl.BlockSpec((tm, d), row),
                   pl.BlockSpec((1, H_B, tm // tb, DV_B, tb), tr),
                   pl.BlockSpec((1, H_B, tm // tb, DV_B, tb), tr)],
        out_shape=[jax.ShapeDtypeStruct((t, QA_W), BF16),
                   jax.ShapeDtypeStruct((t, 2 * KA_W), BF16),
                   jax.ShapeDtypeStruct((t, 2 * VA_W), BF16),
                   jax.ShapeDtypeStruct((t, KB_W), BF16),
                   jax.ShapeDtypeStruct((t, d), BF16),
                   jax.ShapeDtypeStruct((t, d), BF16),
                   jax.ShapeDtypeStruct((bsz, H_B, nqb, DV_B, tb), BF16),
                   jax.ShapeDtypeStruct((bsz, H_B, nqb, DV_B, tb), BF16)],
        name="in_projection",
        compiler_params=_cparams(1),
    )(x2d, sc, sh, g, w_nat, w_t)
    return outs


def _swa_kernel(sink_ref, q_ref, kc_ref, kp_ref, vc_ref, vp_ref, bias_ref, o_ref):
    i = pl.program_id(1)
    nsub = q_ref.shape[1] // BLOCK
    lane_k = lax.broadcasted_iota(I32, (2 * BLOCK, LANES), 1)
    lane_s = lax.broadcasted_iota(I32, (BLOCK, 2 * BLOCK), 1)
    first_mask = jnp.where((i == 0) & (lane_s < BLOCK), MASK_VALUE, 0.0).astype(F32)
    zero_b = jnp.zeros((2 * BLOCK, LANES), BF16)
    for j in range(nsub):
        rows = slice(j * BLOCK, (j + 1) * BLOCK)
        if j == 0:
            kband = jnp.concatenate([kp_ref[0], kc_ref[0, rows, :]], axis=0)
            vband = jnp.concatenate([vp_ref[0], vc_ref[0, rows, :]], axis=0)
        else:
            kband = kc_ref[0, (j - 1) * BLOCK:(j + 1) * BLOCK, :]
            vband = vc_ref[0, (j - 1) * BLOCK:(j + 1) * BLOCK, :]
        for kv in range(KV_A):
            kslab = kband[:, kv * LANES:(kv + 1) * LANES]
            vslab = vband[:, kv * LANES:(kv + 1) * LANES]
            kpad = (jnp.where(lane_k < DH_A, kslab, zero_b), jnp.where(lane_k >= DH_A, kslab, zero_b))
            vpad = (jnp.where(lane_k < DH_A, vslab, zero_b), jnp.where(lane_k >= DH_A, vslab, zero_b))
            for pp in range(G_A // 2):
                pair = kv * (G_A // 2) + pp
                qslab = q_ref[0, rows, pair * LANES:(pair + 1) * LANES]
                o_slab = jnp.zeros((BLOCK, LANES), F32)
                for par in range(2):
                    head = 2 * pair + par
                    s = lax.dot_general(qslab, kpad[par], (((1,), (1,)), ((), ())),
                                        preferred_element_type=F32)
                    s = s + bias_ref[head]
                    if j == 0:
                        s = s + first_mask
                    sink = sink_ref[head]
                    m = jnp.maximum(jnp.max(s, axis=-1, keepdims=True), sink)
                    e = jnp.exp(s - m)
                    den = jnp.sum(e, axis=-1, keepdims=True) + jnp.exp(sink - m)
                    pv = jnp.dot(e.astype(BF16), vpad[par], preferred_element_type=F32)
                    o_slab = o_slab + pv / den
                o_ref[0, rows, pair * LANES:(pair + 1) * LANES] = o_slab.astype(BF16)


def _swa_attention(qa, ka2, va2, sinks, bias_a):
    bsz, seq, _ = qa.shape
    tm = TM_SWA
    nsub = tm // BLOCK
    cur = lambda b, i: (b, i, 0)
    prev = lambda b, i: (b, jnp.maximum(i * nsub - 1, 0), 0)
    return pl.pallas_call(
        _swa_kernel,
        grid=(bsz, seq // tm),
        in_specs=[pl.BlockSpec(memory_space=pltpu.SMEM),
                  pl.BlockSpec((1, tm, QA_W), cur),
                  pl.BlockSpec((1, tm, 2 * KA_W), cur),
                  pl.BlockSpec((1, BLOCK, 2 * KA_W), prev),
                  pl.BlockSpec((1, tm, 2 * VA_W), cur),
                  pl.BlockSpec((1, BLOCK, 2 * VA_W), prev),
                  pl.BlockSpec((H_A, BLOCK, 2 * BLOCK), lambda b, i: (0, 0, 0))],
        out_specs=pl.BlockSpec((1, tm, QA_W), cur),
        out_shape=jax.ShapeDtypeStruct((bsz, seq, QA_W), BF16),
        name="swa_attention",
        compiler_params=_cparams(2),
    )(sinks, qa, ka2, ka2, va2, va2, bias_a)


def _diffattn_kernel(lp_ref, qt_ref, k_ref, vt_ref, bias_ref, g_ref, o_ref, acc_ref, ml_ref, *, tb, lam_init):
    i = pl.program_id(2)
    qt = qt_ref[0, 0, 0]
    row = lax.broadcasted_iota(I32, qt.shape, 0)
    zero_q = jnp.zeros_like(qt)
    qpad = (jnp.where(row < DH_B, qt, zero_q), jnp.where(row >= DH_B, qt, zero_q))
    acc_ref[...] = jnp.zeros_like(acc_ref)
    for mp in range(2):
        ml_ref[2 * mp:2 * mp + 1, :] = jnp.full((1, tb), MASK_VALUE, F32)
        ml_ref[2 * mp + 1:2 * mp + 2, :] = jnp.zeros((1, tb), F32)

    def step(j, bias_idx):
        kblk = k_ref[0, pl.ds(pl.multiple_of(j * tb, tb), tb), :]
        vblk = vt_ref[0, 0, j]
        for mp in range(2):
            s = jnp.dot(kblk, qpad[mp], preferred_element_type=F32)
            if bias_idx is not None:
                s = s + bias_ref[0, bias_idx]
            m_old = ml_ref[2 * mp:2 * mp + 1, :]
            l_old = ml_ref[2 * mp + 1:2 * mp + 2, :]
            m_new = jnp.maximum(m_old, jnp.max(s, axis=0, keepdims=True))
            p = jnp.exp(s - m_new)
            alpha = jnp.exp(m_old - m_new)
            ml_ref[2 * mp:2 * mp + 1, :] = m_new
            ml_ref[2 * mp + 1:2 * mp + 2, :] = alpha * l_old + jnp.sum(p, axis=0, keepdims=True)
            acc_ref[mp] = alpha * acc_ref[mp] + jnp.dot(vblk, p.astype(BF16), preferred_element_type=F32)

    def far_body(j, carry):
        step(j, None)
        return carry

    lax.fori_loop(0, jnp.maximum(i - 1, 0), far_body, 0)

    @pl.when(i >= 1)
    def _():
        step(i - 1, 1)

    step(i, 0)

    lp = lp_ref[...]
    lam = (jnp.exp(jnp.sum(lp[0:1] * lp[1:2], axis=-1, keepdims=True))
           - jnp.exp(jnp.sum(lp[2:3] * lp[3:4], axis=-1, keepdims=True)) + lam_init)
    o1 = acc_ref[0] / ml_ref[1:2, :]
    o2 = acc_ref[1] / ml_ref[3:4, :]
    o = o1 - lam * o2
    ms = jnp.mean(o * o, axis=0, keepdims=True)
    y = o * lax.rsqrt(ms + SUBLN_EPS) * g_ref[...] * (1.0 - lam_init)
    o_ref[0] = y.T.astype(BF16)


def _diff_attention(qbt, kb, vbt, bias_t, lp, g_b, lam_init):
    bsz, _, nqb, _, tb = qbt.shape
    seq = nqb * tb
    return pl.pallas_call(
        functools.partial(_diffattn_kernel, tb=tb, lam_init=lam_init),
        grid=(bsz, H_B, nqb),
        in_specs=[pl.BlockSpec((4, DH_B), lambda b, h, i: (0, 0)),
                  pl.BlockSpec((1, 1, 1, DV_B, tb), lambda b, h, i: (b, h, i, 0, 0)),
                  pl.BlockSpec((1, seq, DV_B), lambda b, h, i: (b, 0, h)),
                  pl.BlockSpec((1, 1, nqb, DV_B, tb), lambda b, h, i: (b, h, 0, 0, 0)),
                  pl.BlockSpec((1, 2, tb, tb), lambda b, h, i: (h, 0, 0, 0)),
                  pl.BlockSpec((DV_B, tb), lambda b, h, i: (0, 0))],
        out_specs=pl.BlockSpec((1, tb, DV_B), lambda b, h, i: (b, i, h)),
        out_shape=jax.ShapeDtypeStruct((bsz, seq, H_B * DV_B), BF16),
        scratch_shapes=[pltpu.VMEM((2, DV_B, tb), F32), pltpu.VMEM((SUBLANES, tb), F32)],
        name="diff_attention",
        compiler_params=_cparams(3),
    )(lp, qbt, kb, vbt, bias_t, g_b)


def _postmix_kernel(x_ref, oa_ref, ob_ref, ga_ref, gb_ref, wo_ref, gpost_ref, gt_ref, gpre_ref, sc_ref, sh_ref,
                    wr_ref, br_ref,
                    x1_ref, h2r_ref, idx_ref, w_ref, rank_ref, cnt_ref, carry_ref):
    step = pl.program_id(0)
    tm = x_ref.shape[0]

    @pl.when(step == 0)
    def _():
        carry_ref[...] = jnp.zeros_like(carry_ref)

    ya = jnp.dot(oa_ref[...], wo_ref[:QA_W, :], preferred_element_type=F32)
    yb = jnp.dot(ob_ref[...], wo_ref[QA_W:, :], preferred_element_type=F32)
    y = ga_ref[...].astype(F32) * ya + gb_ref[...].astype(F32) * yb
    ms = jnp.mean(y * y, axis=-1, keepdims=True)
    x1 = x_ref[...] + gt_ref[0] * (y * lax.rsqrt(ms + NORM_EPS) * gpost_ref[...])
    x1_ref[...] = x1
    ms1 = jnp.mean(x1 * x1, axis=-1, keepdims=True)
    h2 = x1 * lax.rsqrt(ms1 + NORM_EPS) * gpre_ref[...]
    h2 = h2 * (1.0 + sc_ref[0]) + sh_ref[0]
    for c in range(ROW_CHUNKS):
        h2r_ref[pl.ds(c, tm, stride=ROW_CHUNKS), :] = h2[:, c * LANES:(c + 1) * LANES]

    logits = lax.dot_general(wr_ref[...], h2, (((1,), (1,)), ((), ())), preferred_element_type=F32,
                             precision=lax.Precision.HIGHEST) + br_ref[...]
    erow = lax.broadcasted_iota(I32, logits.shape, 0)
    vals, sels = [], []
    cur = logits
    for k in range(TOP_K):
        mv = jnp.max(cur, axis=0, keepdims=True)
        idx = jnp.min(jnp.where(cur == mv, erow, N_EXPERTS), axis=0, keepdims=True)
        sel = erow == idx
        vals.append(mv)
        sels.append(sel)
        idx_ref[k:k + 1, :] = idx
        cur = jnp.where(sel, -jnp.inf, cur)
    es = [jnp.exp(v - vals[0]) for v in vals]
    den = es[0] + es[1] + es[2] + es[3]
    for k in range(TOP_K):
        w_ref[k:k + 1, :] = es[k] / den

    onehot = jnp.zeros(logits.shape, F32)
    for k in range(TOP_K):
        onehot = onehot + jnp.where(sels[k], 1.0, 0.0)
    srow = lax.broadcasted_iota(I32, (tm, tm), 0)
    scol = lax.broadcasted_iota(I32, (tm, tm), 1)
    upper = jnp.where(srow <= scol, 1.0, 0.0).astype(BF16)
    cum = jnp.dot(onehot.astype(BF16), upper, preferred_element_type=F32)
    carry = carry_ref[:, 0:1]
    excl = cum - onehot + carry
    for k in range(TOP_K):
        rank_ref[k:k + 1, :] = jnp.sum(jnp.where(sels[k], excl, 0.0), axis=0, keepdims=True).astype(I32)
    new_carry = carry + jnp.sum(onehot, axis=1, keepdims=True)
    carry_ref[...] = jnp.broadcast_to(new_carry, carry_ref.shape)
    cnt_ref[...] = jnp.broadcast_to(new_carry, cnt_ref.shape)


def _post_mix(x2d, oa, ob, ga, gb, wo, g_post, gt, g_pre, sc, sh, wr_t, br, seq):
    t, d = x2d.shape
    tm = TM_PROJ
    spb = seq // tm
    row = lambda i: (i, 0)
    bat = lambda i: (i // spb, 0, 0)
    const = lambda i: (0, 0)
    col = lambda i: (0, i)
    return pl.pallas_call(
        _postmix_kernel,
        grid=(t // tm,),
        in_specs=[pl.BlockSpec((tm, d), row),
                  pl.BlockSpec((tm, QA_W), row),
                  pl.BlockSpec((tm, H_B * DV_B), row),
                  pl.BlockSpec((tm, d), row),
                  pl.BlockSpec((tm, d), row),
                  pl.BlockSpec((QA_W + H_B * DV_B, d), const),
                  pl.BlockSpec((1, d), const),
                  pl.BlockSpec((1, 1, d), bat),
                  pl.BlockSpec((1, d), const),
                  pl.BlockSpec((1, 1, d), bat),
                  pl.BlockSpec((1, 1, d), bat),
                  pl.BlockSpec((N_EXPERTS, d), const),
                  pl.BlockSpec((N_EXPERTS, 1), const)],
        out_specs=[pl.BlockSpec((tm, d), row),
                   pl.BlockSpec((tm * ROW_CHUNKS, LANES), row),
                   pl.BlockSpec((TOP_K, tm), col),
                   pl.BlockSpec((TOP_K, tm), col),
                   pl.BlockSpec((TOP_K, tm), col),
                   pl.BlockSpec((N_EXPERTS, LANES), const)],
        out_shape=[jax.ShapeDtypeStruct((t, d), F32),
                   jax.ShapeDtypeStruct((t * ROW_CHUNKS, LANES), F32),
                   jax.ShapeDtypeStruct((TOP_K, t), I32),
                   jax.ShapeDtypeStruct((TOP_K, t), F32),
                   jax.ShapeDtypeStruct((TOP_K, t), I32),
                   jax.ShapeDtypeStruct((N_EXPERTS, LANES), F32)],
        scratch_shapes=[pltpu.VMEM((N_EXPERTS, LANES), F32)],
        name="post_mix_router",
        compiler_params=_cparams(1),
    )(x2d, oa, ob, ga, gb, wo, g_post, gt, g_pre, sc, sh, wr_t, br)


def _row_copy(src, src_row, dst, dst_row, sem):
    return pltpu.make_async_copy(src.at[pl.ds(pl.multiple_of(src_row * ROW_CHUNKS, ROW_CHUNKS), ROW_CHUNKS)],
                                 dst.at[pl.ds(pl.multiple_of(dst_row * ROW_CHUNKS, ROW_CHUNKS), ROW_CHUNKS)], sem)


def _dispatch_kernel(dest_ref, h2r_hbm, xs_in_hbm, xs_hbm, dsm, sem_idx, sem, *, tg):
    del xs_in_hbm
    base = pl.program_id(0) * tg
    idx_copy = pltpu.make_async_copy(dest_ref.at[0, 0], dsm, sem_idx)
    idx_copy.start()
    idx_copy.wait()

    def issue(it, carry):
        for u in range(DMA_UNROLL):
            tok = it * DMA_UNROLL + u
            for k in range(TOP_K):
                _row_copy(h2r_hbm, base + tok, xs_hbm, dsm[k * tg + tok], sem).start()
        return carry

    lax.fori_loop(0, tg // DMA_UNROLL, issue, 0)

    def drain(it, carry):
        for _ in range(DMA_UNROLL * TOP_K):
            _row_copy(h2r_hbm, 0, xs_hbm, 0, sem).wait()
        return carry

    lax.fori_loop(0, tg // DMA_UNROLL, drain, 0)


def _dispatch(dest_blocks, h2r, xs_zero, tg):
    nsteps = dest_blocks.shape[0]
    return pl.pallas_call(
        functools.partial(_dispatch_kernel, tg=tg),
        grid=(nsteps,),
        in_specs=[pl.BlockSpec((1, 1, TOP_K * tg), lambda i: (i, 0, 0)),
                  pl.BlockSpec(memory_space=pl.ANY),
                  pl.BlockSpec(memory_space=pl.ANY)],
        out_specs=pl.BlockSpec(memory_space=pl.ANY),
        out_shape=jax.ShapeDtypeStruct(xs_zero.shape, xs_zero.dtype),
        scratch_shapes=[pltpu.SMEM((TOP_K * tg,), I32), pltpu.SemaphoreType.DMA, pltpu.SemaphoreType.DMA],
        input_output_aliases={2: 0},
        name="moe_dispatch",
        compiler_params=_cparams(1),
    )(dest_blocks, h2r, xs_zero)


def _expert_kernel(be_ref, nu_ref, xs_ref, wgu_ref, bgu_ref, wd_ref, bd_ref, ys_ref, *, bm):
    i = pl.program_id(0)

    @pl.when(i < nu_ref[0])
    def _():
        x = jnp.concatenate([xs_ref[pl.ds(c, bm, stride=ROW_CHUNKS), :] for c in range(ROW_CHUNKS)], axis=-1)
        gu = jnp.dot(x.astype(BF16), wgu_ref[0], preferred_element_type=F32) + bgu_ref[0]
        gate = jnp.minimum(gu[:, :D_FF], SWIGLU_LIMIT)
        up = jnp.clip(gu[:, D_FF:], -SWIGLU_LIMIT, SWIGLU_LIMIT)
        act = gate * _sigmoid(SWIGLU_ALPHA * gate) * (up + 1.0)
        y = jnp.dot(act.astype(BF16), wd_ref[0], preferred_element_type=F32) + bd_ref[0]
        for c in range(ROW_CHUNKS):
            ys_ref[pl.ds(c, bm, stride=ROW_CHUNKS), :] = y[:, c * LANES:(c + 1) * LANES]

    @pl.when(i >= nu_ref[0])
    def _():
        ys_ref[...] = jnp.zeros_like(ys_ref)


def _experts(block_e, n_used, xs, wgu, bgu, wd, bd):
    bm = BM_EXPERT
    n_blocks = xs.shape[0] // (bm * ROW_CHUNKS)
    d = D_MODEL
    grid_spec = pltpu.PrefetchScalarGridSpec(
        num_scalar_prefetch=2,
        grid=(n_blocks,),
        in_specs=[pl.BlockSpec((bm * ROW_CHUNKS, LANES), lambda i, be, nu: (i, 0)),
                  pl.BlockSpec((1, d, 2 * D_FF), lambda i, be, nu: (be[i], 0, 0)),
                  pl.BlockSpec((1, 1, 2 * D_FF), lambda i, be, nu: (be[i], 0, 0)),
                  pl.BlockSpec((1, D_FF, d), lambda i, be, nu: (be[i], 0, 0)),
                  pl.BlockSpec((1, 1, d), lambda i, be, nu: (be[i], 0, 0))],
        out_specs=pl.BlockSpec((bm * ROW_CHUNKS, LANES), lambda i, be, nu: (i, 0)),
    )
    return pl.pallas_call(
        functools.partial(_expert_kernel, bm=bm),
        grid_spec=grid_spec,
        out_shape=jax.ShapeDtypeStruct(xs.shape, F32),
        name="moe_experts",
        compiler_params=_cparams(1),
    )(block_e, n_used, xs, wgu, bgu, wd, bd)


def _combine_kernel(dest_ref, w_ref, x_ref, gt_ref, g_ref, ys_hbm, o_ref, dsm, buf, sem_idx, sem, *, tc):
    idx_copy = pltpu.make_async_copy(dest_ref.at[0, 0], dsm, sem_idx)
    idx_copy.start()
    idx_copy.wait()

    def issue(it, carry):
        for u in range(DMA_UNROLL):
            tok = it * DMA_UNROLL + u
            for k in range(TOP_K):
                _row_copy(ys_hbm, dsm[k * tc + tok], buf, k * tc + tok, sem).start()
        return carry

    lax.fori_loop(0, tc // DMA_UNROLL, issue, 0)

    def drain(it, carry):
        for _ in range(DMA_UNROLL * TOP_K):
            _row_copy(ys_hbm, 0, buf, 0, sem).wait()
        return carry

    lax.fori_loop(0, tc // DMA_UNROLL, drain, 0)

    y = jnp.zeros((tc, D_MODEL), F32)
    for k in range(TOP_K):
        yk = jnp.concatenate([buf[pl.ds(k * tc * ROW_CHUNKS + c, tc, stride=ROW_CHUNKS), :]
                              for c in range(ROW_CHUNKS)], axis=-1)
        y = y + w_ref[:, k:k + 1] * yk
    ms = jnp.mean(y * y, axis=-1, keepdims=True)
    o_ref[...] = x_ref[...] + gt_ref[0] * (y * lax.rsqrt(ms + NORM_EPS) * g_ref[...])


def _combine(dest_blocks, w_tok, x1, gt, g_post, ys, seq, tc):
    t, d = x1.shape
    spb = seq // tc
    return pl.pallas_call(
        functools.partial(_combine_kernel, tc=tc),
        grid=(t // tc,),
        in_specs=[pl.BlockSpec((1, 1, TOP_K * tc), lambda i: (i, 0, 0)),
                  pl.BlockSpec((tc, TOP_K), lambda i: (i, 0)),
                  pl.BlockSpec((tc, d), lambda i: (i, 0)),
                  pl.BlockSpec((1, 1, d), lambda i: (i // spb, 0, 0)),
                  pl.BlockSpec((1, d), lambda i: (0, 0)),
                  pl.BlockSpec(memory_space=pl.ANY)],
        out_specs=pl.BlockSpec((tc, d), lambda i: (i, 0)),
        out_shape=jax.ShapeDtypeStruct((t, d), F32),
        scratch_shapes=[pltpu.SMEM((TOP_K * tc,), I32),
                        pltpu.VMEM((TOP_K * tc * ROW_CHUNKS, LANES), F32),
                        pltpu.SemaphoreType.DMA, pltpu.SemaphoreType.DMA],
        name="moe_combine",
        compiler_params=_cparams(1),
    )(dest_blocks, w_tok, x1, gt, g_post, ys)


def _dest_blocks(dest, step_tokens):
    k, t = dest.shape
    return dest.reshape(k, t // step_tokens, step_tokens).transpose(1, 0, 2).reshape(t // step_tokens, 1,
                                                                                      k * step_tokens)


def _bias_tables(rel_bias_table, tb):
    qi = np.arange(BLOCK)[:, None]
    kj = np.arange(2 * BLOCK)[None, :]
    dist = qi + BLOCK - kj
    valid = (dist >= 0) & (dist < WINDOW)
    bias_a = rel_bias_table[:, :H_A][_t5_bucket_np(dist)]
    bias_a = jnp.where(valid[:, :, None], bias_a, MASK_VALUE).transpose(2, 0, 1)
    kk = np.arange(tb)[:, None]
    qq = np.arange(tb)[None, :]
    tiles = []
    far = rel_bias_table[N_BUCKETS - 1, H_A:]
    for off in (0, tb):
        d = qq - kk + off
        b = rel_bias_table[:, H_A:][_t5_bucket_np(d)] - far
        tiles.append(jnp.where((d >= 0)[:, :, None], b, MASK_VALUE))
    bias_t = jnp.stack(tiles, axis=0).transpose(3, 0, 1, 2)
    return bias_a.astype(F32), bias_t.astype(F32)


def _prep_in_weights(w_in_l):
    qa, ka, va, qb, kb, vb, gza, gzb = jnp.split(w_in_l, [int(v) for v in np.cumsum(SPLIT_SIZES)[:-1]], axis=-1)

    def dup(w):
        return jnp.concatenate([w[:, :DH_A], w[:, :DH_A], w[:, DH_A:], w[:, DH_A:]], axis=-1)

    w_nat = jnp.concatenate([qa * DH_A ** -0.5, dup(ka), dup(va), kb, gza, gzb], axis=-1).astype(BF16)
    w_t = jnp.concatenate([qb * DH_B ** -0.5, vb], axis=-1).T.astype(BF16)
    return w_nat, w_t


def kernel(x, c, rel_bias_table, w_ada, b_ada, g_mix_pre, g_mix_post, g_ffn_pre, g_ffn_post, w_in, w_out,
           attn_sinks, diff_lambda, diff_subln_g, w_router, b_router, w_gate_up, b_gate_up, w_down, b_down):
    bsz, seq, d = x.shape
    depth = w_ada.shape[0]
    t = bsz * seq
    tb = TB_DIFF
    assert d == D_MODEL and tb >= BLOCK
    assert seq % TM_PROJ == 0 and seq % TM_SWA == 0 and seq % tb == 0
    assert seq % TG_DISPATCH == 0 and seq % TC_COMBINE == 0

    mod = _modulation(c, w_ada, b_ada)
    bias_a, bias_t = _bias_tables(rel_bias_table, tb)
    n_rows = t * TOP_K + N_EXPERTS * BM_EXPERT
    n_blocks = n_rows // BM_EXPERT

    x2d = x.reshape(t, d)
    for l in range(depth):
        sh_m, sc_m, gt_m, sh_f, sc_f, gt_f = [v[:, None, :] for v in jnp.split(mod[l], 6, axis=-1)]
        lam_init = 0.8 - 0.6 * math.exp(-0.3 * l)
        w_nat, w_t = _prep_in_weights(w_in[l])
        qa, ka2, va2, kb, ga, gb, qbt, vbt = _in_projection(
            x2d, sc_m, sh_m, g_mix_pre[l][None, :], w_nat, w_t, bsz, seq)
        o_a = _swa_attention(qa.reshape(bsz, seq, QA_W), ka2.reshape(bsz, seq, 2 * KA_W),
                             va2.reshape(bsz, seq, 2 * VA_W), attn_sinks[l], bias_a)
        g_b = jnp.broadcast_to(diff_subln_g[l][:, None], (DV_B, tb)).astype(F32)
        o_b = _diff_attention(qbt, kb.reshape(bsz, seq, KB_W), vbt, bias_t, diff_lambda[l], g_b, lam_init)
        x1, h2r, top_idx, top_w, rank, counts = _post_mix(
            x2d, o_a.reshape(t, QA_W), o_b.reshape(t, H_B * DV_B), ga, gb, w_out[l].astype(BF16),
            g_mix_post[l][None, :], gt_m, g_ffn_pre[l][None, :], sc_f, sh_f,
            w_router[l].T, b_router[l][:, None], seq)

        cnt = counts[:, 0].astype(I32)
        padded = (cnt + BM_EXPERT - 1) // BM_EXPERT * BM_EXPERT
        pends = jnp.cumsum(padded)
        pstarts = pends - padded
        dest = pstarts[top_idx] + rank
        block_e = jnp.minimum(
            jnp.searchsorted(pends, jnp.arange(n_blocks, dtype=I32) * BM_EXPERT, side='right'),
            N_EXPERTS - 1).astype(I32)
        n_used = (pends[-1:] // BM_EXPERT).astype(I32)

        xs = _dispatch(_dest_blocks(dest, TG_DISPATCH), h2r,
                       jnp.zeros((n_rows * ROW_CHUNKS, LANES), F32), TG_DISPATCH)
        ys = _experts(block_e, n_used, xs, w_gate_up[l].astype(BF16),
                      b_gate_up[l][:, None, :], w_down[l].astype(BF16), b_down[l][:, None, :])
        x2d = _combine(_dest_blocks(dest, TC_COMBINE), top_w.T, x1, gt_f, g_ffn_post[l][None, :], ys,
                       seq, TC_COMBINE)
    return x2d.reshape(bsz, seq, d)
```

```python
import functools
import math

import numpy as np
import jax
import jax.numpy as jnp
from jax import lax
from jax.experimental import pallas as pl
from jax.experimental.pallas import tpu as pltpu

F32 = jnp.float32
BF16 = jnp.bfloat16
I32 = jnp.int32

D_MODEL = 1024
BLOCK = 128
WINDOW = 128
H_A, KV_A, DH_A = 8, 2, 64
G_A = H_A // KV_A
H_B, DH_B = 4, 64
DV_B = 2 * DH_B
N_BUCKETS = 32
MAX_EXACT = N_BUCKETS // 2
REL_MAX_DIST = 128
N_EXPERTS = 32
TOP_K = 4
D_FF = D_MODEL
SWIGLU_ALPHA = 1.702
SWIGLU_LIMIT = 7.0
NORM_EPS = 1e-6
SUBLN_EPS = 1e-5
QA_W = H_A * DH_A
KA_W = KV_A * DH_A
VA_W = KV_A * DH_A
QB_W = H_B * 2 * DH_B
KB_W = H_B * 2 * DH_B
VB_W = H_B * 2 * DH_B
SPLIT_SIZES = (QA_W, KA_W, VA_W, QB_W, KB_W, VB_W, D_MODEL, D_MODEL)

LANES = 128
SUBLANES = 8
ROW_CHUNKS = D_MODEL // LANES
MASK_VALUE = -1e30

TM_PROJ = 512
TB_DIFF = 256
KC_DIFF = 4
LOG2E = math.log2(math.e)
TM_SWA = 512
BM_EXPERT = 256
TG_DISPATCH = 512
TC_COMBINE = 256
DMA_UNROLL = 8
VMEM_LIMIT = 48 * 1024 * 1024


def _cparams(n_axes):
    return pltpu.CompilerParams(dimension_semantics=("arbitrary",) * n_axes,
                                vmem_limit_bytes=VMEM_LIMIT)


def _sigmoid(x):
    return 1.0 / (1.0 + jnp.exp(-x))


def _t5_bucket_np(dist):
    n = np.maximum(dist, 0)
    nf = np.maximum(n, 1).astype(np.float32)
    large = MAX_EXACT + (np.log(nf / np.float32(MAX_EXACT)) / np.float32(math.log(REL_MAX_DIST / MAX_EXACT))
                         * np.float32(N_BUCKETS - MAX_EXACT)).astype(np.int32)
    large = np.minimum(large, N_BUCKETS - 1)
    return np.where(n < MAX_EXACT, n, large)


def _mod_kernel(c_ref, w_ref, b_ref, o_ref):
    c = c_ref[...]
    ca = c * _sigmoid(c)
    o_ref[0] = jnp.dot(ca, w_ref[0], preferred_element_type=F32,
                       precision=lax.Precision.HIGHEST) + b_ref[0]


def _modulation(c, w_ada, b_ada):
    depth, d, n = w_ada.shape
    bsz = c.shape[0]
    rows = -(-bsz // SUBLANES) * SUBLANES
    c_pad = jnp.zeros((rows, d), F32).at[:bsz].set(c)
    tn = 1536
    out = pl.pallas_call(
        _mod_kernel,
        grid=(depth, n // tn),
        in_specs=[pl.BlockSpec((rows, d), lambda l, j: (0, 0)),
                  pl.BlockSpec((1, d, tn), lambda l, j: (l, 0, j)),
                  pl.BlockSpec((1, 1, tn), lambda l, j: (l, 0, j))],
        out_specs=pl.BlockSpec((1, rows, tn), lambda l, j: (l, 0, j)),
        out_shape=jax.ShapeDtypeStruct((depth, rows, n), F32),
        name="adaln_modulation",
        compiler_params=_cparams(2),
    )(c_pad, w_ada, b_ada.reshape(depth, 1, n))
    return out[:, :bsz]


N_NAT = QA_W + 2 * KA_W + 2 * VA_W + KB_W + 2 * D_MODEL
O_QA, O_KA, O_VA, O_KB, O_GA, O_GB = 0, 512, 768, 1024, 1536, 2560


def _inproj_kernel(x_ref, sc_ref, sh_ref, g_ref, w_ref, wt_ref,
                   qa_ref, ka_ref, va_ref, kb_ref, ga_ref, gb_ref, qbt_ref, vbt_ref, *, tb):
    x = x_ref[...]
    ms = jnp.mean(x * x, axis=-1, keepdims=True)
    h = x * lax.rsqrt(ms + NORM_EPS) * g_ref[...]
    h = h * (1.0 + sc_ref[0]) + sh_ref[0]
    hb = h.astype(BF16)
    tm = x.shape[0]

    def proj(a, b):
        return jnp.dot(hb, w_ref[:, a:b], preferred_element_type=F32)

    qa_ref[...] = proj(O_QA, O_KA).astype(BF16)
    ka_ref[...] = proj(O_KA, O_VA).astype(BF16)
    va_ref[...] = proj(O_VA, O_KB).astype(BF16)
    kb_ref[...] = proj(O_KB, O_GA).astype(BF16)
    ga_ref[...] = _sigmoid(proj(O_GA, O_GB)).astype(BF16)
    gb_ref[...] = _sigmoid(proj(O_GB, N_NAT)).astype(BF16)
    for t_ref, base in ((qbt_ref, 0), (vbt_ref, QB_W)):
        for hh in range(H_B):
            rt = lax.dot_general(wt_ref[base + hh * DV_B: base + (hh + 1) * DV_B, :], hb,
                                 (((1,), (1,)), ((), ())), preferred_element_type=F32)
            for j in range(tm // tb):
                t_ref[0, hh, j] = rt[:, j * tb:(j + 1) * tb].astype(BF16)


def _in_projection(x2d, sc, sh, g, w_nat, w_t, bsz, seq):
    t, d = x2d.shape
    tm, tb = TM_PROJ, TB_DIFF
    spb = seq // tm
    nqb = seq // tb
    row = lambda i: (i, 0)
    bat = lambda i: (i // spb, 0, 0)
    tr = lambda i: (i // spb, 0, i % spb, 0, 0)
    outs = pl.pallas_call(
        functools.partial(_inproj_kernel, tb=tb),
        grid=(t // tm,),
        in_specs=[pl.BlockSpec((tm, d), row),
                  pl.BlockSpec((1, 1, d), bat),
                  pl.BlockSpec((1, 1, d), bat),
                  pl.BlockSpec((1, d), lambda i: (0, 0)),
                  pl.BlockSpec((d, N_NAT), lambda i: (0, 0)),
                  pl.BlockSpec((QB_W + VB_W, d), lambda i: (0, 0))],
        out_specs=[pl.BlockSpec((tm, QA_W), row),
                   pl.BlockSpec((tm, 2 * KA_W), row),
                   pl.BlockSpec((tm, 2 * VA_W), row),
                   pl.BlockSpec((tm, KB_W), row),
                   pl.BlockSpec((tm, d), row),
                   pl.BlockSpec((tm, d), row),
                   pl.BlockSpec((1, H_B, tm // tb, DV_B, tb), tr),
                   pl.BlockSpec((1, H_B, tm // tb, DV_B, tb), tr)],
        out_shape=[jax.ShapeDtypeStruct((t, QA_W), BF16),
                   jax.ShapeDtypeStruct((t, 2 * KA_W), BF16),
                   jax.ShapeDtypeStruct((t, 2 * VA_W), BF16),
                   jax.ShapeDtypeStruct((t, KB_W), BF16),
                   jax.ShapeDtypeStruct((t, d), BF16),
                   jax.ShapeDtypeStruct((t, d), BF16),
                   jax.ShapeDtypeStruct((bsz, H_B, nqb, DV_B, tb), BF16),
                   jax.ShapeDtypeStruct((bsz, H_B, nqb, DV_B, tb), BF16)],
        name="in_projection",
        compiler_params=_cparams(1),
    )(x2d, sc, sh, g, w_nat, w_t)
    return outs


def _swa_kernel(sink_ref, q_ref, kc_ref, kp_ref, vc_ref, vp_ref, bias_ref, o_ref):
    i = pl.program_id(1)
    nsub = q_ref.shape[1] // BLOCK
    lane_k = lax.broadcasted_iota(I32, (2 * BLOCK, LANES), 1)
    lane_s = lax.broadcasted_iota(I32, (BLOCK, 2 * BLOCK), 1)
    first_mask = jnp.where((i == 0) & (lane_s < BLOCK), MASK_VALUE, 0.0).astype(F32)
    zero_b = jnp.zeros((2 * BLOCK, LANES), BF16)
    for j in range(nsub):
        rows = slice(j * BLOCK, (j + 1) * BLOCK)
        if j == 0:
            kband = jnp.concatenate([kp_ref[0], kc_ref[0, rows, :]], axis=0)
            vband = jnp.concatenate([vp_ref[0], vc_ref[0, rows, :]], axis=0)
        else:
            kband = kc_ref[0, (j - 1) * BLOCK:(j + 1) * BLOCK, :]
            vband = vc_ref[0, (j - 1) * BLOCK:(j + 1) * BLOCK, :]
        for kv in range(KV_A):
            kslab = kband[:, kv * LANES:(kv + 1) * LANES]
            vslab = vband[:, kv * LANES:(kv + 1) * LANES]
            kpad = (jnp.where(lane_k < DH_A, kslab, zero_b), jnp.where(lane_k >= DH_A, kslab, zero_b))
            vpad = (jnp.where(lane_k < DH_A, vslab, zero_b), jnp.where(lane_k >= DH_A, vslab, zero_b))
            for pp in range(G_A // 2):
                pair = kv * (G_A // 2) + pp
                qslab = q_ref[0, rows, pair * LANES:(pair + 1) * LANES]
                o_slab = jnp.zeros((BLOCK, LANES), F32)
                for par in range(2):
                    head = 2 * pair + par
                    s = lax.dot_general(qslab, kpad[par], (((1,), (1,)), ((), ())),
                                        preferred_element_type=F32)
                    s = s + bias_ref[head]
                    if j == 0:
                        s = s + first_mask
                    sink = sink_ref[head]
                    m = jnp.maximum(jnp.max(s, axis=-1, keepdims=True), sink)
                    e = jnp.exp(s - m)
                    den = jnp.sum(e, axis=-1, keepdims=True) + jnp.exp(sink - m)
                    pv = jnp.dot(e.astype(BF16), vpad[par], preferred_element_type=F32)
                    o_slab = o_slab + pv / den
                o_ref[0, rows, pair * LANES:(pair + 1) * LANES] = o_slab.astype(BF16)


def _swa_attention(qa, ka2, va2, sinks, bias_a):
    bsz, seq, _ = qa.shape
    tm = TM_SWA
    nsub = tm // BLOCK
    cur = lambda b, i: (b, i, 0)
    prev = lambda b, i: (b, jnp.maximum(i * nsub - 1, 0), 0)
    return pl.pallas_call(
        _swa_kernel,
        grid=(bsz, seq // tm),
        in_specs=[pl.BlockSpec(memory_space=pltpu.SMEM),
                  pl.BlockSpec((1, tm, QA_W), cur),
                  pl.BlockSpec((1, tm, 2 * KA_W), cur),
                  pl.BlockSpec((1, BLOCK, 2 * KA_W), prev),
                  pl.BlockSpec((1, tm, 2 * VA_W), cur),
                  pl.BlockSpec((1, BLOCK, 2 * VA_W), prev),
                  pl.BlockSpec((H_A, BLOCK, 2 * BLOCK), lambda b, i: (0, 0, 0))],
        out_specs=pl.BlockSpec((1, tm, QA_W), cur),
        out_shape=jax.ShapeDtypeStruct((bsz, seq, QA_W), BF16),
        name="swa_attention",
        compiler_params=_cparams(2),
    )(sinks, qa, ka2, ka2, va2, va2, bias_a)


def _diffattn_kernel(lp_ref, qt_ref, k_ref, vt_ref, bias_ref, g_ref, o_ref, acc_ref, *, tb, lam_init):
    i = pl.program_id(2)
    qt = qt_ref[0, 0, 0]
    row = lax.broadcasted_iota(I32, qt.shape, 0)
    zero_q = jnp.zeros_like(qt)
    qboth = jnp.concatenate([jnp.where(row < DH_B, qt, zero_q), jnp.where(row >= DH_B, qt, zero_q)], axis=1)
    acc_ref[...] = jnp.zeros_like(acc_ref)

    def step(chunks, m_old, l_old):
        scores = []
        m_new = m_old
        for j, bias in chunks:
            kblk = k_ref[0, pl.ds(pl.multiple_of(j * tb, tb), tb), :]
            s = jnp.dot(kblk, qboth, preferred_element_type=F32)
            if bias is not None:
                s = s + bias
            scores.append(s)
            m_new = jnp.maximum(m_new, jnp.max(s, axis=0, keepdims=True))
        alpha = jnp.exp2(m_old - m_new)
        l_new = alpha * l_old
        pv = None
        for (j, _), s in zip(chunks, scores):
            p = jnp.exp2(s - m_new)
            l_new = l_new + jnp.sum(p, axis=0, keepdims=True)
            d = jnp.dot(vt_ref[0, 0, j], p.astype(BF16), preferred_element_type=F32)
            pv = d if pv is None else pv + d
        acc_ref[...] = alpha * acc_ref[...] + pv
        return m_new, l_new

    n_far = jnp.maximum(i - 1, 0)
    n_big = n_far // KC_DIFF

    def big_body(it, carry):
        return step([(it * KC_DIFF + c, None) for c in range(KC_DIFF)], *carry)

    def small_body(j, carry):
        return step([(j, None)], *carry)

    carry = (jnp.full((1, 2 * tb), MASK_VALUE, F32), jnp.zeros((1, 2 * tb), F32))
    carry = lax.fori_loop(0, n_big, big_body, carry)
    carry = lax.fori_loop(n_big * KC_DIFF, n_far, small_body, carry)
    no_prev = jnp.where(i == 0, MASK_VALUE, 0.0).astype(F32)
    _, l_fin = step([(jnp.maximum(i - 1, 0), bias_ref[0, 1] + no_prev), (i, bias_ref[0, 0])], *carry)

    lp = lp_ref[...]
    lam = (jnp.exp(jnp.sum(lp[0:1] * lp[1:2], axis=-1, keepdims=True))
           - jnp.exp(jnp.sum(lp[2:3] * lp[3:4], axis=-1, keepdims=True)) + lam_init)
    acc = acc_ref[...]
    o = acc[:, :tb] / l_fin[:, :tb] - lam * (acc[:, tb:] / l_fin[:, tb:])
    ms = jnp.mean(o * o, axis=0, keepdims=True)
    y = o * lax.rsqrt(ms + SUBLN_EPS) * g_ref[...] * (1.0 - lam_init)
    o_ref[0] = y.T.astype(BF16)


def _diff_attention(qbt, kb, vbt, bias_t, lp, g_b, lam_init):
    bsz, _, nqb, _, tb = qbt.shape
    seq = nqb * tb
    return pl.pallas_call(
        functools.partial(_diffattn_kernel, tb=tb, lam_init=lam_init),
        grid=(bsz, H_B, nqb),
        in_specs=[pl.BlockSpec((4, DH_B), lambda b, h, i: (0, 0)),
                  pl.BlockSpec((1, 1, 1, DV_B, tb), lambda b, h, i: (b, h, i, 0, 0)),
                  pl.BlockSpec((1, seq, DV_B), lambda b, h, i: (b, 0, h)),
                  pl.BlockSpec((1, 1, nqb, DV_B, tb), lambda b, h, i: (b, h, 0, 0, 0)),
                  pl.BlockSpec((1, 2, tb, 2 * tb), lambda b, h, i: (h, 0, 0, 0)),
                  pl.BlockSpec((DV_B, tb), lambda b, h, i: (0, 0))],
        out_specs=pl.BlockSpec((1, tb, DV_B), lambda b, h, i: (b, i, h)),
        out_shape=jax.ShapeDtypeStruct((bsz, seq, H_B * DV_B), BF16),
        scratch_shapes=[pltpu.VMEM((DV_B, 2 * tb), F32)],
        name="diff_attention",
        compiler_params=_cparams(3),
    )(lp, qbt, kb, vbt, bias_t, g_b)


def _postmix_kernel(x_ref, oa_ref, ob_ref, ga_ref, gb_ref, wo_ref, gpost_ref, gt_ref, gpre_ref, sc_ref, sh_ref,
                    wr_ref, br_ref,
                    x1_ref, h2r_ref, idx_ref, w_ref, rank_ref, cnt_ref, carry_ref):
    step = pl.program_id(0)
    tm = x_ref.shape[0]

    @pl.when(step == 0)
    def _():
        carry_ref[...] = jnp.zeros_like(carry_ref)

    ya = jnp.dot(oa_ref[...], wo_ref[:QA_W, :], preferred_element_type=F32)
    yb = jnp.dot(ob_ref[...], wo_ref[QA_W:, :], preferred_element_type=F32)
    y = ga_ref[...].astype(F32) * ya + gb_ref[...].astype(F32) * yb
    ms = jnp.mean(y * y, axis=-1, keepdims=True)
    x1 = x_ref[...] + gt_ref[0] * (y * lax.rsqrt(ms + NORM_EPS) * gpost_ref[...])
    x1_ref[...] = x1
    ms1 = jnp.mean(x1 * x1, axis=-1, keepdims=True)
    h2 = x1 * lax.rsqrt(ms1 + NORM_EPS) * gpre_ref[...]
    h2 = h2 * (1.0 + sc_ref[0]) + sh_ref[0]
    for c in range(ROW_CHUNKS):
        h2r_ref[pl.ds(c, tm, stride=ROW_CHUNKS), :] = h2[:, c * LANES:(c + 1) * LANES]

    logits = lax.dot_general(wr_ref[...], h2, (((1,), (1,)), ((), ())), preferred_element_type=F32,
                             precision=lax.Precision.HIGHEST) + br_ref[...]
    erow = lax.broadcasted_iota(I32, logits.shape, 0)
    vals, sels = [], []
    cur = logits
    for k in range(TOP_K):
        mv = jnp.max(cur, axis=0, keepdims=True)
        idx = jnp.min(jnp.where(cur == mv, erow, N_EXPERTS), axis=0, keepdims=True)
        sel = erow == idx
        vals.append(mv)
        sels.append(sel)
        idx_ref[k:k + 1, :] = idx
        cur = jnp.where(sel, -jnp.inf, cur)
    es = [jnp.exp(v - vals[0]) for v in vals]
    den = es[0] + es[1] + es[2] + es[3]
    for k in range(TOP_K):
        w_ref[k:k + 1, :] = es[k] / den

    onehot = jnp.zeros(logits.shape, F32)
    for k in range(TOP_K):
        onehot = onehot + jnp.where(sels[k], 1.0, 0.0)
    srow = lax.broadcasted_iota(I32, (tm, tm), 0)
    scol = lax.broadcasted_iota(I32, (tm, tm), 1)
    upper = jnp.where(srow <= scol, 1.0, 0.0).astype(BF16)
    cum = jnp.dot(onehot.astype(BF16), upper, preferred_element_type=F32)
    carry = carry_ref[:, 0:1]
    excl = cum - onehot + carry
    for k in range(TOP_K):
        rank_ref[k:k + 1, :] = jnp.sum(jnp.where(sels[k], excl, 0.0), axis=0, keepdims=True).astype(I32)
    new_carry = carry + jnp.sum(onehot, axis=1, keepdims=True)
    carry_ref[...] = jnp.broadcast_to(new_carry, carry_ref.shape)
    cnt_ref[...] = jnp.broadcast_to(new_carry, cnt_ref.shape)


def _post_mix(x2d, oa, ob, ga, gb, wo, g_post, gt, g_pre, sc, sh, wr_t, br, seq):
    t, d = x2d.shape
    tm = TM_PROJ
    spb = seq // tm
    row = lambda i: (i, 0)
    bat = lambda i: (i // spb, 0, 0)
    const = lambda i: (0, 0)
    col = lambda i: (0, i)
    return pl.pallas_call(
        _postmix_kernel,
        grid=(t // tm,),
        in_specs=[pl.BlockSpec((tm, d), row),
                  pl.BlockSpec((tm, QA_W), row),
                  pl.BlockSpec((tm, H_B * DV_B), row),
                  pl.BlockSpec((tm, d), row),
                  pl.BlockSpec((tm, d), row),
                  pl.BlockSpec((QA_W + H_B * DV_B, d), const),
                  pl.BlockSpec((1, d), const),
                  pl.BlockSpec((1, 1, d), bat),
                  pl.BlockSpec((1, d), const),
                  pl.BlockSpec((1, 1, d), bat),
                  pl.BlockSpec((1, 1, d), bat),
                  pl.BlockSpec((N_EXPERTS, d), const),
                  pl.BlockSpec((N_EXPERTS, 1), const)],
        out_specs=[pl.BlockSpec((tm, d), row),
                   pl.BlockSpec((tm * ROW_CHUNKS, LANES), row),
                   pl.BlockSpec((TOP_K, tm), col),
                   pl.BlockSpec((TOP_K, tm), col),
                   pl.BlockSpec((TOP_K, tm), col),
                   pl.BlockSpec((N_EXPERTS, LANES), const)],
        out_shape=[jax.ShapeDtypeStruct((t, d), F32),
                   jax.ShapeDtypeStruct((t * ROW_CHUNKS, LANES), F32),
                   jax.ShapeDtypeStruct((TOP_K, t), I32),
                   jax.ShapeDtypeStruct((TOP_K, t), F32),
                   jax.ShapeDtypeStruct((TOP_K, t), I32),
                   jax.ShapeDtypeStruct((N_EXPERTS, LANES), F32)],
        scratch_shapes=[pltpu.VMEM((N_EXPERTS, LANES), F32)],
        name="post_mix_router",
        compiler_params=_cparams(1),
    )(x2d, oa, ob, ga, gb, wo, g_post, gt, g_pre, sc, sh, wr_t, br)


def _row_copy(src, src_row, dst, dst_row, sem):
    return pltpu.make_async_copy(src.at[pl.ds(pl.multiple_of(src_row * ROW_CHUNKS, ROW_CHUNKS), ROW_CHUNKS)],
                                 dst.at[pl.ds(pl.multiple_of(dst_row * ROW_CHUNKS, ROW_CHUNKS), ROW_CHUNKS)], sem)


def _dispatch_kernel(dest_ref, h2r_ref, xs_in_hbm, xs_hbm, dsm, sem_idx, sem, *, tg):
    del xs_in_hbm
    idx_copy = pltpu.make_async_copy(dest_ref.at[0, 0], dsm, sem_idx)
    idx_copy.start()
    idx_copy.wait()

    def issue(it, carry):
        for u in range(DMA_UNROLL):
            tok = it * DMA_UNROLL + u
            for k in range(TOP_K):
                _row_copy(h2r_ref, tok, xs_hbm, dsm[k * tg + tok], sem).start()
        return carry

    lax.fori_loop(0, tg // DMA_UNROLL, issue, 0)

    def drain(it, carry):
        for _ in range(DMA_UNROLL * TOP_K):
            _row_copy(h2r_ref, 0, xs_hbm, 0, sem).wait()
        return carry

    lax.fori_loop(0, tg // DMA_UNROLL, drain, 0)


def _dispatch(dest_blocks, h2r, xs_zero, tg):
    nsteps = dest_blocks.shape[0]
    return pl.pallas_call(
        functools.partial(_dispatch_kernel, tg=tg),
        grid=(nsteps,),
        in_specs=[pl.BlockSpec((1, 1, TOP_K * tg), lambda i: (i, 0, 0)),
                  pl.BlockSpec((tg * ROW_CHUNKS, LANES), lambda i: (i, 0)),
                  pl.BlockSpec(memory_space=pl.ANY)],
        out_specs=pl.BlockSpec(memory_space=pl.ANY),
        out_shape=jax.ShapeDtypeStruct(xs_zero.shape, xs_zero.dtype),
        scratch_shapes=[pltpu.SMEM((TOP_K * tg,), I32), pltpu.SemaphoreType.DMA, pltpu.SemaphoreType.DMA],
        input_output_aliases={2: 0},
        name="moe_dispatch",
        compiler_params=_cparams(1),
    )(dest_blocks, h2r, xs_zero)


def _expert_kernel(be_ref, nu_ref, xs_ref, wgu_ref, bgu_ref, wd_ref, bd_ref, ys_ref, *, bm):
    i = pl.program_id(0)

    @pl.when(i < nu_ref[0])
    def _():
        x = jnp.concatenate([xs_ref[pl.ds(c, bm, stride=ROW_CHUNKS), :] for c in range(ROW_CHUNKS)], axis=-1)
        gu = jnp.dot(x.astype(BF16), wgu_ref[0], preferred_element_type=F32) + bgu_ref[0]
        gate = jnp.minimum(gu[:, :D_FF], SWIGLU_LIMIT)
        up = jnp.clip(gu[:, D_FF:], -SWIGLU_LIMIT, SWIGLU_LIMIT)
        act = gate * _sigmoid(SWIGLU_ALPHA * gate) * (up + 1.0)
        y = jnp.dot(act.astype(BF16), wd_ref[0], preferred_element_type=F32) + bd_ref[0]
        for c in range(ROW_CHUNKS):
            ys_ref[pl.ds(c, bm, stride=ROW_CHUNKS), :] = y[:, c * LANES:(c + 1) * LANES]

    @pl.when(i >= nu_ref[0])
    def _():
        ys_ref[...] = jnp.zeros_like(ys_ref)


def _experts(block_e, n_used, xs, wgu, bgu, wd, bd):
    bm = BM_EXPERT
    n_blocks = xs.shape[0] // (bm * ROW_CHUNKS)
    d = D_MODEL
    grid_spec = pltpu.PrefetchScalarGridSpec(
        num_scalar_prefetch=2,
        grid=(n_blocks,),
        in_specs=[pl.BlockSpec((bm * ROW_CHUNKS, LANES), lambda i, be, nu: (i, 0)),
                  pl.BlockSpec((1, d, 2 * D_FF), lambda i, be, nu: (be[i], 0, 0)),
                  pl.BlockSpec((1, 1, 2 * D_FF), lambda i, be, nu: (be[i], 0, 0)),
                  pl.BlockSpec((1, D_FF, d), lambda i, be, nu: (be[i], 0, 0)),
                  pl.BlockSpec((1, 1, d), lambda i, be, nu: (be[i], 0, 0))],
        out_specs=pl.BlockSpec((bm * ROW_CHUNKS, LANES), lambda i, be, nu: (i, 0)),
    )
    return pl.pallas_call(
        functools.partial(_expert_kernel, bm=bm),
        grid_spec=grid_spec,
        out_shape=jax.ShapeDtypeStruct(xs.shape, F32),
        name="moe_experts",
        compiler_params=_cparams(1),
    )(block_e, n_used, xs, wgu, bgu, wd, bd)


def _combine_kernel(dest_ref, w_ref, x_ref, gt_ref, g_ref, ys_hbm, o_ref, dsm, buf, sem_idx, sem, *, tc):
    idx_copy = pltpu.make_async_copy(dest_ref.at[0, 0], dsm, sem_idx)
    idx_copy.start()
    idx_copy.wait()

    def issue(it, carry):
        for u in range(DMA_UNROLL):
            tok = it * DMA_UNROLL + u
            for k in range(TOP_K):
                _row_copy(ys_hbm, dsm[k * tc + tok], buf, k * tc + tok, sem).start()
        return carry

    lax.fori_loop(0, tc // DMA_UNROLL, issue, 0)

    def drain(it, carry):
        for _ in range(DMA_UNROLL * TOP_K):
            _row_copy(ys_hbm, 0, buf, 0, sem).wait()
        return carry

    lax.fori_loop(0, tc // DMA_UNROLL, drain, 0)

    y = jnp.zeros((tc, D_MODEL), F32)
    for k in range(TOP_K):
        yk = jnp.concatenate([buf[pl.ds(k * tc * ROW_CHUNKS + c, tc, stride=ROW_CHUNKS), :]
                              for c in range(ROW_CHUNKS)], axis=-1)
        y = y + w_ref[:, k:k + 1] * yk
    ms = jnp.mean(y * y, axis=-1, keepdims=True)
    o_ref[...] = x_ref[...] + gt_ref[0] * (y * lax.rsqrt(ms + NORM_EPS) * g_ref[...])


def _combine(dest_blocks, w_tok, x1, gt, g_post, ys, seq, tc):
    t, d = x1.shape
    spb = seq // tc
    return pl.pallas_call(
        functools.partial(_combine_kernel, tc=tc),
        grid=(t // tc,),
        in_specs=[pl.BlockSpec((1, 1, TOP_K * tc), lambda i: (i, 0, 0)),
                  pl.BlockSpec((tc, TOP_K), lambda i: (i, 0)),
                  pl.BlockSpec((tc, d), lambda i: (i, 0)),
                  pl.BlockSpec((1, 1, d), lambda i: (i // spb, 0, 0)),
                  pl.BlockSpec((1, d), lambda i: (0, 0)),
                  pl.BlockSpec(memory_space=pl.ANY)],
        out_specs=pl.BlockSpec((tc, d), lambda i: (i, 0)),
        out_shape=jax.ShapeDtypeStruct((t, d), F32),
        scratch_shapes=[pltpu.SMEM((TOP_K * tc,), I32),
                        pltpu.VMEM((TOP_K * tc * ROW_CHUNKS, LANES), F32),
                        pltpu.SemaphoreType.DMA, pltpu.SemaphoreType.DMA],
        name="moe_combine",
        compiler_params=_cparams(1),
    )(dest_blocks, w_tok, x1, gt, g_post, ys)


def _dest_blocks(dest, step_tokens):
    k, t = dest.shape
    return dest.reshape(k, t // step_tokens, step_tokens).transpose(1, 0, 2).reshape(t // step_tokens, 1,
                                                                                      k * step_tokens)


def _bias_tables(rel_bias_table, tb):
    def lookup(dist, table):
        onehot = np.eye(N_BUCKETS, dtype=np.float32)[_t5_bucket_np(dist)]
        return jnp.einsum('...n,nh->h...', onehot, table, precision=lax.Precision.HIGHEST)

    qi = np.arange(BLOCK)[:, None]
    kj = np.arange(2 * BLOCK)[None, :]
    dist = qi + BLOCK - kj
    valid = (dist >= 0) & (dist < WINDOW)
    bias_a = jnp.where(valid[None], lookup(dist, rel_bias_table[:, :H_A]), MASK_VALUE)
    kk = np.arange(tb)[:, None]
    qq = np.arange(tb)[None, :]
    table_b = rel_bias_table[:, H_A:]
    table_b = (table_b - table_b[N_BUCKETS - 1:]) * LOG2E
    tiles = []
    for off in (0, tb):
        d = qq - kk + off
        tile = jnp.where((d >= 0)[None], lookup(d, table_b), MASK_VALUE)
        tiles.append(jnp.concatenate([tile, tile], axis=-1))
    bias_t = jnp.stack(tiles, axis=1)
    return bias_a.astype(F32), bias_t.astype(F32)


def _prep_in_weights(w_in_l):
    qa, ka, va, qb, kb, vb, gza, gzb = jnp.split(w_in_l, [int(v) for v in np.cumsum(SPLIT_SIZES)[:-1]], axis=-1)

    def dup(w):
        return jnp.concatenate([w[:, :DH_A], w[:, :DH_A], w[:, DH_A:], w[:, DH_A:]], axis=-1)

    w_nat = jnp.concatenate([qa * DH_A ** -0.5, dup(ka), dup(va), kb, gza, gzb], axis=-1).astype(BF16)
    w_t = jnp.concatenate([qb * (DH_B ** -0.5 * LOG2E), vb], axis=-1).T.astype(BF16)
    return w_nat, w_t


def kernel(x, c, rel_bias_table, w_ada, b_ada, g_mix_pre, g_mix_post, g_ffn_pre, g_ffn_post, w_in, w_out,
           attn_sinks, diff_lambda, diff_subln_g, w_router, b_router, w_gate_up, b_gate_up, w_down, b_down):
    bsz, seq, d = x.shape
    depth = w_ada.shape[0]
    t = bsz * seq
    tb = TB_DIFF
    assert d == D_MODEL and tb >= BLOCK
    assert seq % TM_PROJ == 0 and seq % TM_SWA == 0 and seq % tb == 0
    assert seq % TG_DISPATCH == 0 and seq % TC_COMBINE == 0

    mod = _modulation(c, w_ada, b_ada)
    bias_a, bias_t = _bias_tables(rel_bias_table, tb)
    n_rows = t * TOP_K + N_EXPERTS * BM_EXPERT
    n_blocks = n_rows // BM_EXPERT

    x2d = x.reshape(t, d)
    for l in range(depth):
        sh_m, sc_m, gt_m, sh_f, sc_f, gt_f = [v[:, None, :] for v in jnp.split(mod[l], 6, axis=-1)]
        lam_init = 0.8 - 0.6 * math.exp(-0.3 * l)
        w_nat, w_t = _prep_in_weights(w_in[l])
        qa, ka2, va2, kb, ga, gb, qbt, vbt = _in_projection(
            x2d, sc_m, sh_m, g_mix_pre[l][None, :], w_nat, w_t, bsz, seq)
        o_a = _swa_attention(qa.reshape(bsz, seq, QA_W), ka2.reshape(bsz, seq, 2 * KA_W),
                             va2.reshape(bsz, seq, 2 * VA_W), attn_sinks[l], bias_a)
        g_b = jnp.broadcast_to(diff_subln_g[l][:, None], (DV_B, tb)).astype(F32)
        o_b = _diff_attention(qbt, kb.reshape(bsz, seq, KB_W), vbt, bias_t, diff_lambda[l], g_b, lam_init)
        x1, h2r, top_idx, top_w, rank, counts = _post_mix(
            x2d, o_a.reshape(t, QA_W), o_b.reshape(t, H_B * DV_B), ga, gb, w_out[l].astype(BF16),
            g_mix_post[l][None, :], gt_m, g_ffn_pre[l][None, :], sc_f, sh_f,
            w_router[l].T, b_router[l][:, None], seq)

        cnt = counts[:, 0].astype(I32)
        padded = (cnt + BM_EXPERT - 1) // BM_EXPERT * BM_EXPERT
        pends = jnp.cumsum(padded)
        pstarts = pends - padded
        experts = jnp.arange(N_EXPERTS, dtype=I32)
        dest = jnp.sum(jnp.where(top_idx[..., None] == experts, pstarts, 0), axis=-1) + rank
        block_start = jnp.arange(n_blocks, dtype=I32) * BM_EXPERT
        block_e = jnp.minimum(jnp.sum((pends[None, :] <= block_start[:, None]).astype(I32), axis=1),
                              N_EXPERTS - 1).astype(I32)
        n_used = (pends[-1:] // BM_EXPERT).astype(I32)

        xs = _dispatch(_dest_blocks(dest, TG_DISPATCH), h2r,
                       jnp.zeros((n_rows * ROW_CHUNKS, LANES), F32), TG_DISPATCH)
        ys = _experts(block_e, n_used, xs, w_gate_up[l].astype(BF16),
                      b_gate_up[l][:, None, :], w_down[l].astype(BF16), b_down[l][:, None, :])
        x2d = _combine(_dest_blocks(dest, TC_COMBINE), top_w.T, x1, gt_f, g_ffn_post[l][None, :], ys,
                       seq, TC_COMBINE)
    return x2d.reshape(bsz, seq, d)
```

```python
import functools
import math

import numpy as np
import jax
import jax.numpy as jnp
from jax import lax
from jax.experimental import pallas as pl
from jax.experimental.pallas import tpu as pltpu

F32 = jnp.float32
BF16 = jnp.bfloat16
I32 = jnp.int32

D_MODEL = 1024
BLOCK = 128
WINDOW = 128
H_A, KV_A, DH_A = 8, 2, 64
G_A = H_A // KV_A
H_B, DH_B = 4, 64
DV_B = 2 * DH_B
N_BUCKETS = 32
MAX_EXACT = N_BUCKETS // 2
REL_MAX_DIST = 128
N_EXPERTS = 32
TOP_K = 4
D_FF = D_MODEL
SWIGLU_ALPHA = 1.702
SWIGLU_LIMIT = 7.0
NORM_EPS = 1e-6
SUBLN_EPS = 1e-5
QA_W = H_A * DH_A
KA_W = KV_A * DH_A
VA_W = KV_A * DH_A
QB_W = H_B * 2 * DH_B
KB_W = H_B * 2 * DH_B
VB_W = H_B * 2 * DH_B
SPLIT_SIZES = (QA_W, KA_W, VA_W, QB_W, KB_W, VB_W, D_MODEL, D_MODEL)

LANES = 128
SUBLANES = 8
ROW_CHUNKS = D_MODEL // LANES
MASK_VALUE = -1e30

TM_PROJ = 512
TQ_DIFF = 512
TKC_DIFF = 256
KC_DIFF = 4
LOG2E = math.log2(math.e)
TM_SWA = 512
BM_EXPERT = 256
TG_DISPATCH = 512
TC_COMBINE = 256
DMA_UNROLL = 8
VMEM_LIMIT = 48 * 1024 * 1024


def _cparams(n_axes):
    return pltpu.CompilerParams(dimension_semantics=("arbitrary",) * n_axes,
                                vmem_limit_bytes=VMEM_LIMIT)


def _sigmoid(x):
    return 1.0 / (1.0 + jnp.exp(-x))


def _t5_bucket_np(dist):
    n = np.maximum(dist, 0)
    nf = np.maximum(n, 1).astype(np.float32)
    large = MAX_EXACT + (np.log(nf / np.float32(MAX_EXACT)) / np.float32(math.log(REL_MAX_DIST / MAX_EXACT))
                         * np.float32(N_BUCKETS - MAX_EXACT)).astype(np.int32)
    large = np.minimum(large, N_BUCKETS - 1)
    return np.where(n < MAX_EXACT, n, large)


def _mod_kernel(c_ref, w_ref, b_ref, o_ref):
    c = c_ref[...]
    ca = c * _sigmoid(c)
    o_ref[0] = jnp.dot(ca, w_ref[0], preferred_element_type=F32,
                       precision=lax.Precision.HIGHEST) + b_ref[0]


def _modulation(c, w_ada, b_ada):
    depth, d, n = w_ada.shape
    bsz = c.shape[0]
    rows = -(-bsz // SUBLANES) * SUBLANES
    c_pad = jnp.zeros((rows, d), F32).at[:bsz].set(c)
    tn = 1536
    out = pl.pallas_call(
        _mod_kernel,
        grid=(depth, n // tn),
        in_specs=[pl.BlockSpec((rows, d), lambda l, j: (0, 0)),
                  pl.BlockSpec((1, d, tn), lambda l, j: (l, 0, j)),
                  pl.BlockSpec((1, 1, tn), lambda l, j: (l, 0, j))],
        out_specs=pl.BlockSpec((1, rows, tn), lambda l, j: (l, 0, j)),
        out_shape=jax.ShapeDtypeStruct((depth, rows, n), F32),
        name="adaln_modulation",
        compiler_params=_cparams(2),
    )(c_pad, w_ada, b_ada.reshape(depth, 1, n))
    return out[:, :bsz]


N_NAT = QA_W + 2 * KA_W + 2 * VA_W + KB_W + 2 * D_MODEL
O_QA, O_KA, O_VA, O_KB, O_GA, O_GB = 0, 512, 768, 1024, 1536, 2560


def _inproj_kernel(x_ref, sc_ref, sh_ref, g_ref, w_ref, wt_ref,
                   qa_ref, ka_ref, va_ref, kb_ref, ga_ref, gb_ref, qbt_ref, vbt_ref, *, tq, tkc):
    x = x_ref[...]
    ms = jnp.mean(x * x, axis=-1, keepdims=True)
    h = x * lax.rsqrt(ms + NORM_EPS) * g_ref[...]
    h = h * (1.0 + sc_ref[0]) + sh_ref[0]
    hb = h.astype(BF16)
    tm = x.shape[0]

    def proj(a, b):
        return jnp.dot(hb, w_ref[:, a:b], preferred_element_type=F32)

    qa_ref[...] = proj(O_QA, O_KA).astype(BF16)
    ka_ref[...] = proj(O_KA, O_VA).astype(BF16)
    va_ref[...] = proj(O_VA, O_KB).astype(BF16)
    kb_ref[...] = proj(O_KB, O_GA).astype(BF16)
    ga_ref[...] = _sigmoid(proj(O_GA, O_GB)).astype(BF16)
    gb_ref[...] = _sigmoid(proj(O_GB, N_NAT)).astype(BF16)
    for t_ref, base, tb in ((qbt_ref, 0, tq), (vbt_ref, QB_W, tkc)):
        for hh in range(H_B):
            rt = lax.dot_general(wt_ref[base + hh * DV_B: base + (hh + 1) * DV_B, :], hb,
                                 (((1,), (1,)), ((), ())), preferred_element_type=F32)
            for j in range(tm // tb):
                t_ref[0, hh, j] = rt[:, j * tb:(j + 1) * tb].astype(BF16)


def _in_projection(x2d, sc, sh, g, w_nat, w_t, bsz, seq):
    t, d = x2d.shape
    tm, tq, tkc = TM_PROJ, TQ_DIFF, TKC_DIFF
    spb = seq // tm
    row = lambda i: (i, 0)
    bat = lambda i: (i // spb, 0, 0)
    tr = lambda i: (i // spb, 0, i % spb, 0, 0)
    outs = pl.pallas_call(
        functools.partial(_inproj_kernel, tq=tq, tkc=tkc),
        grid=(t // tm,),
        in_specs=[pl.BlockSpec((tm, d), row),
                  pl.BlockSpec((1, 1, d), bat),
                  pl.BlockSpec((1, 1, d), bat),
                  pl.BlockSpec((1, d), lambda i: (0, 0)),
                  pl.BlockSpec((d, N_NAT), lambda i: (0, 0)),
                  pl.BlockSpec((QB_W + VB_W, d), lambda i: (0, 0))],
        out_specs=[pl.BlockSpec((tm, QA_W), row),
                   pl.BlockSpec((tm, 2 * KA_W), row),
                   pl.BlockSpec((tm, 2 * VA_W), row),
                   pl.BlockSpec((tm, KB_W), row),
                   pl.BlockSpec((tm, d), row),
                   pl.BlockSpec((tm, d), row),
                   pl.BlockSpec((1, H_B, tm // tq, DV_B, tq), tr),
                   pl.BlockSpec((1, H_B, tm // tkc, DV_B, tkc), tr)],
        out_shape=[jax.ShapeDtypeStruct((t, QA_W), BF16),
                   jax.ShapeDtypeStruct((t, 2 * KA_W), BF16),
                   jax.ShapeDtypeStruct((t, 2 * VA_W), BF16),
                   jax.ShapeDtypeStruct((t, KB_W), BF16),
                   jax.ShapeDtypeStruct((t, d), BF16),
                   jax.ShapeDtypeStruct((t, d), BF16),
                   jax.ShapeDtypeStruct((bsz, H_B, seq // tq, DV_B, tq), BF16),
                   jax.ShapeDtypeStruct((bsz, H_B, seq // tkc, DV_B, tkc), BF16)],
        name="in_projection",
        compiler_params=_cparams(1),
    )(x2d, sc, sh, g, w_nat, w_t)
    return outs


def _swa_kernel(sink_ref, q_ref, kc_ref, kp_ref, vc_ref, vp_ref, bias_ref, o_ref):
    i = pl.program_id(1)
    nsub = q_ref.shape[1] // BLOCK
    lane_k = lax.broadcasted_iota(I32, (2 * BLOCK, LANES), 1)
    lane_s = lax.broadcasted_iota(I32, (BLOCK, 2 * BLOCK), 1)
    first_mask = jnp.where((i == 0) & (lane_s < BLOCK), MASK_VALUE, 0.0).astype(F32)
    zero_b = jnp.zeros((2 * BLOCK, LANES), BF16)
    for j in range(nsub):
        rows = slice(j * BLOCK, (j + 1) * BLOCK)
        if j == 0:
            kband = jnp.concatenate([kp_ref[0], kc_ref[0, rows, :]], axis=0)
            vband = jnp.concatenate([vp_ref[0], vc_ref[0, rows, :]], axis=0)
        else:
            kband = kc_ref[0, (j - 1) * BLOCK:(j + 1) * BLOCK, :]
            vband = vc_ref[0, (j - 1) * BLOCK:(j + 1) * BLOCK, :]
        for kv in range(KV_A):
            kslab = kband[:, kv * LANES:(kv + 1) * LANES]
            vslab = vband[:, kv * LANES:(kv + 1) * LANES]
            kpad = (jnp.where(lane_k < DH_A, kslab, zero_b), jnp.where(lane_k >= DH_A, kslab, zero_b))
            vpad = (jnp.where(lane_k < DH_A, vslab, zero_b), jnp.where(lane_k >= DH_A, vslab, zero_b))
            for pp in range(G_A // 2):
                pair = kv * (G_A // 2) + pp
                qslab = q_ref[0, rows, pair * LANES:(pair + 1) * LANES]
                o_slab = jnp.zeros((BLOCK, LANES), F32)
                for par in range(2):
                    head = 2 * pair + par
                    s = lax.dot_general(qslab, kpad[par], (((1,), (1,)), ((), ())),
                                        preferred_element_type=F32)
                    s = s + bias_ref[head]
                    if j == 0:
                        s = s + first_mask
                    sink = sink_ref[head]
                    m = jnp.maximum(jnp.max(s, axis=-1, keepdims=True), sink)
                    e = jnp.exp(s - m)
                    den = jnp.sum(e, axis=-1, keepdims=True) + jnp.exp(sink - m)
                    pv = jnp.dot(e.astype(BF16), vpad[par], preferred_element_type=F32)
                    o_slab = o_slab + pv / den
                o_ref[0, rows, pair * LANES:(pair + 1) * LANES] = o_slab.astype(BF16)


def _swa_attention(qa, ka2, va2, sinks, bias_a):
    bsz, seq, _ = qa.shape
    tm = TM_SWA
    nsub = tm // BLOCK
    cur = lambda b, i: (b, i, 0)
    prev = lambda b, i: (b, jnp.maximum(i * nsub - 1, 0), 0)
    return pl.pallas_call(
        _swa_kernel,
        grid=(bsz, seq // tm),
        in_specs=[pl.BlockSpec(memory_space=pltpu.SMEM),
                  pl.BlockSpec((1, tm, QA_W), cur),
                  pl.BlockSpec((1, tm, 2 * KA_W), cur),
                  pl.BlockSpec((1, BLOCK, 2 * KA_W), prev),
                  pl.BlockSpec((1, tm, 2 * VA_W), cur),
                  pl.BlockSpec((1, BLOCK, 2 * VA_W), prev),
                  pl.BlockSpec((H_A, BLOCK, 2 * BLOCK), lambda b, i: (0, 0, 0))],
        out_specs=pl.BlockSpec((1, tm, QA_W), cur),
        out_shape=jax.ShapeDtypeStruct((bsz, seq, QA_W), BF16),
        name="swa_attention",
        compiler_params=_cparams(2),
    )(sinks, qa, ka2, ka2, va2, va2, bias_a)


def _diffattn_kernel(lp_ref, qt_ref, k_ref, vt_ref, bias_ref, g_ref, o_ref, acc_ref, *s_bufs, tq, tkc, lam_init):
    i = pl.program_id(2)
    kc = len(s_bufs)
    r = tq // tkc
    last_chunk = vt_ref.shape[2] - 1
    qt = qt_ref[0, 0, 0]
    row = lax.broadcasted_iota(I32, qt.shape, 0)
    zero_q = jnp.zeros_like(qt)
    qboth = jnp.concatenate([jnp.where(row < DH_B, qt, zero_q), jnp.where(row >= DH_B, qt, zero_q)], axis=1)
    acc_ref[...] = jnp.zeros_like(acc_ref)

    def scores_chunk(c, blk, biased):
        src = jnp.minimum(blk, last_chunk)
        kblk = k_ref[0, pl.ds(pl.multiple_of(src * tkc, tkc), tkc), :]
        s = jnp.dot(kblk, qboth, preferred_element_type=F32)
        if biased:
            s = s + bias_ref[0, jnp.clip(blk - i * r + 2, 0, r + 2)]
        s_bufs[c][...] = s
        return jnp.max(s, axis=0, keepdims=True)

    def group(t, carry, nxt):
        m_old, l_old, mx = carry
        m_new = jnp.maximum(m_old, mx)
        alpha = jnp.exp2(m_old - m_new)
        l_new = alpha * l_old
        mx_next = jnp.full_like(mx, MASK_VALUE)
        pv = None
        for c in range(kc):
            p = jnp.exp2(s_bufs[c][...] - m_new)
            l_new = l_new + jnp.sum(p, axis=0, keepdims=True)
            vblk = vt_ref[0, 0, jnp.minimum(t * kc + c, last_chunk)]
            d = jnp.dot(vblk, p.astype(BF16), preferred_element_type=F32)
            pv = d if pv is None else pv + d
            if nxt is not None:
                mx_next = jnp.maximum(mx_next, scores_chunk(c, (t + 1) * kc + c, nxt == 'biased'))
        acc_ref[...] = alpha * acc_ref[...] + pv
        return m_new, l_new, mx_next

    n_groups = ((i + 1) * r + kc - 1) // kc
    n_plain = jnp.maximum(i * r - 1, 0) // kc
    split = jnp.maximum(n_plain - 1, 0)

    mx0 = jnp.full((1, 2 * tq), MASK_VALUE, F32)
    for c in range(kc):
        mx0 = jnp.maximum(mx0, scores_chunk(c, c, True))
    carry = (jnp.full((1, 2 * tq), MASK_VALUE, F32), jnp.zeros((1, 2 * tq), F32), mx0)
    carry = lax.fori_loop(0, split, lambda t, cr: group(t, cr, 'plain'), carry)
    carry = lax.fori_loop(split, n_groups - 1, lambda t, cr: group(t, cr, 'biased'), carry)
    _, l_fin, _ = group(n_groups - 1, carry, None)

    lp = lp_ref[...]
    lam = (jnp.exp(jnp.sum(lp[0:1] * lp[1:2], axis=-1, keepdims=True))
           - jnp.exp(jnp.sum(lp[2:3] * lp[3:4], axis=-1, keepdims=True)) + lam_init)
    acc = acc_ref[...]
    o = acc[:, :tq] / l_fin[:, :tq] - lam * (acc[:, tq:] / l_fin[:, tq:])
    ms = jnp.mean(o * o, axis=0, keepdims=True)
    y = o * lax.rsqrt(ms + SUBLN_EPS) * g_ref[...] * (1.0 - lam_init)
    o_ref[0] = y.T.astype(BF16)


def _diff_attention(qbt, kb, vbt, bias_t, lp, g_b, lam_init):
    bsz, _, nqb, _, tq = qbt.shape
    nkb, tkc = vbt.shape[2], vbt.shape[4]
    seq = nqb * tq
    n_tiles = bias_t.shape[1]
    return pl.pallas_call(
        functools.partial(_diffattn_kernel, tq=tq, tkc=tkc, lam_init=lam_init),
        grid=(bsz, H_B, nqb),
        in_specs=[pl.BlockSpec((4, DH_B), lambda b, h, i: (0, 0)),
                  pl.BlockSpec((1, 1, 1, DV_B, tq), lambda b, h, i: (b, h, i, 0, 0)),
                  pl.BlockSpec((1, seq, DV_B), lambda b, h, i: (b, 0, h)),
                  pl.BlockSpec((1, 1, nkb, DV_B, tkc), lambda b, h, i: (b, h, 0, 0, 0)),
                  pl.BlockSpec((1, n_tiles, tkc, 2 * tq), lambda b, h, i: (h, 0, 0, 0)),
                  pl.BlockSpec((DV_B, tq), lambda b, h, i: (0, 0))],
        out_specs=pl.BlockSpec((1, tq, DV_B), lambda b, h, i: (b, i, h)),
        out_shape=jax.ShapeDtypeStruct((bsz, seq, H_B * DV_B), BF16),
        scratch_shapes=[pltpu.VMEM((DV_B, 2 * tq), F32)] + [pltpu.VMEM((tkc, 2 * tq), F32)] * KC_DIFF,
        name="diff_attention",
        compiler_params=_cparams(3),
    )(lp, qbt, kb, vbt, bias_t, g_b)


def _postmix_kernel(x_ref, oa_ref, ob_ref, ga_ref, gb_ref, wo_ref, gpost_ref, gt_ref, gpre_ref, sc_ref, sh_ref,
                    wr_ref, br_ref,
                    x1_ref, h2r_ref, idx_ref, w_ref, rank_ref, cnt_ref, carry_ref):
    step = pl.program_id(0)
    tm = x_ref.shape[0]

    @pl.when(step == 0)
    def _():
        carry_ref[...] = jnp.zeros_like(carry_ref)

    ya = jnp.dot(oa_ref[...], wo_ref[:QA_W, :], preferred_element_type=F32)
    yb = jnp.dot(ob_ref[...], wo_ref[QA_W:, :], preferred_element_type=F32)
    y = ga_ref[...].astype(F32) * ya + gb_ref[...].astype(F32) * yb
    ms = jnp.mean(y * y, axis=-1, keepdims=True)
    x1 = x_ref[...] + gt_ref[0] * (y * lax.rsqrt(ms + NORM_EPS) * gpost_ref[...])
    x1_ref[...] = x1
    ms1 = jnp.mean(x1 * x1, axis=-1, keepdims=True)
    h2 = x1 * lax.rsqrt(ms1 + NORM_EPS) * gpre_ref[...]
    h2 = h2 * (1.0 + sc_ref[0]) + sh_ref[0]
    for c in range(ROW_CHUNKS):
        h2r_ref[pl.ds(c, tm, stride=ROW_CHUNKS), :] = h2[:, c * LANES:(c + 1) * LANES]

    logits = lax.dot_general(wr_ref[...], h2, (((1,), (1,)), ((), ())), preferred_element_type=F32,
                             precision=lax.Precision.HIGHEST) + br_ref[...]
    erow = lax.broadcasted_iota(I32, logits.shape, 0)
    vals, sels = [], []
    cur = logits
    for k in range(TOP_K):
        mv = jnp.max(cur, axis=0, keepdims=True)
        idx = jnp.min(jnp.where(cur == mv, erow, N_EXPERTS), axis=0, keepdims=True)
        sel = erow == idx
        vals.append(mv)
        sels.append(sel)
        idx_ref[k:k + 1, :] = idx
        cur = jnp.where(sel, -jnp.inf, cur)
    es = [jnp.exp(v - vals[0]) for v in vals]
    den = es[0] + es[1] + es[2] + es[3]
    for k in range(TOP_K):
        w_ref[k:k + 1, :] = es[k] / den

    onehot = jnp.zeros(logits.shape, F32)
    for k in range(TOP_K):
        onehot = onehot + jnp.where(sels[k], 1.0, 0.0)
    srow = lax.broadcasted_iota(I32, (tm, tm), 0)
    scol = lax.broadcasted_iota(I32, (tm, tm), 1)
    upper = jnp.where(srow <= scol, 1.0, 0.0).astype(BF16)
    cum = jnp.dot(onehot.astype(BF16), upper, preferred_element_type=F32)
    carry = carry_ref[:, 0:1]
    excl = cum - onehot + carry
    for k in range(TOP_K):
        rank_ref[k:k + 1, :] = jnp.sum(jnp.where(sels[k], excl, 0.0), axis=0, keepdims=True).astype(I32)
    new_carry = carry + jnp.sum(onehot, axis=1, keepdims=True)
    carry_ref[...] = jnp.broadcast_to(new_carry, carry_ref.shape)
    cnt_ref[...] = jnp.broadcast_to(new_carry, cnt_ref.shape)


def _post_mix(x2d, oa, ob, ga, gb, wo, g_post, gt, g_pre, sc, sh, wr_t, br, seq):
    t, d = x2d.shape
    tm = TM_PROJ
    spb = seq // tm
    row = lambda i: (i, 0)
    bat = lambda i: (i // spb, 0, 0)
    const = lambda i: (0, 0)
    col = lambda i: (0, i)
    return pl.pallas_call(
        _postmix_kernel,
        grid=(t // tm,),
        in_specs=[pl.BlockSpec((tm, d), row),
                  pl.BlockSpec((tm, QA_W), row),
                  pl.BlockSpec((tm, H_B * DV_B), row),
                  pl.BlockSpec((tm, d), row),
                  pl.BlockSpec((tm, d), row),
                  pl.BlockSpec((QA_W + H_B * DV_B, d), const),
                  pl.BlockSpec((1, d), const),
                  pl.BlockSpec((1, 1, d), bat),
                  pl.BlockSpec((1, d), const),
                  pl.BlockSpec((1, 1, d), bat),
                  pl.BlockSpec((1, 1, d), bat),
                  pl.BlockSpec((N_EXPERTS, d), const),
                  pl.BlockSpec((N_EXPERTS, 1), const)],
        out_specs=[pl.BlockSpec((tm, d), row),
                   pl.BlockSpec((tm * ROW_CHUNKS, LANES), row),
                   pl.BlockSpec((TOP_K, tm), col),
                   pl.BlockSpec((TOP_K, tm), col),
                   pl.BlockSpec((TOP_K, tm), col),
                   pl.BlockSpec((N_EXPERTS, LANES), const)],
        out_shape=[jax.ShapeDtypeStruct((t, d), F32),
                   jax.ShapeDtypeStruct((t * ROW_CHUNKS, LANES), F32),
                   jax.ShapeDtypeStruct((TOP_K, t), I32),
                   jax.ShapeDtypeStruct((TOP_K, t), F32),
                   jax.ShapeDtypeStruct((TOP_K, t), I32),
                   jax.ShapeDtypeStruct((N_EXPERTS, LANES), F32)],
        scratch_shapes=[pltpu.VMEM((N_EXPERTS, LANES), F32)],
        name="post_mix_router",
        compiler_params=_cparams(1),
    )(x2d, oa, ob, ga, gb, wo, g_post, gt, g_pre, sc, sh, wr_t, br)


def _row_copy(src, src_row, dst, dst_row, sem):
    return pltpu.make_async_copy(src.at[pl.ds(pl.multiple_of(src_row * ROW_CHUNKS, ROW_CHUNKS), ROW_CHUNKS)],
                                 dst.at[pl.ds(pl.multiple_of(dst_row * ROW_CHUNKS, ROW_CHUNKS), ROW_CHUNKS)], sem)


def _dispatch_kernel(dest_ref, h2r_ref, xs_in_hbm, xs_hbm, dsm, sem_idx, sem, *, tg):
    del xs_in_hbm
    idx_copy = pltpu.make_async_copy(dest_ref.at[0, 0], dsm, sem_idx)
    idx_copy.start()
    idx_copy.wait()

    def issue(it, carry):
        for u in range(DMA_UNROLL):
            tok = it * DMA_UNROLL + u
            for k in range(TOP_K):
                _row_copy(h2r_ref, tok, xs_hbm, dsm[k * tg + tok], sem).start()
        return carry

    lax.fori_loop(0, tg // DMA_UNROLL, issue, 0)

    def drain(it, carry):
        for _ in range(DMA_UNROLL * TOP_K):
            _row_copy(h2r_ref, 0, xs_hbm, 0, sem).wait()
        return carry

    lax.fori_loop(0, tg // DMA_UNROLL, drain, 0)


def _dispatch(dest_blocks, h2r, xs_zero, tg):
    nsteps = dest_blocks.shape[0]
    return pl.pallas_call(
        functools.partial(_dispatch_kernel, tg=tg),
        grid=(nsteps,),
        in_specs=[pl.BlockSpec((1, 1, TOP_K * tg), lambda i: (i, 0, 0)),
                  pl.BlockSpec((tg * ROW_CHUNKS, LANES), lambda i: (i, 0)),
                  pl.BlockSpec(memory_space=pl.ANY)],
        out_specs=pl.BlockSpec(memory_space=pl.ANY),
        out_shape=jax.ShapeDtypeStruct(xs_zero.shape, xs_zero.dtype),
        scratch_shapes=[pltpu.SMEM((TOP_K * tg,), I32), pltpu.SemaphoreType.DMA, pltpu.SemaphoreType.DMA],
        input_output_aliases={2: 0},
        name="moe_dispatch",
        compiler_params=_cparams(1),
    )(dest_blocks, h2r, xs_zero)


def _expert_kernel(be_ref, nu_ref, xs_ref, wgu_ref, bgu_ref, wd_ref, bd_ref, ys_ref, *, bm):
    i = pl.program_id(0)

    @pl.when(i < nu_ref[0])
    def _():
        x = jnp.concatenate([xs_ref[pl.ds(c, bm, stride=ROW_CHUNKS), :] for c in range(ROW_CHUNKS)], axis=-1)
        gu = jnp.dot(x.astype(BF16), wgu_ref[0], preferred_element_type=F32) + bgu_ref[0]
        gate = jnp.minimum(gu[:, :D_FF], SWIGLU_LIMIT)
        up = jnp.clip(gu[:, D_FF:], -SWIGLU_LIMIT, SWIGLU_LIMIT)
        act = gate * _sigmoid(SWIGLU_ALPHA * gate) * (up + 1.0)
        y = jnp.dot(act.astype(BF16), wd_ref[0], preferred_element_type=F32) + bd_ref[0]
        for c in range(ROW_CHUNKS):
            ys_ref[pl.ds(c, bm, stride=ROW_CHUNKS), :] = y[:, c * LANES:(c + 1) * LANES]

    @pl.when(i >= nu_ref[0])
    def _():
        ys_ref[...] = jnp.zeros_like(ys_ref)


def _experts(block_e, n_used, xs, wgu, bgu, wd, bd):
    bm = BM_EXPERT
    n_blocks = xs.shape[0] // (bm * ROW_CHUNKS)
    d = D_MODEL
    grid_spec = pltpu.PrefetchScalarGridSpec(
        num_scalar_prefetch=2,
        grid=(n_blocks,),
        in_specs=[pl.BlockSpec((bm * ROW_CHUNKS, LANES), lambda i, be, nu: (i, 0)),
                  pl.BlockSpec((1, d, 2 * D_FF), lambda i, be, nu: (be[i], 0, 0)),
                  pl.BlockSpec((1, 1, 2 * D_FF), lambda i, be, nu: (be[i], 0, 0)),
                  pl.BlockSpec((1, D_FF, d), lambda i, be, nu: (be[i], 0, 0)),
                  pl.BlockSpec((1, 1, d), lambda i, be, nu: (be[i], 0, 0))],
        out_specs=pl.BlockSpec((bm * ROW_CHUNKS, LANES), lambda i, be, nu: (i, 0)),
    )
    return pl.pallas_call(
        functools.partial(_expert_kernel, bm=bm),
        grid_spec=grid_spec,
        out_shape=jax.ShapeDtypeStruct(xs.shape, F32),
        name="moe_experts",
        compiler_params=_cparams(1),
    )(block_e, n_used, xs, wgu, bgu, wd, bd)


def _combine_kernel(dest_ref, w_ref, x_ref, gt_ref, g_ref, ys_hbm, o_ref, dsm, buf, sem_idx, sem, *, tc):
    idx_copy = pltpu.make_async_copy(dest_ref.at[0, 0], dsm, sem_idx)
    idx_copy.start()
    idx_copy.wait()

    def issue(it, carry):
        for u in range(DMA_UNROLL):
            tok = it * DMA_UNROLL + u
            for k in range(TOP_K):
                _row_copy(ys_hbm, dsm[k * tc + tok], buf, k * tc + tok, sem).start()
        return carry

    lax.fori_loop(0, tc // DMA_UNROLL, issue, 0)

    def drain(it, carry):
        for _ in range(DMA_UNROLL * TOP_K):
            _row_copy(ys_hbm, 0, buf, 0, sem).wait()
        return carry

    lax.fori_loop(0, tc // DMA_UNROLL, drain, 0)

    y = jnp.zeros((tc, D_MODEL), F32)
    for k in range(TOP_K):
        yk = jnp.concatenate([buf[pl.ds(k * tc * ROW_CHUNKS + c, tc, stride=ROW_CHUNKS), :]
                              for c in range(ROW_CHUNKS)], axis=-1)
        y = y + w_ref[:, k:k + 1] * yk
    ms = jnp.mean(y * y, axis=-1, keepdims=True)
    o_ref[...] = x_ref[...] + gt_ref[0] * (y * lax.rsqrt(ms + NORM_EPS) * g_ref[...])


def _combine(dest_blocks, w_tok, x1, gt, g_post, ys, seq, tc):
    t, d = x1.shape
    spb = seq // tc
    return pl.pallas_call(
        functools.partial(_combine_kernel, tc=tc),
        grid=(t // tc,),
        in_specs=[pl.BlockSpec((1, 1, TOP_K * tc), lambda i: (i, 0, 0)),
                  pl.BlockSpec((tc, TOP_K), lambda i: (i, 0)),
                  pl.BlockSpec((tc, d), lambda i: (i, 0)),
                  pl.BlockSpec((1, 1, d), lambda i: (i // spb, 0, 0)),
                  pl.BlockSpec((1, d), lambda i: (0, 0)),
                  pl.BlockSpec(memory_space=pl.ANY)],
        out_specs=pl.BlockSpec((tc, d), lambda i: (i, 0)),
        out_shape=jax.ShapeDtypeStruct((t, d), F32),
        scratch_shapes=[pltpu.SMEM((TOP_K * tc,), I32),
                        pltpu.VMEM((TOP_K * tc * ROW_CHUNKS, LANES), F32),
                        pltpu.SemaphoreType.DMA, pltpu.SemaphoreType.DMA],
        name="moe_combine",
        compiler_params=_cparams(1),
    )(dest_blocks, w_tok, x1, gt, g_post, ys)


def _dest_blocks(dest, step_tokens):
    k, t = dest.shape
    return dest.reshape(k, t // step_tokens, step_tokens).transpose(1, 0, 2).reshape(t // step_tokens, 1,
                                                                                      k * step_tokens)


def _bias_tables(rel_bias_table, tq, tkc):
    def lookup(dist, table):
        onehot = np.eye(N_BUCKETS, dtype=np.float32)[_t5_bucket_np(dist)]
        return jnp.einsum('...n,nh->h...', onehot, table, precision=lax.Precision.HIGHEST)

    qi = np.arange(BLOCK)[:, None]
    kj = np.arange(2 * BLOCK)[None, :]
    dist = qi + BLOCK - kj
    valid = (dist >= 0) & (dist < WINDOW)
    bias_a = jnp.where(valid[None], lookup(dist, rel_bias_table[:, :H_A]), MASK_VALUE)
    kk = np.arange(tkc)[:, None]
    qq = np.arange(tq)[None, :]
    table_b = rel_bias_table[:, H_A:]
    table_b = (table_b - table_b[N_BUCKETS - 1:]) * LOG2E
    tiles = [jnp.zeros((H_B, tkc, tq), F32)]
    for rel in range(-1, tq // tkc):
        d = qq - (rel * tkc + kk)
        tiles.append(jnp.where((d >= 0)[None], lookup(d, table_b), MASK_VALUE))
    tiles.append(jnp.full((H_B, tkc, tq), MASK_VALUE, F32))
    bias_t = jnp.stack([jnp.concatenate([t, t], axis=-1) for t in tiles], axis=1)
    return bias_a.astype(F32), bias_t.astype(F32)


def _prep_in_weights(w_in_l):
    qa, ka, va, qb, kb, vb, gza, gzb = jnp.split(w_in_l, [int(v) for v in np.cumsum(SPLIT_SIZES)[:-1]], axis=-1)

    def dup(w):
        return jnp.concatenate([w[:, :DH_A], w[:, :DH_A], w[:, DH_A:], w[:, DH_A:]], axis=-1)

    w_nat = jnp.concatenate([qa * DH_A ** -0.5, dup(ka), dup(va), kb, gza, gzb], axis=-1).astype(BF16)
    w_t = jnp.concatenate([qb * (DH_B ** -0.5 * LOG2E), vb], axis=-1).T.astype(BF16)
    return w_nat, w_t


def kernel(x, c, rel_bias_table, w_ada, b_ada, g_mix_pre, g_mix_post, g_ffn_pre, g_ffn_post, w_in, w_out,
           attn_sinks, diff_lambda, diff_subln_g, w_router, b_router, w_gate_up, b_gate_up, w_down, b_down):
    bsz, seq, d = x.shape
    depth = w_ada.shape[0]
    t = bsz * seq
    tq, tkc = TQ_DIFF, TKC_DIFF
    assert d == D_MODEL and tkc >= BLOCK and tq % tkc == 0
    assert TM_PROJ % tq == 0 and TM_PROJ % tkc == 0 and (seq // tkc) % KC_DIFF == 0
    assert seq % TM_PROJ == 0 and seq % TM_SWA == 0
    assert seq % TG_DISPATCH == 0 and seq % TC_COMBINE == 0

    mod = _modulation(c, w_ada, b_ada)
    bias_a, bias_t = _bias_tables(rel_bias_table, tq, tkc)
    n_rows = t * TOP_K + N_EXPERTS * BM_EXPERT
    n_blocks = n_rows // BM_EXPERT

    x2d = x.reshape(t, d)
    for l in range(depth):
        sh_m, sc_m, gt_m, sh_f, sc_f, gt_f = [v[:, None, :] for v in jnp.split(mod[l], 6, axis=-1)]
        lam_init = 0.8 - 0.6 * math.exp(-0.3 * l)
        w_nat, w_t = _prep_in_weights(w_in[l])
        qa, ka2, va2, kb, ga, gb, qbt, vbt = _in_projection(
            x2d, sc_m, sh_m, g_mix_pre[l][None, :], w_nat, w_t, bsz, seq)
        o_a = _swa_attention(qa.reshape(bsz, seq, QA_W), ka2.reshape(bsz, seq, 2 * KA_W),
                             va2.reshape(bsz, seq, 2 * VA_W), attn_sinks[l], bias_a)
        g_b = jnp.broadcast_to(diff_subln_g[l][:, None], (DV_B, tq)).astype(F32)
        o_b = _diff_attention(qbt, kb.reshape(bsz, seq, KB_W), vbt, bias_t, diff_lambda[l], g_b, lam_init)
        x1, h2r, top_idx, top_w, rank, counts = _post_mix(
            x2d, o_a.reshape(t, QA_W), o_b.reshape(t, H_B * DV_B), ga, gb, w_out[l].astype(BF16),
            g_mix_post[l][None, :], gt_m, g_ffn_pre[l][None, :], sc_f, sh_f,
            w_router[l].T, b_router[l][:, None], seq)

        cnt = counts[:, 0].astype(I32)
        padded = (cnt + BM_EXPERT - 1) // BM_EXPERT * BM_EXPERT
        pends = jnp.cumsum(padded)
        pstarts = pends - padded
        experts = jnp.arange(N_EXPERTS, dtype=I32)
        dest = jnp.sum(jnp.where(top_idx[..., None] == experts, pstarts, 0), axis=-1) + rank
        block_start = jnp.arange(n_blocks, dtype=I32) * BM_EXPERT
        block_e = jnp.minimum(jnp.sum((pends[None, :] <= block_start[:, None]).astype(I32), axis=1),
                              N_EXPERTS - 1).astype(I32)
        n_used = (pends[-1:] // BM_EXPERT).astype(I32)

        xs = _dispatch(_dest_blocks(dest, TG_DISPATCH), h2r,
                       jnp.zeros((n_rows * ROW_CHUNKS, LANES), F32), TG_DISPATCH)
        ys = _experts(block_e, n_used, xs, w_gate_up[l].astype(BF16),
                      b_gate_up[l][:, None, :], w_down[l].astype(BF16), b_down[l][:, None, :])
        x2d = _combine(_dest_blocks(dest, TC_COMBINE), top_w.T, x1, gt_f, g_ffn_post[l][None, :], ys,
                       seq, TC_COMBINE)
    return x2d.reshape(bsz, seq, d)
```

```python
import functools
import math

import numpy as np
import jax
import jax.numpy as jnp
from jax import lax
from jax.experimental import pallas as pl
from jax.experimental.pallas import tpu as pltpu

F32 = jnp.float32
BF16 = jnp.bfloat16
I32 = jnp.int32

D_MODEL = 1024
BLOCK = 128
WINDOW = 128
H_A, KV_A, DH_A = 8, 2, 64
G_A = H_A // KV_A
H_B, DH_B = 4, 64
DV_B = 2 * DH_B
N_BUCKETS = 32
MAX_EXACT = N_BUCKETS // 2
REL_MAX_DIST = 128
N_EXPERTS = 32
TOP_K = 4
D_FF = D_MODEL
SWIGLU_ALPHA = 1.702
SWIGLU_LIMIT = 7.0
NORM_EPS = 1e-6
SUBLN_EPS = 1e-5
QA_W = H_A * DH_A
KA_W = KV_A * DH_A
VA_W = KV_A * DH_A
QB_W = H_B * 2 * DH_B
KB_W = H_B * 2 * DH_B
VB_W = H_B * 2 * DH_B
SPLIT_SIZES = (QA_W, KA_W, VA_W, QB_W, KB_W, VB_W, D_MODEL, D_MODEL)

LANES = 128
SUBLANES = 8
ROW_CHUNKS = D_MODEL // LANES
MASK_VALUE = -1e30

TM_PROJ = 512
TQ_DIFF = 512
TKC_DIFF = 256
KC_DIFF = 4
LOG2E = math.log2(math.e)
TM_SWA = 512
BM_EXPERT = 256
TG_DISPATCH = 512
TC_COMBINE = 256
DMA_UNROLL = 8
VMEM_LIMIT = 48 * 1024 * 1024
VMEM_LIMIT_EXPERTS = 56 * 1024 * 1024


def _cparams(n_axes):
    return pltpu.CompilerParams(dimension_semantics=("arbitrary",) * n_axes,
                                vmem_limit_bytes=VMEM_LIMIT)


def _sigmoid(x):
    return 1.0 / (1.0 + jnp.exp(-x))


def _t5_bucket_np(dist):
    n = np.maximum(dist, 0)
    nf = np.maximum(n, 1).astype(np.float32)
    large = MAX_EXACT + (np.log(nf / np.float32(MAX_EXACT)) / np.float32(math.log(REL_MAX_DIST / MAX_EXACT))
                         * np.float32(N_BUCKETS - MAX_EXACT)).astype(np.int32)
    large = np.minimum(large, N_BUCKETS - 1)
    return np.where(n < MAX_EXACT, n, large)


def _mod_kernel(c_ref, w_ref, b_ref, o_ref):
    c = c_ref[...]
    ca = c * _sigmoid(c)
    o_ref[0] = jnp.dot(ca, w_ref[0], preferred_element_type=F32,
                       precision=lax.Precision.HIGHEST) + b_ref[0]


def _modulation(c, w_ada, b_ada):
    depth, d, n = w_ada.shape
    bsz = c.shape[0]
    rows = -(-bsz // SUBLANES) * SUBLANES
    c_pad = jnp.zeros((rows, d), F32).at[:bsz].set(c)
    tn = 1536
    out = pl.pallas_call(
        _mod_kernel,
        grid=(depth, n // tn),
        in_specs=[pl.BlockSpec((rows, d), lambda l, j: (0, 0)),
                  pl.BlockSpec((1, d, tn), lambda l, j: (l, 0, j)),
                  pl.BlockSpec((1, 1, tn), lambda l, j: (l, 0, j))],
        out_specs=pl.BlockSpec((1, rows, tn), lambda l, j: (l, 0, j)),
        out_shape=jax.ShapeDtypeStruct((depth, rows, n), F32),
        name="adaln_modulation",
        compiler_params=_cparams(2),
    )(c_pad, w_ada, b_ada.reshape(depth, 1, n))
    return out[:, :bsz]


N_NAT = QA_W + 2 * KA_W + 2 * VA_W + KB_W + 2 * D_MODEL
O_QA, O_KA, O_VA, O_KB, O_GA, O_GB = 0, 512, 768, 1024, 1536, 2560


def _inproj_kernel(x_ref, sc_ref, sh_ref, g_ref, w_ref, wt_ref,
                   qa_ref, ka_ref, va_ref, kb_ref, ga_ref, gb_ref, qbt_ref, vbt_ref, *, tq, tkc):
    x = x_ref[...]
    ms = jnp.mean(x * x, axis=-1, keepdims=True)
    h = x * lax.rsqrt(ms + NORM_EPS) * g_ref[...]
    h = h * (1.0 + sc_ref[0]) + sh_ref[0]
    hb = h.astype(BF16)
    tm = x.shape[0]

    def proj(a, b):
        return jnp.dot(hb, w_ref[:, a:b], preferred_element_type=F32)

    qa_ref[...] = proj(O_QA, O_KA).astype(BF16)
    ka_ref[...] = proj(O_KA, O_VA).astype(BF16)
    va_ref[...] = proj(O_VA, O_KB).astype(BF16)
    kb_ref[...] = proj(O_KB, O_GA).astype(BF16)
    ga_ref[...] = _sigmoid(proj(O_GA, O_GB)).astype(BF16)
    gb_ref[...] = _sigmoid(proj(O_GB, N_NAT)).astype(BF16)
    for t_ref, base, tb in ((qbt_ref, 0, tq), (vbt_ref, QB_W, tkc)):
        rt = lax.dot_general(wt_ref[base: base + H_B * DV_B, :], hb,
                             (((1,), (1,)), ((), ())), preferred_element_type=F32)
        for hh in range(H_B):
            for j in range(tm // tb):
                t_ref[0, hh, j] = rt[hh * DV_B:(hh + 1) * DV_B, j * tb:(j + 1) * tb].astype(BF16)


def _in_projection(x2d, sc, sh, g, w_nat, w_t, bsz, seq):
    t, d = x2d.shape
    tm, tq, tkc = TM_PROJ, TQ_DIFF, TKC_DIFF
    spb = seq // tm
    row = lambda i: (i, 0)
    bat = lambda i: (i // spb, 0, 0)
    tr = lambda i: (i // spb, 0, i % spb, 0, 0)
    outs = pl.pallas_call(
        functools.partial(_inproj_kernel, tq=tq, tkc=tkc),
        grid=(t // tm,),
        in_specs=[pl.BlockSpec((tm, d), row),
                  pl.BlockSpec((1, 1, d), bat),
                  pl.BlockSpec((1, 1, d), bat),
                  pl.BlockSpec((1, d), lambda i: (0, 0)),
                  pl.BlockSpec((d, N_NAT), lambda i: (0, 0)),
                  pl.BlockSpec((QB_W + VB_W, d), lambda i: (0, 0))],
        out_specs=[pl.BlockSpec((tm, QA_W), row),
                   pl.BlockSpec((tm, 2 * KA_W), row),
                   pl.BlockSpec((tm, 2 * VA_W), row),
                   pl.BlockSpec((tm, KB_W), row),
                   pl.BlockSpec((tm, d), row),
                   pl.BlockSpec((tm, d), row),
                   pl.BlockSpec((1, H_B, tm // tq, DV_B, tq), tr),
                   pl.BlockSpec((1, H_B, tm // tkc, DV_B, tkc), tr)],
        out_shape=[jax.ShapeDtypeStruct((t, QA_W), BF16),
                   jax.ShapeDtypeStruct((t, 2 * KA_W), BF16),
                   jax.ShapeDtypeStruct((t, 2 * VA_W), BF16),
                   jax.ShapeDtypeStruct((t, KB_W), BF16),
                   jax.ShapeDtypeStruct((t, d), BF16),
                   jax.ShapeDtypeStruct((t, d), BF16),
                   jax.ShapeDtypeStruct((bsz, H_B, seq // tq, DV_B, tq), BF16),
                   jax.ShapeDtypeStruct((bsz, H_B, seq // tkc, DV_B, tkc), BF16)],
        name="in_projection",
        compiler_params=_cparams(1),
    )(x2d, sc, sh, g, w_nat, w_t)
    return outs


def _swa_kernel(sink_ref, q_ref, kc_ref, kp_ref, vc_ref, vp_ref, bias_ref, o_ref):
    i = pl.program_id(1)
    nsub = q_ref.shape[1] // BLOCK
    lane_k = lax.broadcasted_iota(I32, (2 * BLOCK, LANES), 1)
    lane_s = lax.broadcasted_iota(I32, (BLOCK, 2 * BLOCK), 1)
    first_mask = jnp.where((i == 0) & (lane_s < BLOCK), MASK_VALUE, 0.0).astype(F32)
    zero_b = jnp.zeros((2 * BLOCK, LANES), BF16)
    for j in range(nsub):
        rows = slice(j * BLOCK, (j + 1) * BLOCK)
        if j == 0:
            kband = jnp.concatenate([kp_ref[0], kc_ref[0, rows, :]], axis=0)
            vband = jnp.concatenate([vp_ref[0], vc_ref[0, rows, :]], axis=0)
        else:
            kband = kc_ref[0, (j - 1) * BLOCK:(j + 1) * BLOCK, :]
            vband = vc_ref[0, (j - 1) * BLOCK:(j + 1) * BLOCK, :]
        for kv in range(KV_A):
            kslab = kband[:, kv * LANES:(kv + 1) * LANES]
            vslab = vband[:, kv * LANES:(kv + 1) * LANES]
            kpad = (jnp.where(lane_k < DH_A, kslab, zero_b), jnp.where(lane_k >= DH_A, kslab, zero_b))
            vpad = (jnp.where(lane_k < DH_A, vslab, zero_b), jnp.where(lane_k >= DH_A, vslab, zero_b))
            for pp in range(G_A // 2):
                pair = kv * (G_A // 2) + pp
                qslab = q_ref[0, rows, pair * LANES:(pair + 1) * LANES]
                o_slab = jnp.zeros((BLOCK, LANES), F32)
                for par in range(2):
                    head = 2 * pair + par
                    s = lax.dot_general(qslab, kpad[par], (((1,), (1,)), ((), ())),
                                        preferred_element_type=F32)
                    s = s + bias_ref[head]
                    if j == 0:
                        s = s + first_mask
                    sink = sink_ref[head]
                    m = jnp.maximum(jnp.max(s, axis=-1, keepdims=True), sink)
                    e = jnp.exp(s - m)
                    den = jnp.sum(e, axis=-1, keepdims=True) + jnp.exp(sink - m)
                    pv = jnp.dot(e.astype(BF16), vpad[par], preferred_element_type=F32)
                    o_slab = o_slab + pv / den
                o_ref[0, rows, pair * LANES:(pair + 1) * LANES] = o_slab.astype(BF16)


def _swa_attention(qa, ka2, va2, sinks, bias_a):
    bsz, seq, _ = qa.shape
    tm = TM_SWA
    nsub = tm // BLOCK
    cur = lambda b, i: (b, i, 0)
    prev = lambda b, i: (b, jnp.maximum(i * nsub - 1, 0), 0)
    return pl.pallas_call(
        _swa_kernel,
        grid=(bsz, seq // tm),
        in_specs=[pl.BlockSpec(memory_space=pltpu.SMEM),
                  pl.BlockSpec((1, tm, QA_W), cur),
                  pl.BlockSpec((1, tm, 2 * KA_W), cur),
                  pl.BlockSpec((1, BLOCK, 2 * KA_W), prev),
                  pl.BlockSpec((1, tm, 2 * VA_W), cur),
                  pl.BlockSpec((1, BLOCK, 2 * VA_W), prev),
                  pl.BlockSpec((H_A, BLOCK, 2 * BLOCK), lambda b, i: (0, 0, 0))],
        out_specs=pl.BlockSpec((1, tm, QA_W), cur),
        out_shape=jax.ShapeDtypeStruct((bsz, seq, QA_W), BF16),
        name="swa_attention",
        compiler_params=_cparams(2),
    )(sinks, qa, ka2, ka2, va2, va2, bias_a)


def _diffattn_kernel(lp_ref, qt_ref, k_ref, vt_ref, bias_ref, g_ref, o_ref, acc_ref, *s_bufs, tq, tkc, lam_init):
    i = pl.program_id(2)
    kc = len(s_bufs)
    r = tq // tkc
    last_chunk = vt_ref.shape[2] - 1
    qt = qt_ref[0, 0, 0]
    row = lax.broadcasted_iota(I32, qt.shape, 0)
    zero_q = jnp.zeros_like(qt)
    qboth = jnp.concatenate([jnp.where(row < DH_B, qt, zero_q), jnp.where(row >= DH_B, qt, zero_q)], axis=1)
    acc_ref[...] = jnp.zeros_like(acc_ref)

    def scores_chunk(c, blk, biased):
        src = jnp.minimum(blk, last_chunk)
        kblk = k_ref[0, pl.ds(pl.multiple_of(src * tkc, tkc), tkc), :]
        s = jnp.dot(kblk, qboth, preferred_element_type=F32)
        if biased:
            s = s + bias_ref[0, jnp.clip(blk - i * r + 2, 0, r + 2)]
        s_bufs[c][...] = s
        return jnp.max(s, axis=0, keepdims=True)

    def group(t, carry, nxt):
        m_old, l_old, mx = carry
        m_new = jnp.maximum(m_old, mx)
        alpha = jnp.exp2(m_old - m_new)
        l_new = alpha * l_old
        mx_next = jnp.full_like(mx, MASK_VALUE)
        pv = None
        for c in range(kc):
            p = jnp.exp2(s_bufs[c][...] - m_new)
            l_new = l_new + jnp.sum(p, axis=0, keepdims=True)
            vblk = vt_ref[0, 0, jnp.minimum(t * kc + c, last_chunk)]
            d = jnp.dot(vblk, p.astype(BF16), preferred_element_type=F32)
            pv = d if pv is None else pv + d
            if nxt is not None:
                mx_next = jnp.maximum(mx_next, scores_chunk(c, (t + 1) * kc + c, nxt == 'biased'))
        acc_ref[...] = alpha * acc_ref[...] + pv
        return m_new, l_new, mx_next

    n_groups = ((i + 1) * r + kc - 1) // kc
    n_plain = jnp.maximum(i * r - 1, 0) // kc
    split = jnp.maximum(n_plain - 1, 0)

    mx0 = jnp.full((1, 2 * tq), MASK_VALUE, F32)
    for c in range(kc):
        mx0 = jnp.maximum(mx0, scores_chunk(c, c, True))
    carry = (jnp.full((1, 2 * tq), MASK_VALUE, F32), jnp.zeros((1, 2 * tq), F32), mx0)
    carry = lax.fori_loop(0, split, lambda t, cr: group(t, cr, 'plain'), carry)
    carry = lax.fori_loop(split, n_groups - 1, lambda t, cr: group(t, cr, 'biased'), carry)
    _, l_fin, _ = group(n_groups - 1, carry, None)

    lp = lp_ref[...]
    lam = (jnp.exp(jnp.sum(lp[0:1] * lp[1:2], axis=-1, keepdims=True))
           - jnp.exp(jnp.sum(lp[2:3] * lp[3:4], axis=-1, keepdims=True)) + lam_init)
    acc = acc_ref[...]
    o = acc[:, :tq] / l_fin[:, :tq] - lam * (acc[:, tq:] / l_fin[:, tq:])
    ms = jnp.mean(o * o, axis=0, keepdims=True)
    y = o * lax.rsqrt(ms + SUBLN_EPS) * g_ref[...] * (1.0 - lam_init)
    o_ref[0] = y.T.astype(BF16)


def _diff_attention(qbt, kb, vbt, bias_t, lp, g_b, lam_init):
    bsz, _, nqb, _, tq = qbt.shape
    nkb, tkc = vbt.shape[2], vbt.shape[4]
    seq = nqb * tq
    n_tiles = bias_t.shape[1]
    return pl.pallas_call(
        functools.partial(_diffattn_kernel, tq=tq, tkc=tkc, lam_init=lam_init),
        grid=(bsz, H_B, nqb),
        in_specs=[pl.BlockSpec((4, DH_B), lambda b, h, i: (0, 0)),
                  pl.BlockSpec((1, 1, 1, DV_B, tq), lambda b, h, i: (b, h, i, 0, 0)),
                  pl.BlockSpec((1, seq, DV_B), lambda b, h, i: (b, 0, h)),
                  pl.BlockSpec((1, 1, nkb, DV_B, tkc), lambda b, h, i: (b, h, 0, 0, 0)),
                  pl.BlockSpec((1, n_tiles, tkc, 2 * tq), lambda b, h, i: (h, 0, 0, 0)),
                  pl.BlockSpec((DV_B, tq), lambda b, h, i: (0, 0))],
        out_specs=pl.BlockSpec((1, tq, DV_B), lambda b, h, i: (b, i, h)),
        out_shape=jax.ShapeDtypeStruct((bsz, seq, H_B * DV_B), BF16),
        scratch_shapes=[pltpu.VMEM((DV_B, 2 * tq), F32)] + [pltpu.VMEM((tkc, 2 * tq), F32)] * KC_DIFF,
        name="diff_attention",
        compiler_params=_cparams(3),
    )(lp, qbt, kb, vbt, bias_t, g_b)


def _postmix_kernel(x_ref, oa_ref, ob_ref, ga_ref, gb_ref, wo_ref, gpost_ref, gt_ref, gpre_ref, sc_ref, sh_ref,
                    wr_ref, br_ref,
                    x1_ref, h2r_ref, idx_ref, w_ref, rank_ref, cnt_ref, carry_ref):
    step = pl.program_id(0)
    tm = x_ref.shape[0]

    @pl.when(step == 0)
    def _():
        carry_ref[...] = jnp.zeros_like(carry_ref)

    ya = jnp.dot(oa_ref[...], wo_ref[:QA_W, :], preferred_element_type=F32)
    yb = jnp.dot(ob_ref[...], wo_ref[QA_W:, :], preferred_element_type=F32)
    y = ga_ref[...].astype(F32) * ya + gb_ref[...].astype(F32) * yb
    ms = jnp.mean(y * y, axis=-1, keepdims=True)
    x1 = x_ref[...] + gt_ref[0] * (y * lax.rsqrt(ms + NORM_EPS) * gpost_ref[...])
    x1_ref[...] = x1
    ms1 = jnp.mean(x1 * x1, axis=-1, keepdims=True)
    h2 = x1 * lax.rsqrt(ms1 + NORM_EPS) * gpre_ref[...]
    h2 = h2 * (1.0 + sc_ref[0]) + sh_ref[0]
    for c in range(ROW_CHUNKS):
        h2r_ref[pl.ds(c, tm, stride=ROW_CHUNKS), :] = h2[:, c * LANES:(c + 1) * LANES]

    logits = lax.dot_general(wr_ref[...], h2, (((1,), (1,)), ((), ())), preferred_element_type=F32,
                             precision=lax.Precision.HIGHEST) + br_ref[...]
    erow = lax.broadcasted_iota(I32, logits.shape, 0)
    vals, sels = [], []
    cur = logits
    for k in range(TOP_K):
        mv = jnp.max(cur, axis=0, keepdims=True)
        idx = jnp.min(jnp.where(cur == mv, erow, N_EXPERTS), axis=0, keepdims=True)
        sel = erow == idx
        vals.append(mv)
        sels.append(sel)
        idx_ref[k:k + 1, :] = idx
        cur = jnp.where(sel, -jnp.inf, cur)
    es = [jnp.exp(v - vals[0]) for v in vals]
    den = es[0] + es[1] + es[2] + es[3]
    for k in range(TOP_K):
        w_ref[k:k + 1, :] = es[k] / den

    onehot = jnp.zeros(logits.shape, F32)
    for k in range(TOP_K):
        onehot = onehot + jnp.where(sels[k], 1.0, 0.0)
    srow = lax.broadcasted_iota(I32, (tm, tm), 0)
    scol = lax.broadcasted_iota(I32, (tm, tm), 1)
    upper = jnp.where(srow <= scol, 1.0, 0.0).astype(BF16)
    cum = jnp.dot(onehot.astype(BF16), upper, preferred_element_type=F32)
    carry = carry_ref[:, 0:1]
    excl = cum - onehot + carry
    for k in range(TOP_K):
        rank_ref[k:k + 1, :] = jnp.sum(jnp.where(sels[k], excl, 0.0), axis=0, keepdims=True).astype(I32)
    new_carry = carry + jnp.sum(onehot, axis=1, keepdims=True)
    carry_ref[...] = jnp.broadcast_to(new_carry, carry_ref.shape)
    cnt_ref[...] = jnp.broadcast_to(new_carry, cnt_ref.shape)


def _post_mix(x2d, oa, ob, ga, gb, wo, g_post, gt, g_pre, sc, sh, wr_t, br, seq):
    t, d = x2d.shape
    tm = TM_PROJ
    spb = seq // tm
    row = lambda i: (i, 0)
    bat = lambda i: (i // spb, 0, 0)
    const = lambda i: (0, 0)
    col = lambda i: (0, i)
    return pl.pallas_call(
        _postmix_kernel,
        grid=(t // tm,),
        in_specs=[pl.BlockSpec((tm, d), row),
                  pl.BlockSpec((tm, QA_W), row),
                  pl.BlockSpec((tm, H_B * DV_B), row),
                  pl.BlockSpec((tm, d), row),
                  pl.BlockSpec((tm, d), row),
                  pl.BlockSpec((QA_W + H_B * DV_B, d), const),
                  pl.BlockSpec((1, d), const),
                  pl.BlockSpec((1, 1, d), bat),
                  pl.BlockSpec((1, d), const),
                  pl.BlockSpec((1, 1, d), bat),
                  pl.BlockSpec((1, 1, d), bat),
                  pl.BlockSpec((N_EXPERTS, d), const),
                  pl.BlockSpec((N_EXPERTS, 1), const)],
        out_specs=[pl.BlockSpec((tm, d), row),
                   pl.BlockSpec((tm * ROW_CHUNKS, LANES), row),
                   pl.BlockSpec((TOP_K, tm), col),
                   pl.BlockSpec((TOP_K, tm), col),
                   pl.BlockSpec((TOP_K, tm), col),
                   pl.BlockSpec((N_EXPERTS, LANES), const)],
        out_shape=[jax.ShapeDtypeStruct((t, d), F32),
                   jax.ShapeDtypeStruct((t * ROW_CHUNKS, LANES), F32),
                   jax.ShapeDtypeStruct((TOP_K, t), I32),
                   jax.ShapeDtypeStruct((TOP_K, t), F32),
                   jax.ShapeDtypeStruct((TOP_K, t), I32),
                   jax.ShapeDtypeStruct((N_EXPERTS, LANES), F32)],
        scratch_shapes=[pltpu.VMEM((N_EXPERTS, LANES), F32)],
        name="post_mix_router",
        compiler_params=_cparams(1),
    )(x2d, oa, ob, ga, gb, wo, g_post, gt, g_pre, sc, sh, wr_t, br)


def _row_copy(src, src_row, dst, dst_row, sem):
    return pltpu.make_async_copy(src.at[pl.ds(pl.multiple_of(src_row * ROW_CHUNKS, ROW_CHUNKS), ROW_CHUNKS)],
                                 dst.at[pl.ds(pl.multiple_of(dst_row * ROW_CHUNKS, ROW_CHUNKS), ROW_CHUNKS)], sem)


def _dispatch_kernel(dest_ref, h2r_ref, xs_in_hbm, xs_hbm, dsm, sem_idx, sem, *, tg):
    del xs_in_hbm
    idx_copy = pltpu.make_async_copy(dest_ref.at[0, 0], dsm, sem_idx)
    idx_copy.start()
    idx_copy.wait()

    def issue(it, carry):
        for u in range(DMA_UNROLL):
            tok = it * DMA_UNROLL + u
            for k in range(TOP_K):
                _row_copy(h2r_ref, tok, xs_hbm, dsm[k * tg + tok], sem).start(priority=k % 2)
        return carry

    lax.fori_loop(0, tg // DMA_UNROLL, issue, 0)

    def drain(it, carry):
        for _ in range(DMA_UNROLL * TOP_K):
            _row_copy(h2r_ref, 0, xs_hbm, 0, sem).wait()
        return carry

    lax.fori_loop(0, tg // DMA_UNROLL, drain, 0)


def _dispatch(dest_blocks, h2r, xs_zero, tg):
    nsteps = dest_blocks.shape[0]
    return pl.pallas_call(
        functools.partial(_dispatch_kernel, tg=tg),
        grid=(nsteps,),
        in_specs=[pl.BlockSpec((1, 1, TOP_K * tg), lambda i: (i, 0, 0)),
                  pl.BlockSpec((tg * ROW_CHUNKS, LANES), lambda i: (i, 0)),
                  pl.BlockSpec(memory_space=pl.ANY)],
        out_specs=pl.BlockSpec(memory_space=pl.ANY),
        out_shape=jax.ShapeDtypeStruct(xs_zero.shape, xs_zero.dtype),
        scratch_shapes=[pltpu.SMEM((TOP_K * tg,), I32), pltpu.SemaphoreType.DMA, pltpu.SemaphoreType.DMA],
        input_output_aliases={2: 0},
        name="moe_dispatch",
        compiler_params=_cparams(1),
    )(dest_blocks, h2r, xs_zero)


def _expert_kernel(be_ref, nu_ref, xs_ref, wgu_ref, bgu_ref, wd_ref, bd_ref, ys_ref, wgu_bf, wd_bf, *, bm):
    i = pl.program_id(0)

    @pl.when((i == 0) | (be_ref[i] != be_ref[jnp.maximum(i - 1, 0)]))
    def _():
        wgu_bf[...] = wgu_ref[0, 0].astype(BF16)
        wd_bf[...] = wd_ref[0, 0].astype(BF16)

    @pl.when(i < nu_ref[0])
    def _():
        x = jnp.concatenate([xs_ref[pl.ds(c, bm, stride=ROW_CHUNKS), :] for c in range(ROW_CHUNKS)], axis=-1)
        gu = jnp.dot(x.astype(BF16), wgu_bf[...], preferred_element_type=F32) + bgu_ref[0, 0]
        gate = jnp.minimum(gu[:, :D_FF], SWIGLU_LIMIT)
        up = jnp.clip(gu[:, D_FF:], -SWIGLU_LIMIT, SWIGLU_LIMIT)
        act = gate * _sigmoid(SWIGLU_ALPHA * gate) * (up + 1.0)
        y = jnp.dot(act.astype(BF16), wd_bf[...], preferred_element_type=F32) + bd_ref[0, 0]
        for c in range(ROW_CHUNKS):
            ys_ref[pl.ds(c, bm, stride=ROW_CHUNKS), :] = y[:, c * LANES:(c + 1) * LANES]

    @pl.when(i >= nu_ref[0])
    def _():
        ys_ref[...] = jnp.zeros_like(ys_ref)


def _experts(block_e, n_used, xs, w_gate_up, b_gate_up, w_down, b_down, layer):
    bm = BM_EXPERT
    n_blocks = xs.shape[0] // (bm * ROW_CHUNKS)
    d = D_MODEL
    wsel = lambda i, be, nu: (layer, be[i], 0, 0)
    grid_spec = pltpu.PrefetchScalarGridSpec(
        num_scalar_prefetch=2,
        grid=(n_blocks,),
        in_specs=[pl.BlockSpec((bm * ROW_CHUNKS, LANES), lambda i, be, nu: (i, 0)),
                  pl.BlockSpec((1, 1, d, 2 * D_FF), wsel),
                  pl.BlockSpec((1, 1, 1, 2 * D_FF), wsel),
                  pl.BlockSpec((1, 1, D_FF, d), wsel),
                  pl.BlockSpec((1, 1, 1, d), wsel)],
        out_specs=pl.BlockSpec((bm * ROW_CHUNKS, LANES), lambda i, be, nu: (i, 0)),
        scratch_shapes=[pltpu.VMEM((d, 2 * D_FF), BF16), pltpu.VMEM((D_FF, d), BF16)],
    )
    return pl.pallas_call(
        functools.partial(_expert_kernel, bm=bm),
        grid_spec=grid_spec,
        out_shape=jax.ShapeDtypeStruct(xs.shape, F32),
        name="moe_experts",
        compiler_params=pltpu.CompilerParams(dimension_semantics=("arbitrary",),
                                             vmem_limit_bytes=VMEM_LIMIT_EXPERTS),
    )(block_e, n_used, xs, w_gate_up, b_gate_up, w_down, b_down)


def _combine_kernel(dest_ref, w_ref, x_ref, gt_ref, g_ref, ys_hbm, o_ref, dsm, buf, sem_idx, sem, *, tc):
    idx_copy = pltpu.make_async_copy(dest_ref.at[0, 0], dsm, sem_idx)
    idx_copy.start()
    idx_copy.wait()

    def issue(it, carry):
        for u in range(DMA_UNROLL):
            tok = it * DMA_UNROLL + u
            for k in range(TOP_K):
                _row_copy(ys_hbm, dsm[k * tc + tok], buf, k * tc + tok, sem).start(priority=k % 2)
        return carry

    lax.fori_loop(0, tc // DMA_UNROLL, issue, 0)

    def drain(it, carry):
        for _ in range(DMA_UNROLL * TOP_K):
            _row_copy(ys_hbm, 0, buf, 0, sem).wait()
        return carry

    lax.fori_loop(0, tc // DMA_UNROLL, drain, 0)

    y = jnp.zeros((tc, D_MODEL), F32)
    for k in range(TOP_K):
        yk = jnp.concatenate([buf[pl.ds(k * tc * ROW_CHUNKS + c, tc, stride=ROW_CHUNKS), :]
                              for c in range(ROW_CHUNKS)], axis=-1)
        y = y + w_ref[:, k:k + 1] * yk
    ms = jnp.mean(y * y, axis=-1, keepdims=True)
    o_ref[...] = x_ref[...] + gt_ref[0] * (y * lax.rsqrt(ms + NORM_EPS) * g_ref[...])


def _combine(dest_blocks, w_tok, x1, gt, g_post, ys, seq, tc):
    t, d = x1.shape
    spb = seq // tc
    return pl.pallas_call(
        functools.partial(_combine_kernel, tc=tc),
        grid=(t // tc,),
        in_specs=[pl.BlockSpec((1, 1, TOP_K * tc), lambda i: (i, 0, 0)),
                  pl.BlockSpec((tc, TOP_K), lambda i: (i, 0)),
                  pl.BlockSpec((tc, d), lambda i: (i, 0)),
                  pl.BlockSpec((1, 1, d), lambda i: (i // spb, 0, 0)),
                  pl.BlockSpec((1, d), lambda i: (0, 0)),
                  pl.BlockSpec(memory_space=pl.ANY)],
        out_specs=pl.BlockSpec((tc, d), lambda i: (i, 0)),
        out_shape=jax.ShapeDtypeStruct((t, d), F32),
        scratch_shapes=[pltpu.SMEM((TOP_K * tc,), I32),
                        pltpu.VMEM((TOP_K * tc * ROW_CHUNKS, LANES), F32),
                        pltpu.SemaphoreType.DMA, pltpu.SemaphoreType.DMA],
        name="moe_combine",
        compiler_params=_cparams(1),
    )(dest_blocks, w_tok, x1, gt, g_post, ys)


def _dest_blocks(dest, step_tokens):
    k, t = dest.shape
    return dest.reshape(k, t // step_tokens, step_tokens).transpose(1, 0, 2).reshape(t // step_tokens, 1,
                                                                                      k * step_tokens)


def _bias_tables(rel_bias_table, tq, tkc):
    def lookup(dist, table):
        onehot = np.eye(N_BUCKETS, dtype=np.float32)[_t5_bucket_np(dist)]
        return jnp.einsum('...n,nh->h...', onehot, table, precision=lax.Precision.HIGHEST)

    qi = np.arange(BLOCK)[:, None]
    kj = np.arange(2 * BLOCK)[None, :]
    dist = qi + BLOCK - kj
    valid = (dist >= 0) & (dist < WINDOW)
    bias_a = jnp.where(valid[None], lookup(dist, rel_bias_table[:, :H_A]), MASK_VALUE)
    kk = np.arange(tkc)[:, None]
    qq = np.arange(tq)[None, :]
    table_b = rel_bias_table[:, H_A:]
    table_b = (table_b - table_b[N_BUCKETS - 1:]) * LOG2E
    tiles = [jnp.zeros((H_B, tkc, tq), F32)]
    for rel in range(-1, tq // tkc):
        d = qq - (rel * tkc + kk)
        tiles.append(jnp.where((d >= 0)[None], lookup(d, table_b), MASK_VALUE))
    tiles.append(jnp.full((H_B, tkc, tq), MASK_VALUE, F32))
    bias_t = jnp.stack([jnp.concatenate([t, t], axis=-1) for t in tiles], axis=1)
    return bias_a.astype(F32), bias_t.astype(F32)


def _prep_in_weights(w_in_l):
    qa, ka, va, qb, kb, vb, gza, gzb = jnp.split(w_in_l, [int(v) for v in np.cumsum(SPLIT_SIZES)[:-1]], axis=-1)

    def dup(w):
        return jnp.concatenate([w[:, :DH_A], w[:, :DH_A], w[:, DH_A:], w[:, DH_A:]], axis=-1)

    w_nat = jnp.concatenate([qa * DH_A ** -0.5, dup(ka), dup(va), kb, gza, gzb], axis=-1).astype(BF16)
    w_t = jnp.concatenate([qb * (DH_B ** -0.5 * LOG2E), vb], axis=-1).T.astype(BF16)
    return w_nat, w_t


def kernel(x, c, rel_bias_table, w_ada, b_ada, g_mix_pre, g_mix_post, g_ffn_pre, g_ffn_post, w_in, w_out,
           attn_sinks, diff_lambda, diff_subln_g, w_router, b_router, w_gate_up, b_gate_up, w_down, b_down):
    bsz, seq, d = x.shape
    depth = w_ada.shape[0]
    t = bsz * seq
    tq, tkc = TQ_DIFF, TKC_DIFF
    assert d == D_MODEL and tkc >= BLOCK and tq % tkc == 0
    assert TM_PROJ % tq == 0 and TM_PROJ % tkc == 0 and (seq // tkc) % KC_DIFF == 0
    assert seq % TM_PROJ == 0 and seq % TM_SWA == 0
    assert seq % TG_DISPATCH == 0 and seq % TC_COMBINE == 0

    mod = _modulation(c, w_ada, b_ada)
    bias_a, bias_t = _bias_tables(rel_bias_table, tq, tkc)
    n_rows = t * TOP_K + N_EXPERTS * BM_EXPERT
    n_blocks = n_rows // BM_EXPERT

    x2d = x.reshape(t, d)
    for l in range(depth):
        sh_m, sc_m, gt_m, sh_f, sc_f, gt_f = [v[:, None, :] for v in jnp.split(mod[l], 6, axis=-1)]
        lam_init = 0.8 - 0.6 * math.exp(-0.3 * l)
        w_nat, w_t = _prep_in_weights(w_in[l])
        qa, ka2, va2, kb, ga, gb, qbt, vbt = _in_projection(
            x2d, sc_m, sh_m, g_mix_pre[l][None, :], w_nat, w_t, bsz, seq)
        o_a = _swa_attention(qa.reshape(bsz, seq, QA_W), ka2.reshape(bsz, seq, 2 * KA_W),
                             va2.reshape(bsz, seq, 2 * VA_W), attn_sinks[l], bias_a)
        g_b = jnp.broadcast_to(diff_subln_g[l][:, None], (DV_B, tq)).astype(F32)
        o_b = _diff_attention(qbt, kb.reshape(bsz, seq, KB_W), vbt, bias_t, diff_lambda[l], g_b, lam_init)
        x1, h2r, top_idx, top_w, rank, counts = _post_mix(
            x2d, o_a.reshape(t, QA_W), o_b.reshape(t, H_B * DV_B), ga, gb, w_out[l].astype(BF16),
            g_mix_post[l][None, :], gt_m, g_ffn_pre[l][None, :], sc_f, sh_f,
            w_router[l].T, b_router[l][:, None], seq)

        cnt = counts[:, 0].astype(I32)
        padded = (cnt + BM_EXPERT - 1) // BM_EXPERT * BM_EXPERT
        pends = jnp.cumsum(padded)
        pstarts = pends - padded
        experts = jnp.arange(N_EXPERTS, dtype=I32)
        dest = jnp.sum(jnp.where(top_idx[..., None] == experts, pstarts, 0), axis=-1) + rank
        block_start = jnp.arange(n_blocks, dtype=I32) * BM_EXPERT
        block_e = jnp.minimum(jnp.sum((pends[None, :] <= block_start[:, None]).astype(I32), axis=1),
                              N_EXPERTS - 1).astype(I32)
        n_used = (pends[-1:] // BM_EXPERT).astype(I32)

        xs = _dispatch(_dest_blocks(dest, TG_DISPATCH), h2r,
                       jnp.zeros((n_rows * ROW_CHUNKS, LANES), F32), TG_DISPATCH)
        ys = _experts(block_e, n_used, xs, w_gate_up, b_gate_up[:, :, None, :], w_down, b_down[:, :, None, :], l)
        x2d = _combine(_dest_blocks(dest, TC_COMBINE), top_w.T, x1, gt_f, g_ffn_post[l][None, :], ys,
                       seq, TC_COMBINE)
    return x2d.reshape(bsz, seq, d)
```

```python
import functools
import math

import numpy as np
import jax
import jax.numpy as jnp
from jax import lax
from jax.experimental import pallas as pl
from jax.experimental.pallas import tpu as pltpu

F32 = jnp.float32
BF16 = jnp.bfloat16
I32 = jnp.int32

D_MODEL = 1024
BLOCK = 128
WINDOW = 128
H_A, KV_A, DH_A = 8, 2, 64
G_A = H_A // KV_A
H_B, DH_B = 4, 64
DV_B = 2 * DH_B
DV_EXT = DV_B + 16
N_BUCKETS = 32
MAX_EXACT = N_BUCKETS // 2
REL_MAX_DIST = 128
N_EXPERTS = 32
TOP_K = 4
D_FF = D_MODEL
SWIGLU_ALPHA = 1.702
SWIGLU_LIMIT = 7.0
NORM_EPS = 1e-6
SUBLN_EPS = 1e-5
QA_W = H_A * DH_A
KA_W = KV_A * DH_A
VA_W = KV_A * DH_A
QB_W = H_B * 2 * DH_B
KB_W = H_B * 2 * DH_B
VB_W = H_B * 2 * DH_B
SPLIT_SIZES = (QA_W, KA_W, VA_W, QB_W, KB_W, VB_W, D_MODEL, D_MODEL)

LANES = 128
SUBLANES = 8
ROW_CHUNKS = D_MODEL // LANES
MASK_VALUE = -1e30

TM_PROJ = 512
TQ_DIFF = 512
TKC_DIFF = 256
KC_DIFF = 4
LOG2E = math.log2(math.e)
TM_SWA = 512
BM_EXPERT = 512
TG_DISPATCH = 512
TC_COMBINE = 256
DMA_UNROLL = 8
VMEM_LIMIT = 48 * 1024 * 1024
VMEM_LIMIT_EXPERTS = 56 * 1024 * 1024


def _cparams(n_axes):
    return pltpu.CompilerParams(dimension_semantics=("arbitrary",) * n_axes,
                                vmem_limit_bytes=VMEM_LIMIT)


def _sigmoid(x):
    return 1.0 / (1.0 + jnp.exp(-x))


def _t5_bucket_np(dist):
    n = np.maximum(dist, 0)
    nf = np.maximum(n, 1).astype(np.float32)
    large = MAX_EXACT + (np.log(nf / np.float32(MAX_EXACT)) / np.float32(math.log(REL_MAX_DIST / MAX_EXACT))
                         * np.float32(N_BUCKETS - MAX_EXACT)).astype(np.int32)
    large = np.minimum(large, N_BUCKETS - 1)
    return np.where(n < MAX_EXACT, n, large)


def _mod_kernel(c_ref, w_ref, b_ref, o_ref):
    c = c_ref[...]
    ca = c * _sigmoid(c)
    o_ref[0] = jnp.dot(ca, w_ref[0], preferred_element_type=F32,
                       precision=lax.Precision.HIGHEST) + b_ref[0]


def _modulation(c, w_ada, b_ada):
    depth, d, n = w_ada.shape
    bsz = c.shape[0]
    rows = -(-bsz // SUBLANES) * SUBLANES
    c_pad = jnp.zeros((rows, d), F32).at[:bsz].set(c)
    tn = 1536
    out = pl.pallas_call(
        _mod_kernel,
        grid=(depth, n // tn),
        in_specs=[pl.BlockSpec((rows, d), lambda l, j: (0, 0)),
                  pl.BlockSpec((1, d, tn), lambda l, j: (l, 0, j)),
                  pl.BlockSpec((1, 1, tn), lambda l, j: (l, 0, j))],
        out_specs=pl.BlockSpec((1, rows, tn), lambda l, j: (l, 0, j)),
        out_shape=jax.ShapeDtypeStruct((depth, rows, n), F32),
        name="adaln_modulation",
        compiler_params=_cparams(2),
    )(c_pad, w_ada, b_ada.reshape(depth, 1, n))
    return out[:, :bsz]


N_NAT = QA_W + 2 * KA_W + 2 * VA_W + KB_W + 2 * D_MODEL
O_QA, O_KA, O_VA, O_KB, O_GA, O_GB = 0, 512, 768, 1024, 1536, 2560


def _inproj_kernel(x_ref, sc_ref, sh_ref, g_ref, w_ref, wt_ref,
                   qa_ref, ka_ref, va_ref, kb_ref, ga_ref, gb_ref, qbt_ref, vbt_ref, *, tq, tkc):
    x = x_ref[...]
    ms = jnp.mean(x * x, axis=-1, keepdims=True)
    h = x * lax.rsqrt(ms + NORM_EPS) * g_ref[...]
    h = h * (1.0 + sc_ref[0]) + sh_ref[0]
    hb = h.astype(BF16)
    tm = x.shape[0]

    def proj(a, b):
        return jnp.dot(hb, w_ref[:, a:b], preferred_element_type=F32)

    qa_ref[...] = proj(O_QA, O_KA).astype(BF16)
    ka_ref[...] = proj(O_KA, O_VA).astype(BF16)
    va_ref[...] = proj(O_VA, O_KB).astype(BF16)
    kb_ref[...] = proj(O_KB, O_GA).astype(BF16)
    ga_ref[...] = _sigmoid(proj(O_GA, O_GB)).astype(BF16)
    gb_ref[...] = _sigmoid(proj(O_GB, N_NAT)).astype(BF16)
    for t_ref, base, tb in ((qbt_ref, 0, tq), (vbt_ref, QB_W, tkc)):
        rt = lax.dot_general(wt_ref[base: base + H_B * DV_B, :], hb,
                             (((1,), (1,)), ((), ())), preferred_element_type=F32)
        for hh in range(H_B):
            for j in range(tm // tb):
                t_ref[0, hh, j, :DV_B, :] = rt[hh * DV_B:(hh + 1) * DV_B, j * tb:(j + 1) * tb].astype(BF16)
    vbt_ref[:, :, :, DV_B:, :] = jnp.ones((1, H_B, tm // tkc, DV_EXT - DV_B, tkc), BF16)


def _in_projection(x2d, sc, sh, g, w_nat, w_t, bsz, seq):
    t, d = x2d.shape
    tm, tq, tkc = TM_PROJ, TQ_DIFF, TKC_DIFF
    spb = seq // tm
    row = lambda i: (i, 0)
    bat = lambda i: (i // spb, 0, 0)
    tr = lambda i: (i // spb, 0, i % spb, 0, 0)
    outs = pl.pallas_call(
        functools.partial(_inproj_kernel, tq=tq, tkc=tkc),
        grid=(t // tm,),
        in_specs=[pl.BlockSpec((tm, d), row),
                  pl.BlockSpec((1, 1, d), bat),
                  pl.BlockSpec((1, 1, d), bat),
                  pl.BlockSpec((1, d), lambda i: (0, 0)),
                  pl.BlockSpec((d, N_NAT), lambda i: (0, 0)),
                  pl.BlockSpec((QB_W + VB_W, d), lambda i: (0, 0))],
        out_specs=[pl.BlockSpec((tm, QA_W), row),
                   pl.BlockSpec((tm, 2 * KA_W), row),
                   pl.BlockSpec((tm, 2 * VA_W), row),
                   pl.BlockSpec((tm, KB_W), row),
                   pl.BlockSpec((tm, d), row),
                   pl.BlockSpec((tm, d), row),
                   pl.BlockSpec((1, H_B, tm // tq, DV_B, tq), tr),
                   pl.BlockSpec((1, H_B, tm // tkc, DV_EXT, tkc), tr)],
        out_shape=[jax.ShapeDtypeStruct((t, QA_W), BF16),
                   jax.ShapeDtypeStruct((t, 2 * KA_W), BF16),
                   jax.ShapeDtypeStruct((t, 2 * VA_W), BF16),
                   jax.ShapeDtypeStruct((t, KB_W), BF16),
                   jax.ShapeDtypeStruct((t, d), BF16),
                   jax.ShapeDtypeStruct((t, d), BF16),
                   jax.ShapeDtypeStruct((bsz, H_B, seq // tq, DV_B, tq), BF16),
                   jax.ShapeDtypeStruct((bsz, H_B, seq // tkc, DV_EXT, tkc), BF16)],
        name="in_projection",
        compiler_params=_cparams(1),
    )(x2d, sc, sh, g, w_nat, w_t)
    return outs


def _swa_kernel(sink_ref, q_ref, kc_ref, kp_ref, vc_ref, vp_ref, bias_ref, o_ref):
    i = pl.program_id(1)
    nsub = q_ref.shape[1] // BLOCK
    lane_k = lax.broadcasted_iota(I32, (2 * BLOCK, LANES), 1)
    lane_s = lax.broadcasted_iota(I32, (BLOCK, 2 * BLOCK), 1)
    first_mask = jnp.where((i == 0) & (lane_s < BLOCK), MASK_VALUE, 0.0).astype(F32)
    zero_b = jnp.zeros((2 * BLOCK, LANES), BF16)
    for j in range(nsub):
        rows = slice(j * BLOCK, (j + 1) * BLOCK)
        if j == 0:
            kband = jnp.concatenate([kp_ref[0], kc_ref[0, rows, :]], axis=0)
            vband = jnp.concatenate([vp_ref[0], vc_ref[0, rows, :]], axis=0)
        else:
            kband = kc_ref[0, (j - 1) * BLOCK:(j + 1) * BLOCK, :]
            vband = vc_ref[0, (j - 1) * BLOCK:(j + 1) * BLOCK, :]
        for kv in range(KV_A):
            kslab = kband[:, kv * LANES:(kv + 1) * LANES]
            vslab = vband[:, kv * LANES:(kv + 1) * LANES]
            kpad = (jnp.where(lane_k < DH_A, kslab, zero_b), jnp.where(lane_k >= DH_A, kslab, zero_b))
            vpad = (jnp.where(lane_k < DH_A, vslab, zero_b), jnp.where(lane_k >= DH_A, vslab, zero_b))
            for pp in range(G_A // 2):
                pair = kv * (G_A // 2) + pp
                qslab = q_ref[0, rows, pair * LANES:(pair + 1) * LANES]
                o_slab = jnp.zeros((BLOCK, LANES), F32)
                for par in range(2):
                    head = 2 * pair + par
                    s = lax.dot_general(qslab, kpad[par], (((1,), (1,)), ((), ())),
                                        preferred_element_type=F32)
                    s = s + bias_ref[head]
                    if j == 0:
                        s = s + first_mask
                    sink = sink_ref[head]
                    m = jnp.maximum(jnp.max(s, axis=-1, keepdims=True), sink)
                    e = jnp.exp(s - m)
                    den = jnp.sum(e, axis=-1, keepdims=True) + jnp.exp(sink - m)
                    pv = jnp.dot(e.astype(BF16), vpad[par], preferred_element_type=F32)
                    o_slab = o_slab + pv / den
                o_ref[0, rows, pair * LANES:(pair + 1) * LANES] = o_slab.astype(BF16)


def _swa_attention(qa, ka2, va2, sinks, bias_a):
    bsz, seq, _ = qa.shape
    tm = TM_SWA
    nsub = tm // BLOCK
    cur = lambda b, i: (b, i, 0)
    prev = lambda b, i: (b, jnp.maximum(i * nsub - 1, 0), 0)
    return pl.pallas_call(
        _swa_kernel,
        grid=(bsz, seq // tm),
        in_specs=[pl.BlockSpec(memory_space=pltpu.SMEM),
                  pl.BlockSpec((1, tm, QA_W), cur),
                  pl.BlockSpec((1, tm, 2 * KA_W), cur),
                  pl.BlockSpec((1, BLOCK, 2 * KA_W), prev),
                  pl.BlockSpec((1, tm, 2 * VA_W), cur),
                  pl.BlockSpec((1, BLOCK, 2 * VA_W), prev),
                  pl.BlockSpec((H_A, BLOCK, 2 * BLOCK), lambda b, i: (0, 0, 0))],
        out_specs=pl.BlockSpec((1, tm, QA_W), cur),
        out_shape=jax.ShapeDtypeStruct((bsz, seq, QA_W), BF16),
        name="swa_attention",
        compiler_params=_cparams(2),
    )(sinks, qa, ka2, ka2, va2, va2, bias_a)


def _diffattn_kernel(lp_ref, qt_ref, k_ref, vt_ref, bias_ref, g_ref, o_ref, acc_ref, *s_bufs, tq, tkc, lam_init):
    i = pl.program_id(2)
    kc = len(s_bufs)
    r = tq // tkc
    last_chunk = vt_ref.shape[2] - 1
    qt = qt_ref[0, 0, 0]
    row = lax.broadcasted_iota(I32, qt.shape, 0)
    zero_q = jnp.zeros_like(qt)
    qboth = jnp.concatenate([jnp.where(row < DH_B, qt, zero_q), jnp.where(row >= DH_B, qt, zero_q)], axis=1)
    acc_ref[...] = jnp.zeros_like(acc_ref)

    def scores_chunk(c, blk, biased):
        src = jnp.minimum(blk, last_chunk)
        kblk = k_ref[0, pl.ds(pl.multiple_of(src * tkc, tkc), tkc), :]
        s = jnp.dot(kblk, qboth, preferred_element_type=F32)
        if biased:
            s = s + bias_ref[0, jnp.clip(blk - i * r + 2, 0, r + 2)]
        s_bufs[c][...] = s
        return jnp.max(s, axis=0, keepdims=True)

    def group(t, carry, nxt):
        m_old, mx = carry
        m_new = jnp.maximum(m_old, mx)
        alpha = jnp.exp2(m_old - m_new)
        mx_next = jnp.full_like(mx, MASK_VALUE)
        pv = None
        for c in range(kc):
            p = jnp.exp2((s_bufs[c][...] - m_new).astype(BF16))
            vblk = vt_ref[0, 0, jnp.minimum(t * kc + c, last_chunk)]
            d = jnp.dot(vblk, p, preferred_element_type=F32)
            pv = d if pv is None else pv + d
            if nxt is not None:
                mx_next = jnp.maximum(mx_next, scores_chunk(c, (t + 1) * kc + c, nxt == 'biased'))
        acc_ref[...] = alpha * acc_ref[...] + pv
        return m_new, mx_next

    n_groups = ((i + 1) * r + kc - 1) // kc
    n_plain = jnp.maximum(i * r - 1, 0) // kc
    split = jnp.maximum(n_plain - 1, 0)

    mx0 = jnp.full((1, 2 * tq), MASK_VALUE, F32)
    for c in range(kc):
        mx0 = jnp.maximum(mx0, scores_chunk(c, c, True))
    carry = (jnp.full((1, 2 * tq), MASK_VALUE, F32), mx0)
    carry = lax.fori_loop(0, split, lambda t, cr: group(t, cr, 'plain'), carry)
    carry = lax.fori_loop(split, n_groups - 1, lambda t, cr: group(t, cr, 'biased'), carry)
    group(n_groups - 1, carry, None)

    lp = lp_ref[...]
    lam = (jnp.exp(jnp.sum(lp[0:1] * lp[1:2], axis=-1, keepdims=True))
           - jnp.exp(jnp.sum(lp[2:3] * lp[3:4], axis=-1, keepdims=True)) + lam_init)
    acc = acc_ref[:DV_B, :]
    l_fin = acc_ref[DV_B:DV_B + 1, :]
    o = acc[:, :tq] / l_fin[:, :tq] - lam * (acc[:, tq:] / l_fin[:, tq:])
    ms = jnp.mean(o * o, axis=0, keepdims=True)
    y = o * lax.rsqrt(ms + SUBLN_EPS) * g_ref[...] * (1.0 - lam_init)
    o_ref[0] = y.T.astype(BF16)


def _diff_attention(qbt, kb, vbt, bias_t, lp, g_b, lam_init):
    bsz, _, nqb, _, tq = qbt.shape
    nkb, tkc = vbt.shape[2], vbt.shape[4]
    seq = nqb * tq
    n_tiles = bias_t.shape[1]
    return pl.pallas_call(
        functools.partial(_diffattn_kernel, tq=tq, tkc=tkc, lam_init=lam_init),
        grid=(bsz, H_B, nqb),
        in_specs=[pl.BlockSpec((4, DH_B), lambda b, h, i: (0, 0)),
                  pl.BlockSpec((1, 1, 1, DV_B, tq), lambda b, h, i: (b, h, i, 0, 0)),
                  pl.BlockSpec((1, seq, DV_B), lambda b, h, i: (b, 0, h)),
                  pl.BlockSpec((1, 1, nkb, DV_EXT, tkc), lambda b, h, i: (b, h, 0, 0, 0)),
                  pl.BlockSpec((1, n_tiles, tkc, 2 * tq), lambda b, h, i: (h, 0, 0, 0)),
                  pl.BlockSpec((DV_B, tq), lambda b, h, i: (0, 0))],
        out_specs=pl.BlockSpec((1, tq, DV_B), lambda b, h, i: (b, i, h)),
        out_shape=jax.ShapeDtypeStruct((bsz, seq, H_B * DV_B), BF16),
        scratch_shapes=[pltpu.VMEM((DV_EXT, 2 * tq), F32)] + [pltpu.VMEM((tkc, 2 * tq), F32)] * KC_DIFF,
        name="diff_attention",
        compiler_params=_cparams(3),
    )(lp, qbt, kb, vbt, bias_t, g_b)


def _postmix_kernel(x_ref, oa_ref, ob_ref, ga_ref, gb_ref, wo_ref, gpost_ref, gt_ref, gpre_ref, sc_ref, sh_ref,
                    wr_ref, br_ref,
                    x1_ref, h2r_ref, idx_ref, w_ref, rank_ref, cnt_ref, carry_ref):
    step = pl.program_id(0)
    tm = x_ref.shape[0]

    @pl.when(step == 0)
    def _():
        carry_ref[...] = jnp.zeros_like(carry_ref)

    ya = jnp.dot(oa_ref[...], wo_ref[:QA_W, :], preferred_element_type=F32)
    yb = jnp.dot(ob_ref[...], wo_ref[QA_W:, :], preferred_element_type=F32)
    y = ga_ref[...].astype(F32) * ya + gb_ref[...].astype(F32) * yb
    ms = jnp.mean(y * y, axis=-1, keepdims=True)
    x1 = x_ref[...] + gt_ref[0] * (y * lax.rsqrt(ms + NORM_EPS) * gpost_ref[...])
    x1_ref[...] = x1
    ms1 = jnp.mean(x1 * x1, axis=-1, keepdims=True)
    h2 = x1 * lax.rsqrt(ms1 + NORM_EPS) * gpre_ref[...]
    h2 = h2 * (1.0 + sc_ref[0]) + sh_ref[0]
    for c in range(ROW_CHUNKS):
        h2r_ref[pl.ds(c, tm, stride=ROW_CHUNKS), :] = h2[:, c * LANES:(c + 1) * LANES]

    logits = lax.dot_general(wr_ref[...], h2, (((1,), (1,)), ((), ())), preferred_element_type=F32,
                             precision=lax.Precision.HIGHEST) + br_ref[...]
    erow = lax.broadcasted_iota(I32, logits.shape, 0)
    vals, sels = [], []
    cur = logits
    for k in range(TOP_K):
        mv = jnp.max(cur, axis=0, keepdims=True)
        idx = jnp.min(jnp.where(cur == mv, erow, N_EXPERTS), axis=0, keepdims=True)
        sel = erow == idx
        vals.append(mv)
        sels.append(sel)
        idx_ref[k:k + 1, :] = idx
        cur = jnp.where(sel, -jnp.inf, cur)
    es = [jnp.exp(v - vals[0]) for v in vals]
    den = es[0] + es[1] + es[2] + es[3]
    for k in range(TOP_K):
        w_ref[k:k + 1, :] = es[k] / den

    onehot = jnp.zeros(logits.shape, F32)
    for k in range(TOP_K):
        onehot = onehot + jnp.where(sels[k], 1.0, 0.0)
    srow = lax.broadcasted_iota(I32, (tm, tm), 0)
    scol = lax.broadcasted_iota(I32, (tm, tm), 1)
    upper = jnp.where(srow <= scol, 1.0, 0.0).astype(BF16)
    cum = jnp.dot(onehot.astype(BF16), upper, preferred_element_type=F32)
    carry = carry_ref[:, 0:1]
    excl = cum - onehot + carry
    for k in range(TOP_K):
        rank_ref[k:k + 1, :] = jnp.sum(jnp.where(sels[k], excl, 0.0), axis=0, keepdims=True).astype(I32)
    new_carry = carry + jnp.sum(onehot, axis=1, keepdims=True)
    carry_ref[...] = jnp.broadcast_to(new_carry, carry_ref.shape)
    cnt_ref[...] = jnp.broadcast_to(new_carry, cnt_ref.shape)


def _post_mix(x2d, oa, ob, ga, gb, wo, g_post, gt, g_pre, sc, sh, wr_t, br, seq):
    t, d = x2d.shape
    tm = TM_PROJ
    spb = seq // tm
    row = lambda i: (i, 0)
    bat = lambda i: (i // spb, 0, 0)
    const = lambda i: (0, 0)
    col = lambda i: (0, i)
    return pl.pallas_call(
        _postmix_kernel,
        grid=(t // tm,),
        in_specs=[pl.BlockSpec((tm, d), row),
                  pl.BlockSpec((tm, QA_W), row),
                  pl.BlockSpec((tm, H_B * DV_B), row),
                  pl.BlockSpec((tm, d), row),
                  pl.BlockSpec((tm, d), row),
                  pl.BlockSpec((QA_W + H_B * DV_B, d), const),
                  pl.BlockSpec((1, d), const),
                  pl.BlockSpec((1, 1, d), bat),
                  pl.BlockSpec((1, d), const),
                  pl.BlockSpec((1, 1, d), bat),
                  pl.BlockSpec((1, 1, d), bat),
                  pl.BlockSpec((N_EXPERTS, d), const),
                  pl.BlockSpec((N_EXPERTS, 1), const)],
        out_specs=[pl.BlockSpec((tm, d), row),
                   pl.BlockSpec((tm * ROW_CHUNKS, LANES), row),
                   pl.BlockSpec((TOP_K, tm), col),
                   pl.BlockSpec((TOP_K, tm), col),
                   pl.BlockSpec((TOP_K, tm), col),
                   pl.BlockSpec((N_EXPERTS, LANES), const)],
        out_shape=[jax.ShapeDtypeStruct((t, d), F32),
                   jax.ShapeDtypeStruct((t * ROW_CHUNKS, LANES), F32),
                   jax.ShapeDtypeStruct((TOP_K, t), I32),
                   jax.ShapeDtypeStruct((TOP_K, t), F32),
                   jax.ShapeDtypeStruct((TOP_K, t), I32),
                   jax.ShapeDtypeStruct((N_EXPERTS, LANES), F32)],
        scratch_shapes=[pltpu.VMEM((N_EXPERTS, LANES), F32)],
        name="post_mix_router",
        compiler_params=_cparams(1),
    )(x2d, oa, ob, ga, gb, wo, g_post, gt, g_pre, sc, sh, wr_t, br)


def _row_copy(src, src_row, dst, dst_row, sem):
    return pltpu.make_async_copy(src.at[pl.ds(pl.multiple_of(src_row * ROW_CHUNKS, ROW_CHUNKS), ROW_CHUNKS)],
                                 dst.at[pl.ds(pl.multiple_of(dst_row * ROW_CHUNKS, ROW_CHUNKS), ROW_CHUNKS)], sem)


def _dispatch_kernel(pad_lo_ref, pad_hi_ref, dest_ref, h2r_ref, xs_hbm, dsm, zero_row, sem_idx, sem, sem_pad, *, tg):
    @pl.when(pl.program_id(0) == 0)
    def _():
        zero_row[...] = jnp.zeros_like(zero_row)

        def per_expert(e, carry):
            def fill(r, c):
                _row_copy(zero_row, 0, xs_hbm, r, sem_pad).start()
                return c

            def drain(r, c):
                _row_copy(zero_row, 0, xs_hbm, 0, sem_pad).wait()
                return c

            lax.fori_loop(pad_lo_ref[e], pad_hi_ref[e], fill, 0)
            lax.fori_loop(pad_lo_ref[e], pad_hi_ref[e], drain, 0)
            return carry

        lax.fori_loop(0, N_EXPERTS, per_expert, 0)

    idx_copy = pltpu.make_async_copy(dest_ref.at[0, 0], dsm, sem_idx)
    idx_copy.start()
    idx_copy.wait()

    def issue(it, carry):
        for u in range(DMA_UNROLL):
            tok = it * DMA_UNROLL + u
            for k in range(TOP_K):
                _row_copy(h2r_ref, tok, xs_hbm, dsm[k * tg + tok], sem).start(priority=k % 2)
        return carry

    lax.fori_loop(0, tg // DMA_UNROLL, issue, 0)

    def drain(it, carry):
        for _ in range(DMA_UNROLL * TOP_K):
            _row_copy(h2r_ref, 0, xs_hbm, 0, sem).wait()
        return carry

    lax.fori_loop(0, tg // DMA_UNROLL, drain, 0)


def _dispatch(pad_lo, pad_hi, dest_blocks, h2r, n_rows, tg):
    nsteps = dest_blocks.shape[0]
    grid_spec = pltpu.PrefetchScalarGridSpec(
        num_scalar_prefetch=2,
        grid=(nsteps,),
        in_specs=[pl.BlockSpec((1, 1, TOP_K * tg), lambda i, lo, hi: (i, 0, 0)),
                  pl.BlockSpec((tg * ROW_CHUNKS, LANES), lambda i, lo, hi: (i, 0))],
        out_specs=pl.BlockSpec(memory_space=pl.ANY),
        scratch_shapes=[pltpu.SMEM((TOP_K * tg,), I32), pltpu.VMEM((ROW_CHUNKS, LANES), F32),
                        pltpu.SemaphoreType.DMA, pltpu.SemaphoreType.DMA, pltpu.SemaphoreType.DMA],
    )
    return pl.pallas_call(
        functools.partial(_dispatch_kernel, tg=tg),
        grid_spec=grid_spec,
        out_shape=jax.ShapeDtypeStruct((n_rows * ROW_CHUNKS, LANES), F32),
        name="moe_dispatch",
        compiler_params=_cparams(1),
    )(pad_lo, pad_hi, dest_blocks, h2r)


def _expert_kernel(be_ref, nu_ref, xs_ref, wgu_ref, bgu_ref, wd_ref, bd_ref, ys_ref, wgu_bf, wd_bf, *, bm):
    i = pl.program_id(0)

    @pl.when((i == 0) | (be_ref[i] != be_ref[jnp.maximum(i - 1, 0)]))
    def _():
        wgu_bf[...] = wgu_ref[0, 0].astype(BF16)
        wd_bf[...] = wd_ref[0, 0].astype(BF16)

    @pl.when(i < nu_ref[0])
    def _():
        x = jnp.concatenate([xs_ref[pl.ds(c, bm, stride=ROW_CHUNKS), :] for c in range(ROW_CHUNKS)], axis=-1)
        gu = jnp.dot(x.astype(BF16), wgu_bf[...], preferred_element_type=F32) + bgu_ref[0, 0]
        gate = jnp.minimum(gu[:, :D_FF], SWIGLU_LIMIT)
        up = jnp.clip(gu[:, D_FF:], -SWIGLU_LIMIT, SWIGLU_LIMIT)
        act = gate * _sigmoid(SWIGLU_ALPHA * gate) * (up + 1.0)
        y = jnp.dot(act.astype(BF16), wd_bf[...], preferred_element_type=F32) + bd_ref[0, 0]
        for c in range(ROW_CHUNKS):
            ys_ref[pl.ds(c, bm, stride=ROW_CHUNKS), :] = y[:, c * LANES:(c + 1) * LANES]

    @pl.when(i >= nu_ref[0])
    def _():
        ys_ref[...] = jnp.zeros_like(ys_ref)


def _experts(block_e, n_used, xs, w_gate_up, b_gate_up, w_down, b_down, layer):
    bm = BM_EXPERT
    n_blocks = xs.shape[0] // (bm * ROW_CHUNKS)
    d = D_MODEL
    wsel = lambda i, be, nu: (layer, be[i], 0, 0)
    grid_spec = pltpu.PrefetchScalarGridSpec(
        num_scalar_prefetch=2,
        grid=(n_blocks,),
        in_specs=[pl.BlockSpec((bm * ROW_CHUNKS, LANES), lambda i, be, nu: (jnp.minimum(i, nu[0] - 1), 0)),
                  pl.BlockSpec((1, 1, d, 2 * D_FF), wsel),
                  pl.BlockSpec((1, 1, 1, 2 * D_FF), wsel),
                  pl.BlockSpec((1, 1, D_FF, d), wsel),
                  pl.BlockSpec((1, 1, 1, d), wsel)],
        out_specs=pl.BlockSpec((bm * ROW_CHUNKS, LANES), lambda i, be, nu: (i, 0)),
        scratch_shapes=[pltpu.VMEM((d, 2 * D_FF), BF16), pltpu.VMEM((D_FF, d), BF16)],
    )
    return pl.pallas_call(
        functools.partial(_expert_kernel, bm=bm),
        grid_spec=grid_spec,
        out_shape=jax.ShapeDtypeStruct(xs.shape, F32),
        name="moe_experts",
        compiler_params=pltpu.CompilerParams(dimension_semantics=("arbitrary",),
                                             vmem_limit_bytes=VMEM_LIMIT_EXPERTS),
    )(block_e, n_used, xs, w_gate_up, b_gate_up, w_down, b_down)


def _combine_kernel(dest_ref, w_ref, x_ref, gt_ref, g_ref, ys_hbm, o_ref, dsm, buf, sem_idx, sem, *, tc):
    idx_copy = pltpu.make_async_copy(dest_ref.at[0, 0], dsm, sem_idx)
    idx_copy.start()
    idx_copy.wait()

    def issue(it, carry):
        for u in range(DMA_UNROLL):
            tok = it * DMA_UNROLL + u
            for k in range(TOP_K):
                _row_copy(ys_hbm, dsm[k * tc + tok], buf, k * tc + tok, sem).start(priority=k % 2)
        return carry

    lax.fori_loop(0, tc // DMA_UNROLL, issue, 0)

    def drain(it, carry):
        for _ in range(DMA_UNROLL * TOP_K):
            _row_copy(ys_hbm, 0, buf, 0, sem).wait()
        return carry

    lax.fori_loop(0, tc // DMA_UNROLL, drain, 0)

    y = jnp.zeros((tc, D_MODEL), F32)
    for k in range(TOP_K):
        yk = jnp.concatenate([buf[pl.ds(k * tc * ROW_CHUNKS + c, tc, stride=ROW_CHUNKS), :]
                              for c in range(ROW_CHUNKS)], axis=-1)
        y = y + w_ref[:, k:k + 1] * yk
    ms = jnp.mean(y * y, axis=-1, keepdims=True)
    o_ref[...] = x_ref[...] + gt_ref[0] * (y * lax.rsqrt(ms + NORM_EPS) * g_ref[...])


def _combine(dest_blocks, w_tok, x1, gt, g_post, ys, seq, tc):
    t, d = x1.shape
    spb = seq // tc
    return pl.pallas_call(
        functools.partial(_combine_kernel, tc=tc),
        grid=(t // tc,),
        in_specs=[pl.BlockSpec((1, 1, TOP_K * tc), lambda i: (i, 0, 0)),
                  pl.BlockSpec((tc, TOP_K), lambda i: (i, 0)),
                  pl.BlockSpec((tc, d), lambda i: (i, 0)),
                  pl.BlockSpec((1, 1, d), lambda i: (i // spb, 0, 0)),
                  pl.BlockSpec((1, d), lambda i: (0, 0)),
                  pl.BlockSpec(memory_space=pl.ANY)],
        out_specs=pl.BlockSpec((tc, d), lambda i: (i, 0)),
        out_shape=jax.ShapeDtypeStruct((t, d), F32),
        scratch_shapes=[pltpu.SMEM((TOP_K * tc,), I32),
                        pltpu.VMEM((TOP_K * tc * ROW_CHUNKS, LANES), F32),
                        pltpu.SemaphoreType.DMA, pltpu.SemaphoreType.DMA],
        name="moe_combine",
        compiler_params=_cparams(1),
    )(dest_blocks, w_tok, x1, gt, g_post, ys)


def _dest_blocks(dest, step_tokens):
    k, t = dest.shape
    return dest.reshape(k, t // step_tokens, step_tokens).transpose(1, 0, 2).reshape(t // step_tokens, 1,
                                                                                      k * step_tokens)


def _bias_tables(rel_bias_table, tq, tkc):
    def lookup(dist, table):
        onehot = np.eye(N_BUCKETS, dtype=np.float32)[_t5_bucket_np(dist)]
        return jnp.einsum('...n,nh->h...', onehot, table, precision=lax.Precision.HIGHEST)

    qi = np.arange(BLOCK)[:, None]
    kj = np.arange(2 * BLOCK)[None, :]
    dist = qi + BLOCK - kj
    valid = (dist >= 0) & (dist < WINDOW)
    bias_a = jnp.where(valid[None], lookup(dist, rel_bias_table[:, :H_A]), MASK_VALUE)
    kk = np.arange(tkc)[:, None]
    qq = np.arange(tq)[None, :]
    table_b = rel_bias_table[:, H_A:]
    table_b = (table_b - table_b[N_BUCKETS - 1:]) * LOG2E
    tiles = [jnp.zeros((H_B, tkc, tq), F32)]
    for rel in range(-1, tq // tkc):
        d = qq - (rel * tkc + kk)
        tiles.append(jnp.where((d >= 0)[None], lookup(d, table_b), MASK_VALUE))
    tiles.append(jnp.full((H_B, tkc, tq), MASK_VALUE, F32))
    bias_t = jnp.stack([jnp.concatenate([t, t], axis=-1) for t in tiles], axis=1)
    return bias_a.astype(F32), bias_t.astype(F32)


def _prep_in_weights(w_in_l):
    qa, ka, va, qb, kb, vb, gza, gzb = jnp.split(w_in_l, [int(v) for v in np.cumsum(SPLIT_SIZES)[:-1]], axis=-1)

    def dup(w):
        return jnp.concatenate([w[:, :DH_A], w[:, :DH_A], w[:, DH_A:], w[:, DH_A:]], axis=-1)

    w_nat = jnp.concatenate([qa * DH_A ** -0.5, dup(ka), dup(va), kb, gza, gzb], axis=-1).astype(BF16)
    w_t = jnp.concatenate([qb * (DH_B ** -0.5 * LOG2E), vb], axis=-1).T.astype(BF16)
    return w_nat, w_t


def kernel(x, c, rel_bias_table, w_ada, b_ada, g_mix_pre, g_mix_post, g_ffn_pre, g_ffn_post, w_in, w_out,
           attn_sinks, diff_lambda, diff_subln_g, w_router, b_router, w_gate_up, b_gate_up, w_down, b_down):
    bsz, seq, d = x.shape
    depth = w_ada.shape[0]
    t = bsz * seq
    tq, tkc = TQ_DIFF, TKC_DIFF
    assert d == D_MODEL and tkc >= BLOCK and tq % tkc == 0
    assert TM_PROJ % tq == 0 and TM_PROJ % tkc == 0 and (seq // tkc) % KC_DIFF == 0
    assert seq % TM_PROJ == 0 and seq % TM_SWA == 0
    assert seq % TG_DISPATCH == 0 and seq % TC_COMBINE == 0

    mod = _modulation(c, w_ada, b_ada)
    bias_a, bias_t = _bias_tables(rel_bias_table, tq, tkc)
    n_rows = t * TOP_K + N_EXPERTS * BM_EXPERT
    n_blocks = n_rows // BM_EXPERT

    x2d = x.reshape(t, d)
    for l in range(depth):
        sh_m, sc_m, gt_m, sh_f, sc_f, gt_f = [v[:, None, :] for v in jnp.split(mod[l], 6, axis=-1)]
        lam_init = 0.8 - 0.6 * math.exp(-0.3 * l)
        w_nat, w_t = _prep_in_weights(w_in[l])
        qa, ka2, va2, kb, ga, gb, qbt, vbt = _in_projection(
            x2d, sc_m, sh_m, g_mix_pre[l][None, :], w_nat, w_t, bsz, seq)
        o_a = _swa_attention(qa.reshape(bsz, seq, QA_W), ka2.reshape(bsz, seq, 2 * KA_W),
                             va2.reshape(bsz, seq, 2 * VA_W), attn_sinks[l], bias_a)
        g_b = jnp.broadcast_to(diff_subln_g[l][:, None], (DV_B, tq)).astype(F32)
        o_b = _diff_attention(qbt, kb.reshape(bsz, seq, KB_W), vbt, bias_t, diff_lambda[l], g_b, lam_init)
        x1, h2r, top_idx, top_w, rank, counts = _post_mix(
            x2d, o_a.reshape(t, QA_W), o_b.reshape(t, H_B * DV_B), ga, gb, w_out[l].astype(BF16),
            g_mix_post[l][None, :], gt_m, g_ffn_pre[l][None, :], sc_f, sh_f,
            w_router[l].T, b_router[l][:, None], seq)

        cnt = counts[:, 0].astype(I32)
        padded = (cnt + BM_EXPERT - 1) // BM_EXPERT * BM_EXPERT
        pends = jnp.cumsum(padded)
        pstarts = pends - padded
        experts = jnp.arange(N_EXPERTS, dtype=I32)
        dest = jnp.sum(jnp.where(top_idx[..., None] == experts, pstarts, 0), axis=-1) + rank
        block_start = jnp.arange(n_blocks, dtype=I32) * BM_EXPERT
        block_e = jnp.minimum(jnp.sum((pends[None, :] <= block_start[:, None]).astype(I32), axis=1),
                              N_EXPERTS - 1).astype(I32)
        n_used = (pends[-1:] // BM_EXPERT).astype(I32)

        xs = _dispatch((pstarts + cnt).astype(I32), pends.astype(I32), _dest_blocks(dest, TG_DISPATCH), h2r,
                       n_rows, TG_DISPATCH)
        ys = _experts(block_e, n_used, xs, w_gate_up, b_gate_up[:, :, None, :], w_down, b_down[:, :, None, :], l)
        x2d = _combine(_dest_blocks(dest, TC_COMBINE), top_w.T, x1, gt_f, g_ffn_post[l][None, :], ys,
                       seq, TC_COMBINE)
    return x2d.reshape(bsz, seq, d)
```

```python
import functools
import math

import numpy as np
import jax
import jax.numpy as jnp
from jax import lax
from jax.experimental import pallas as pl
from jax.experimental.pallas import tpu as pltpu

F32 = jnp.float32
BF16 = jnp.bfloat16
I32 = jnp.int32

D_MODEL = 1024
BLOCK = 128
WINDOW = 128
H_A, KV_A, DH_A = 8, 2, 64
G_A = H_A // KV_A
H_B, DH_B = 4, 64
DV_B = 2 * DH_B
DV_EXT = DV_B + 16
N_BUCKETS = 32
MAX_EXACT = N_BUCKETS // 2
REL_MAX_DIST = 128
N_EXPERTS = 32
TOP_K = 4
D_FF = D_MODEL
SWIGLU_ALPHA = 1.702
SWIGLU_LIMIT = 7.0
NORM_EPS = 1e-6
SUBLN_EPS = 1e-5
QA_W = H_A * DH_A
KA_W = KV_A * DH_A
VA_W = KV_A * DH_A
QB_W = H_B * 2 * DH_B
KB_W = H_B * 2 * DH_B
VB_W = H_B * 2 * DH_B
SPLIT_SIZES = (QA_W, KA_W, VA_W, QB_W, KB_W, VB_W, D_MODEL, D_MODEL)

LANES = 128
SUBLANES = 8
ROW_CHUNKS = D_MODEL // LANES
MASK_VALUE = -1e30

TM_PROJ = 512
TQ_DIFF = 512
TKC_DIFF = 256
KC_DIFF = 4
LOG2E = math.log2(math.e)
TM_SWA = 512
BM_EXPERT = 512
TG_DISPATCH = 512
TC_COMBINE = 256
DMA_UNROLL = 8
VMEM_LIMIT = 48 * 1024 * 1024
VMEM_LIMIT_EXPERTS = 56 * 1024 * 1024


def _cparams(n_axes):
    return pltpu.CompilerParams(dimension_semantics=("arbitrary",) * n_axes,
                                vmem_limit_bytes=VMEM_LIMIT)


def _sigmoid(x):
    return 1.0 / (1.0 + jnp.exp(-x))


def _t5_bucket_np(dist):
    n = np.maximum(dist, 0)
    nf = np.maximum(n, 1).astype(np.float32)
    large = MAX_EXACT + (np.log(nf / np.float32(MAX_EXACT)) / np.float32(math.log(REL_MAX_DIST / MAX_EXACT))
                         * np.float32(N_BUCKETS - MAX_EXACT)).astype(np.int32)
    large = np.minimum(large, N_BUCKETS - 1)
    return np.where(n < MAX_EXACT, n, large)


def _mod_kernel(c_ref, w_ref, b_ref, o_ref):
    c = c_ref[...]
    ca = c * _sigmoid(c)
    o_ref[0] = jnp.dot(ca, w_ref[0], preferred_element_type=F32,
                       precision=lax.Precision.HIGHEST) + b_ref[0]


def _modulation(c, w_ada, b_ada):
    depth, d, n = w_ada.shape
    bsz = c.shape[0]
    rows = -(-bsz // SUBLANES) * SUBLANES
    c_pad = jnp.zeros((rows, d), F32).at[:bsz].set(c)
    tn = 1536
    out = pl.pallas_call(
        _mod_kernel,
        grid=(depth, n // tn),
        in_specs=[pl.BlockSpec((rows, d), lambda l, j: (0, 0)),
                  pl.BlockSpec((1, d, tn), lambda l, j: (l, 0, j)),
                  pl.BlockSpec((1, 1, tn), lambda l, j: (l, 0, j))],
        out_specs=pl.BlockSpec((1, rows, tn), lambda l, j: (l, 0, j)),
        out_shape=jax.ShapeDtypeStruct((depth, rows, n), F32),
        name="adaln_modulation",
        compiler_params=_cparams(2),
    )(c_pad, w_ada, b_ada.reshape(depth, 1, n))
    return out[:, :bsz]


N_NAT = QA_W + 2 * KA_W + 2 * VA_W + KB_W + 2 * D_MODEL
O_QA, O_KA, O_VA, O_KB, O_GA, O_GB = 0, 512, 768, 1024, 1536, 2560


def _inproj_kernel(x_ref, sc_ref, sh_ref, g_ref, w_ref, wt_ref,
                   qa_ref, ka_ref, va_ref, kb_ref, ga_ref, gb_ref, qbt_ref, vbt_ref):
    x = x_ref[...]
    ms = jnp.mean(x * x, axis=-1, keepdims=True)
    h = x * lax.rsqrt(ms + NORM_EPS) * g_ref[...]
    h = h * (1.0 + sc_ref[0]) + sh_ref[0]
    hb = h.astype(BF16)
    tm = x.shape[0]

    def proj(a, b):
        return jnp.dot(hb, w_ref[:, a:b], preferred_element_type=F32)

    qa_ref[...] = proj(O_QA, O_KA).astype(BF16)
    ka_ref[...] = proj(O_KA, O_VA).astype(BF16)
    va_ref[...] = proj(O_VA, O_KB).astype(BF16)
    kb_ref[...] = proj(O_KB, O_GA).astype(BF16)
    ga_ref[...] = _sigmoid(proj(O_GA, O_GB)).astype(BF16)
    gb_ref[...] = _sigmoid(proj(O_GB, N_NAT)).astype(BF16)
    for t_ref, base in ((qbt_ref, 0), (vbt_ref, QB_W)):
        rt = lax.dot_general(wt_ref[base: base + H_B * DV_B, :], hb,
                             (((1,), (1,)), ((), ())), preferred_element_type=F32)
        n_sub, width = t_ref.shape[2], t_ref.shape[4]
        for hh in range(H_B):
            for j in range(n_sub):
                t_ref[0, hh, j, :DV_B, :] = rt[hh * DV_B:(hh + 1) * DV_B, j * width:(j + 1) * width].astype(BF16)
    ones_shape = vbt_ref.shape[:3] + (DV_EXT - DV_B, vbt_ref.shape[4])
    vbt_ref[:, :, :, DV_B:, :] = jnp.ones(ones_shape, BF16)


def _in_projection(x2d, sc, sh, g, w_nat, w_t, bsz, seq):
    t, d = x2d.shape
    tm, tq, tkc = TM_PROJ, TQ_DIFF, TKC_DIFF
    spb = seq // tm
    row = lambda i: (i, 0)
    bat = lambda i: (i // spb, 0, 0)
    tr = lambda i: (i // spb, 0, i % spb, 0, 0)
    if tq <= tm:
        q_spec = pl.BlockSpec((1, H_B, tm // tq, DV_B, tq), tr)
    else:
        q_spec = pl.BlockSpec((1, H_B, 1, DV_B, tm),
                              lambda i: (i // spb, 0, (i % spb) // (tq // tm), 0, (i % spb) % (tq // tm)))
    outs = pl.pallas_call(
        _inproj_kernel,
        grid=(t // tm,),
        in_specs=[pl.BlockSpec((tm, d), row),
                  pl.BlockSpec((1, 1, d), bat),
                  pl.BlockSpec((1, 1, d), bat),
                  pl.BlockSpec((1, d), lambda i: (0, 0)),
                  pl.BlockSpec((d, N_NAT), lambda i: (0, 0)),
                  pl.BlockSpec((QB_W + VB_W, d), lambda i: (0, 0))],
        out_specs=[pl.BlockSpec((tm, QA_W), row),
                   pl.BlockSpec((tm, 2 * KA_W), row),
                   pl.BlockSpec((tm, 2 * VA_W), row),
                   pl.BlockSpec((tm, KB_W), row),
                   pl.BlockSpec((tm, d), row),
                   pl.BlockSpec((tm, d), row),
                   q_spec,
                   pl.BlockSpec((1, H_B, tm // tkc, DV_EXT, tkc), tr)],
        out_shape=[jax.ShapeDtypeStruct((t, QA_W), BF16),
                   jax.ShapeDtypeStruct((t, 2 * KA_W), BF16),
                   jax.ShapeDtypeStruct((t, 2 * VA_W), BF16),
                   jax.ShapeDtypeStruct((t, KB_W), BF16),
                   jax.ShapeDtypeStruct((t, d), BF16),
                   jax.ShapeDtypeStruct((t, d), BF16),
                   jax.ShapeDtypeStruct((bsz, H_B, seq // tq, DV_B, tq), BF16),
                   jax.ShapeDtypeStruct((bsz, H_B, seq // tkc, DV_EXT, tkc), BF16)],
        name="in_projection",
        compiler_params=_cparams(1),
    )(x2d, sc, sh, g, w_nat, w_t)
    return outs


def _swa_kernel(sink_ref, q_ref, kc_ref, kp_ref, vc_ref, vp_ref, bias_ref, o_ref):
    i = pl.program_id(1)
    nsub = q_ref.shape[1] // BLOCK
    lane_k = lax.broadcasted_iota(I32, (2 * BLOCK, LANES), 1)
    lane_s = lax.broadcasted_iota(I32, (BLOCK, 2 * BLOCK), 1)
    first_mask = jnp.where((i == 0) & (lane_s < BLOCK), MASK_VALUE, 0.0).astype(F32)
    zero_b = jnp.zeros((2 * BLOCK, LANES), BF16)
    for j in range(nsub):
        rows = slice(j * BLOCK, (j + 1) * BLOCK)
        if j == 0:
            kband = jnp.concatenate([kp_ref[0], kc_ref[0, rows, :]], axis=0)
            vband = jnp.concatenate([vp_ref[0], vc_ref[0, rows, :]], axis=0)
        else:
            kband = kc_ref[0, (j - 1) * BLOCK:(j + 1) * BLOCK, :]
            vband = vc_ref[0, (j - 1) * BLOCK:(j + 1) * BLOCK, :]
        for kv in range(KV_A):
            kslab = kband[:, kv * LANES:(kv + 1) * LANES]
            vslab = vband[:, kv * LANES:(kv + 1) * LANES]
            kpad = (jnp.where(lane_k < DH_A, kslab, zero_b), jnp.where(lane_k >= DH_A, kslab, zero_b))
            vpad = (jnp.where(lane_k < DH_A, vslab, zero_b), jnp.where(lane_k >= DH_A, vslab, zero_b))
            for pp in range(G_A // 2):
                pair = kv * (G_A // 2) + pp
                qslab = q_ref[0, rows, pair * LANES:(pair + 1) * LANES]
                o_slab = jnp.zeros((BLOCK, LANES), F32)
                for par in range(2):
                    head = 2 * pair + par
                    s = lax.dot_general(qslab, kpad[par], (((1,), (1,)), ((), ())),
                                        preferred_element_type=F32)
                    s = s + bias_ref[head]
                    if j == 0:
                        s = s + first_mask
                    sink = sink_ref[head]
                    m = jnp.maximum(jnp.max(s, axis=-1, keepdims=True), sink)
                    e = jnp.exp(s - m)
                    den = jnp.sum(e, axis=-1, keepdims=True) + jnp.exp(sink - m)
                    pv = jnp.dot(e.astype(BF16), vpad[par], preferred_element_type=F32)
                    o_slab = o_slab + pv / den
                o_ref[0, rows, pair * LANES:(pair + 1) * LANES] = o_slab.astype(BF16)


def _swa_attention(qa, ka2, va2, sinks, bias_a):
    bsz, seq, _ = qa.shape
    tm = TM_SWA
    nsub = tm // BLOCK
    cur = lambda b, i: (b, i, 0)
    prev = lambda b, i: (b, jnp.maximum(i * nsub - 1, 0), 0)
    return pl.pallas_call(
        _swa_kernel,
        grid=(bsz, seq // tm),
        in_specs=[pl.BlockSpec(memory_space=pltpu.SMEM),
                  pl.BlockSpec((1, tm, QA_W), cur),
                  pl.BlockSpec((1, tm, 2 * KA_W), cur),
                  pl.BlockSpec((1, BLOCK, 2 * KA_W), prev),
                  pl.BlockSpec((1, tm, 2 * VA_W), cur),
                  pl.BlockSpec((1, BLOCK, 2 * VA_W), prev),
                  pl.BlockSpec((H_A, BLOCK, 2 * BLOCK), lambda b, i: (0, 0, 0))],
        out_specs=pl.BlockSpec((1, tm, QA_W), cur),
        out_shape=jax.ShapeDtypeStruct((bsz, seq, QA_W), BF16),
        name="swa_attention",
        compiler_params=_cparams(2),
    )(sinks, qa, ka2, ka2, va2, va2, bias_a)


def _diffattn_kernel(lp_ref, qt_ref, k_ref, vt_ref, bias_ref, g_ref, o_ref, acc_ref, *s_bufs, tq, tkc, lam_init):
    kc = len(s_bufs)
    r = tq // tkc
    nqb = qt_ref.shape[2]
    last_chunk = vt_ref.shape[2] - 1
    row = lax.broadcasted_iota(I32, (DV_B, tq), 0)
    zero_q = jnp.zeros((DV_B, tq), BF16)
    lp = lp_ref[...]
    lam = (jnp.exp(jnp.sum(lp[0:1] * lp[1:2], axis=-1, keepdims=True))
           - jnp.exp(jnp.sum(lp[2:3] * lp[3:4], axis=-1, keepdims=True)) + lam_init)

    def n_groups(i):
        return ((i + 1) * r + kc - 1) // kc

    def q_both(i):
        qt = qt_ref[0, 0, i]
        return jnp.concatenate([jnp.where(row < DH_B, qt, zero_q), jnp.where(row >= DH_B, qt, zero_q)], axis=1)

    def scores_chunk(c, i, g, qb):
        blk = g * kc + c
        src = jnp.minimum(blk, last_chunk)
        kblk = k_ref[0, pl.ds(pl.multiple_of(src * tkc, tkc), tkc), :]
        s = jnp.dot(kblk, qb, preferred_element_type=F32)
        tile = bias_ref[0, jnp.clip(blk - i * r + 2, 0, r + 2)]
        s = s + jnp.concatenate([tile, tile], axis=1)
        s_bufs[c][...] = s
        return jnp.max(s, axis=0, keepdims=True)

    def finish(i):
        acc = acc_ref[:DV_B, :]
        l_fin = acc_ref[DV_B:DV_B + 1, :]
        o = acc[:, :tq] / l_fin[:, :tq] - lam * (acc[:, tq:] / l_fin[:, tq:])
        ms = jnp.mean(o * o, axis=0, keepdims=True)
        y = o * lax.rsqrt(ms + SUBLN_EPS) * g_ref[...] * (1.0 - lam_init)
        o_ref[0, pl.ds(pl.multiple_of(i * tq, tq), tq), :] = y.T.astype(BF16)

    def step(_, carry):
        i, g, m_prev, mx = carry
        last = g == n_groups(i) - 1
        i_next = jnp.minimum(jnp.where(last, i + 1, i), nqb - 1)
        g_next = jnp.where(last, 0, g + 1)
        qb_next = q_both(i_next)
        m_old = jnp.where(g == 0, MASK_VALUE, m_prev)
        m_new = jnp.maximum(m_old, mx)
        alpha = jnp.exp2(m_old - m_new)
        mx_next = jnp.full_like(mx, MASK_VALUE)
        pv = None
        for c in range(kc):
            p = jnp.exp2((s_bufs[c][...] - m_new).astype(BF16))
            vblk = vt_ref[0, 0, jnp.minimum(g * kc + c, last_chunk)]
            d = jnp.dot(vblk, p, preferred_element_type=F32)
            pv = d if pv is None else pv + d
            mx_next = jnp.maximum(mx_next, scores_chunk(c, i_next, g_next, qb_next))
        acc_ref[...] = alpha * acc_ref[...] + pv

        @pl.when(last)
        def _():
            finish(i)

        return i_next, g_next, m_new, mx_next

    acc_ref[...] = jnp.zeros_like(acc_ref)
    qb0 = q_both(0)
    mx0 = jnp.full((1, 2 * tq), MASK_VALUE, F32)
    for c in range(kc):
        mx0 = jnp.maximum(mx0, scores_chunk(c, 0, 0, qb0))
    n_steps = sum(((i + 1) * r + kc - 1) // kc for i in range(nqb))
    lax.fori_loop(0, n_steps, step, (jnp.int32(0), jnp.int32(0), jnp.full((1, 2 * tq), MASK_VALUE, F32), mx0))


def _diff_attention(qbt, kb, vbt, bias_t, lp, g_b, lam_init):
    bsz, _, nqb, _, tq = qbt.shape
    nkb, tkc = vbt.shape[2], vbt.shape[4]
    seq = nqb * tq
    n_tiles = bias_t.shape[1]
    return pl.pallas_call(
        functools.partial(_diffattn_kernel, tq=tq, tkc=tkc, lam_init=lam_init),
        grid=(bsz, H_B),
        in_specs=[pl.BlockSpec((4, DH_B), lambda b, h: (0, 0)),
                  pl.BlockSpec((1, 1, nqb, DV_B, tq), lambda b, h: (b, h, 0, 0, 0)),
                  pl.BlockSpec((1, seq, DV_B), lambda b, h: (b, 0, h)),
                  pl.BlockSpec((1, 1, nkb, DV_EXT, tkc), lambda b, h: (b, h, 0, 0, 0)),
                  pl.BlockSpec((1, n_tiles, tkc, tq), lambda b, h: (h, 0, 0, 0)),
                  pl.BlockSpec((DV_B, tq), lambda b, h: (0, 0))],
        out_specs=pl.BlockSpec((1, seq, DV_B), lambda b, h: (b, 0, h)),
        out_shape=jax.ShapeDtypeStruct((bsz, seq, H_B * DV_B), BF16),
        scratch_shapes=[pltpu.VMEM((DV_EXT, 2 * tq), F32)] + [pltpu.VMEM((tkc, 2 * tq), F32)] * KC_DIFF,
        name="diff_attention",
        compiler_params=_cparams(2),
    )(lp, qbt, kb, vbt, bias_t, g_b)


def _postmix_kernel(x_ref, oa_ref, ob_ref, ga_ref, gb_ref, wo_ref, gpost_ref, gt_ref, gpre_ref, sc_ref, sh_ref,
                    wr_ref, br_ref,
                    x1_ref, h2r_ref, idx_ref, w_ref, rank_ref, cnt_ref, carry_ref):
    step = pl.program_id(0)
    tm = x_ref.shape[0]

    @pl.when(step == 0)
    def _():
        carry_ref[...] = jnp.zeros_like(carry_ref)

    ya = jnp.dot(oa_ref[...], wo_ref[:QA_W, :], preferred_element_type=F32)
    yb = jnp.dot(ob_ref[...], wo_ref[QA_W:, :], preferred_element_type=F32)
    y = ga_ref[...].astype(F32) * ya + gb_ref[...].astype(F32) * yb
    ms = jnp.mean(y * y, axis=-1, keepdims=True)
    x1 = x_ref[...] + gt_ref[0] * (y * lax.rsqrt(ms + NORM_EPS) * gpost_ref[...])
    x1_ref[...] = x1
    ms1 = jnp.mean(x1 * x1, axis=-1, keepdims=True)
    h2 = x1 * lax.rsqrt(ms1 + NORM_EPS) * gpre_ref[...]
    h2 = h2 * (1.0 + sc_ref[0]) + sh_ref[0]
    for c in range(ROW_CHUNKS):
        h2r_ref[pl.ds(c, tm, stride=ROW_CHUNKS), :] = h2[:, c * LANES:(c + 1) * LANES]

    logits = lax.dot_general(wr_ref[...], h2, (((1,), (1,)), ((), ())), preferred_element_type=F32,
                             precision=lax.Precision.HIGHEST) + br_ref[...]
    erow = lax.broadcasted_iota(I32, logits.shape, 0)
    vals, sels = [], []
    cur = logits
    for k in range(TOP_K):
        mv = jnp.max(cur, axis=0, keepdims=True)
        idx = jnp.min(jnp.where(cur == mv, erow, N_EXPERTS), axis=0, keepdims=True)
        sel = erow == idx
        vals.append(mv)
        sels.append(sel)
        idx_ref[k:k + 1, :] = idx
        cur = jnp.where(sel, -jnp.inf, cur)
    es = [jnp.exp(v - vals[0]) for v in vals]
    den = es[0] + es[1] + es[2] + es[3]
    for k in range(TOP_K):
        w_ref[k:k + 1, :] = es[k] / den

    onehot = jnp.zeros(logits.shape, F32)
    for k in range(TOP_K):
        onehot = onehot + jnp.where(sels[k], 1.0, 0.0)
    srow = lax.broadcasted_iota(I32, (tm, tm), 0)
    scol = lax.broadcasted_iota(I32, (tm, tm), 1)
    upper = jnp.where(srow <= scol, 1.0, 0.0).astype(BF16)
    cum = jnp.dot(onehot.astype(BF16), upper, preferred_element_type=F32)
    carry = carry_ref[:, 0:1]
    excl = cum - onehot + carry
    for k in range(TOP_K):
        rank_ref[k:k + 1, :] = jnp.sum(jnp.where(sels[k], excl, 0.0), axis=0, keepdims=True).astype(I32)
    new_carry = carry + jnp.sum(onehot, axis=1, keepdims=True)
    carry_ref[...] = jnp.broadcast_to(new_carry, carry_ref.shape)
    cnt_ref[...] = jnp.broadcast_to(new_carry, cnt_ref.shape)


def _post_mix(x2d, oa, ob, ga, gb, wo, g_post, gt, g_pre, sc, sh, wr_t, br, seq):
    t, d = x2d.shape
    tm = TM_PROJ
    spb = seq // tm
    row = lambda i: (i, 0)
    bat = lambda i: (i // spb, 0, 0)
    const = lambda i: (0, 0)
    col = lambda i: (0, i)
    return pl.pallas_call(
        _postmix_kernel,
        grid=(t // tm,),
        in_specs=[pl.BlockSpec((tm, d), row),
                  pl.BlockSpec((tm, QA_W), row),
                  pl.BlockSpec((tm, H_B * DV_B), row),
                  pl.BlockSpec((tm, d), row),
                  pl.BlockSpec((tm, d), row),
                  pl.BlockSpec((QA_W + H_B * DV_B, d), const),
                  pl.BlockSpec((1, d), const),
                  pl.BlockSpec((1, 1, d), bat),
                  pl.BlockSpec((1, d), const),
                  pl.BlockSpec((1, 1, d), bat),
                  pl.BlockSpec((1, 1, d), bat),
                  pl.BlockSpec((N_EXPERTS, d), const),
                  pl.BlockSpec((N_EXPERTS, 1), const)],
        out_specs=[pl.BlockSpec((tm, d), row),
                   pl.BlockSpec((tm * ROW_CHUNKS, LANES), row),
                   pl.BlockSpec((TOP_K, tm), col),
                   pl.BlockSpec((TOP_K, tm), col),
                   pl.BlockSpec((TOP_K, tm), col),
                   pl.BlockSpec((N_EXPERTS, LANES), const)],
        out_shape=[jax.ShapeDtypeStruct((t, d), F32),
                   jax.ShapeDtypeStruct((t * ROW_CHUNKS, LANES), F32),
                   jax.ShapeDtypeStruct((TOP_K, t), I32),
                   jax.ShapeDtypeStruct((TOP_K, t), F32),
                   jax.ShapeDtypeStruct((TOP_K, t), I32),
                   jax.ShapeDtypeStruct((N_EXPERTS, LANES), F32)],
        scratch_shapes=[pltpu.VMEM((N_EXPERTS, LANES), F32)],
        name="post_mix_router",
        compiler_params=_cparams(1),
    )(x2d, oa, ob, ga, gb, wo, g_post, gt, g_pre, sc, sh, wr_t, br)


def _row_copy(src, src_row, dst, dst_row, sem):
    return pltpu.make_async_copy(src.at[pl.ds(pl.multiple_of(src_row * ROW_CHUNKS, ROW_CHUNKS), ROW_CHUNKS)],
                                 dst.at[pl.ds(pl.multiple_of(dst_row * ROW_CHUNKS, ROW_CHUNKS), ROW_CHUNKS)], sem)


def _dispatch_kernel(pad_lo_ref, pad_hi_ref, dest_ref, h2r_ref, xs_hbm, dsm, zero_row, sem_idx, sem, sem_pad, *, tg):
    def pad_rows(op):
        def per_expert(e, carry):
            return lax.fori_loop(pad_lo_ref[e], pad_hi_ref[e], lambda r, c: (op(r), c)[1], carry)
        lax.fori_loop(0, N_EXPERTS, per_expert, 0)

    @pl.when(pl.program_id(0) == 0)
    def _():
        zero_row[...] = jnp.zeros_like(zero_row)
        pad_rows(lambda r: _row_copy(zero_row, 0, xs_hbm, r, sem_pad).start())

    idx_copy = pltpu.make_async_copy(dest_ref.at[0, 0], dsm, sem_idx)
    idx_copy.start()
    idx_copy.wait()

    def issue(it, carry):
        for u in range(DMA_UNROLL):
            tok = it * DMA_UNROLL + u
            for k in range(TOP_K):
                _row_copy(h2r_ref, tok, xs_hbm, dsm[k * tg + tok], sem).start(priority=k % 2)
        return carry

    lax.fori_loop(0, tg // DMA_UNROLL, issue, 0)

    def drain(it, carry):
        for _ in range(DMA_UNROLL * TOP_K):
            _row_copy(h2r_ref, 0, xs_hbm, 0, sem).wait()
        return carry

    lax.fori_loop(0, tg // DMA_UNROLL, drain, 0)

    @pl.when(pl.program_id(0) == pl.num_programs(0) - 1)
    def _():
        pad_rows(lambda r: _row_copy(zero_row, 0, xs_hbm, 0, sem_pad).wait())


def _dispatch(pad_lo, pad_hi, dest_blocks, h2r, n_rows, tg):
    nsteps = dest_blocks.shape[0]
    grid_spec = pltpu.PrefetchScalarGridSpec(
        num_scalar_prefetch=2,
        grid=(nsteps,),
        in_specs=[pl.BlockSpec((1, 1, TOP_K * tg), lambda i, lo, hi: (i, 0, 0)),
                  pl.BlockSpec((tg * ROW_CHUNKS, LANES), lambda i, lo, hi: (i, 0))],
        out_specs=pl.BlockSpec(memory_space=pl.ANY),
        scratch_shapes=[pltpu.SMEM((TOP_K * tg,), I32), pltpu.VMEM((ROW_CHUNKS, LANES), F32),
                        pltpu.SemaphoreType.DMA, pltpu.SemaphoreType.DMA, pltpu.SemaphoreType.DMA],
    )
    return pl.pallas_call(
        functools.partial(_dispatch_kernel, tg=tg),
        grid_spec=grid_spec,
        out_shape=jax.ShapeDtypeStruct((n_rows * ROW_CHUNKS, LANES), F32),
        name="moe_dispatch",
        compiler_params=_cparams(1),
    )(pad_lo, pad_hi, dest_blocks, h2r)


def _expert_kernel(be_ref, nu_ref, xs_ref, wgu_ref, bgu_ref, wd_ref, bd_ref, ys_ref, wgu_bf, wd_bf, *, bm):
    i = pl.program_id(0)

    @pl.when((i == 0) | (be_ref[i] != be_ref[jnp.maximum(i - 1, 0)]))
    def _():
        wgu_bf[...] = wgu_ref[0, 0].astype(BF16)
        wd_bf[...] = wd_ref[0, 0].astype(BF16)

    @pl.when(i < nu_ref[0])
    def _():
        x = jnp.concatenate([xs_ref[pl.ds(c, bm, stride=ROW_CHUNKS), :] for c in range(ROW_CHUNKS)], axis=-1)
        gu = jnp.dot(x.astype(BF16), wgu_bf[...], preferred_element_type=F32) + bgu_ref[0, 0]
        gate = jnp.minimum(gu[:, :D_FF], SWIGLU_LIMIT)
        up = jnp.clip(gu[:, D_FF:], -SWIGLU_LIMIT, SWIGLU_LIMIT)
        act = gate * _sigmoid(SWIGLU_ALPHA * gate) * (up + 1.0)
        y = jnp.dot(act.astype(BF16), wd_bf[...], preferred_element_type=F32) + bd_ref[0, 0]
        for c in range(ROW_CHUNKS):
            ys_ref[pl.ds(c, bm, stride=ROW_CHUNKS), :] = y[:, c * LANES:(c + 1) * LANES]

    @pl.when(i >= nu_ref[0])
    def _():
        ys_ref[...] = jnp.zeros_like(ys_ref)


def _experts(block_e, n_used, xs, w_gate_up, b_gate_up, w_down, b_down, layer):
    bm = BM_EXPERT
    n_blocks = xs.shape[0] // (bm * ROW_CHUNKS)
    d = D_MODEL
    wsel = lambda i, be, nu: (layer, be[i], 0, 0)
    grid_spec = pltpu.PrefetchScalarGridSpec(
        num_scalar_prefetch=2,
        grid=(n_blocks,),
        in_specs=[pl.BlockSpec((bm * ROW_CHUNKS, LANES), lambda i, be, nu: (jnp.minimum(i, nu[0] - 1), 0)),
                  pl.BlockSpec((1, 1, d, 2 * D_FF), wsel),
                  pl.BlockSpec((1, 1, 1, 2 * D_FF), wsel),
                  pl.BlockSpec((1, 1, D_FF, d), wsel),
                  pl.BlockSpec((1, 1, 1, d), wsel)],
        out_specs=pl.BlockSpec((bm * ROW_CHUNKS, LANES), lambda i, be, nu: (i, 0)),
        scratch_shapes=[pltpu.VMEM((d, 2 * D_FF), BF16), pltpu.VMEM((D_FF, d), BF16)],
    )
    return pl.pallas_call(
        functools.partial(_expert_kernel, bm=bm),
        grid_spec=grid_spec,
        out_shape=jax.ShapeDtypeStruct(xs.shape, F32),
        name="moe_experts",
        compiler_params=pltpu.CompilerParams(dimension_semantics=("arbitrary",),
                                             vmem_limit_bytes=VMEM_LIMIT_EXPERTS),
    )(block_e, n_used, xs, w_gate_up, b_gate_up, w_down, b_down)


def _combine_kernel(dest_ref, w_ref, x_ref, gt_ref, g_ref, ys_hbm, o_ref, dsm, buf, sem_idx, sem, *, tc):
    idx_copy = pltpu.make_async_copy(dest_ref.at[0, 0], dsm, sem_idx)
    idx_copy.start()
    idx_copy.wait()

    def issue(it, carry):
        for u in range(DMA_UNROLL):
            tok = it * DMA_UNROLL + u
            for k in range(TOP_K):
                _row_copy(ys_hbm, dsm[k * tc + tok], buf, k * tc + tok, sem).start(priority=k % 2)
        return carry

    lax.fori_loop(0, tc // DMA_UNROLL, issue, 0)

    def drain(it, carry):
        for _ in range(DMA_UNROLL * TOP_K):
            _row_copy(ys_hbm, 0, buf, 0, sem).wait()
        return carry

    lax.fori_loop(0, tc // DMA_UNROLL, drain, 0)

    y = jnp.zeros((tc, D_MODEL), F32)
    for k in range(TOP_K):
        yk = jnp.concatenate([buf[pl.ds(k * tc * ROW_CHUNKS + c, tc, stride=ROW_CHUNKS), :]
                              for c in range(ROW_CHUNKS)], axis=-1)
        y = y + w_ref[:, k:k + 1] * yk
    ms = jnp.mean(y * y, axis=-1, keepdims=True)
    o_ref[...] = x_ref[...] + gt_ref[0] * (y * lax.rsqrt(ms + NORM_EPS) * g_ref[...])


def _combine(dest_blocks, w_tok, x1, gt, g_post, ys, seq, tc):
    t, d = x1.shape
    spb = seq // tc
    return pl.pallas_call(
        functools.partial(_combine_kernel, tc=tc),
        grid=(t // tc,),
        in_specs=[pl.BlockSpec((1, 1, TOP_K * tc), lambda i: (i, 0, 0)),
                  pl.BlockSpec((tc, TOP_K), lambda i: (i, 0)),
                  pl.BlockSpec((tc, d), lambda i: (i, 0)),
                  pl.BlockSpec((1, 1, d), lambda i: (i // spb, 0, 0)),
                  pl.BlockSpec((1, d), lambda i: (0, 0)),
                  pl.BlockSpec(memory_space=pl.ANY)],
        out_specs=pl.BlockSpec((tc, d), lambda i: (i, 0)),
        out_shape=jax.ShapeDtypeStruct((t, d), F32),
        scratch_shapes=[pltpu.SMEM((TOP_K * tc,), I32),
                        pltpu.VMEM((TOP_K * tc * ROW_CHUNKS, LANES), F32),
                        pltpu.SemaphoreType.DMA, pltpu.SemaphoreType.DMA],
        name="moe_combine",
        compiler_params=_cparams(1),
    )(dest_blocks, w_tok, x1, gt, g_post, ys)


def _dest_blocks(dest, step_tokens):
    k, t = dest.shape
    return dest.reshape(k, t // step_tokens, step_tokens).transpose(1, 0, 2).reshape(t // step_tokens, 1,
                                                                                      k * step_tokens)


def _bias_tables(rel_bias_table, tq, tkc):
    def lookup(dist, table):
        onehot = np.eye(N_BUCKETS, dtype=np.float32)[_t5_bucket_np(dist)]
        return jnp.einsum('...n,nh->h...', onehot, table, precision=lax.Precision.HIGHEST)

    qi = np.arange(BLOCK)[:, None]
    kj = np.arange(2 * BLOCK)[None, :]
    dist = qi + BLOCK - kj
    valid = (dist >= 0) & (dist < WINDOW)
    bias_a = jnp.where(valid[None], lookup(dist, rel_bias_table[:, :H_A]), MASK_VALUE)
    kk = np.arange(tkc)[:, None]
    qq = np.arange(tq)[None, :]
    table_b = rel_bias_table[:, H_A:]
    table_b = (table_b - table_b[N_BUCKETS - 1:]) * LOG2E
    tiles = [jnp.zeros((H_B, tkc, tq), F32)]
    for rel in range(-1, tq // tkc):
        d = qq - (rel * tkc + kk)
        tiles.append(jnp.where((d >= 0)[None], lookup(d, table_b), MASK_VALUE))
    tiles.append(jnp.full((H_B, tkc, tq), MASK_VALUE, F32))
    bias_t = jnp.stack(tiles, axis=1)
    return bias_a.astype(F32), bias_t.astype(F32)


def _prep_in_weights(w_in_l):
    qa, ka, va, qb, kb, vb, gza, gzb = jnp.split(w_in_l, [int(v) for v in np.cumsum(SPLIT_SIZES)[:-1]], axis=-1)

    def dup(w):
        return jnp.concatenate([w[:, :DH_A], w[:, :DH_A], w[:, DH_A:], w[:, DH_A:]], axis=-1)

    w_nat = jnp.concatenate([qa * DH_A ** -0.5, dup(ka), dup(va), kb, gza, gzb], axis=-1).astype(BF16)
    w_t = jnp.concatenate([qb * (DH_B ** -0.5 * LOG2E), vb], axis=-1).T.astype(BF16)
    return w_nat, w_t


def kernel(x, c, rel_bias_table, w_ada, b_ada, g_mix_pre, g_mix_post, g_ffn_pre, g_ffn_post, w_in, w_out,
           attn_sinks, diff_lambda, diff_subln_g, w_router, b_router, w_gate_up, b_gate_up, w_down, b_down):
    bsz, seq, d = x.shape
    depth = w_ada.shape[0]
    t = bsz * seq
    tq, tkc = TQ_DIFF, TKC_DIFF
    assert d == D_MODEL and tkc >= BLOCK and tq % tkc == 0
    assert (TM_PROJ % tq == 0 or tq % TM_PROJ == 0) and TM_PROJ % tkc == 0
    assert seq % tq == 0 and (seq // tkc) % KC_DIFF == 0
    assert seq % TM_PROJ == 0 and seq % TM_SWA == 0
    assert seq % TG_DISPATCH == 0 and seq % TC_COMBINE == 0

    mod = _modulation(c, w_ada, b_ada)
    bias_a, bias_t = _bias_tables(rel_bias_table, tq, tkc)
    n_rows = t * TOP_K + N_EXPERTS * BM_EXPERT
    n_blocks = n_rows // BM_EXPERT

    x2d = x.reshape(t, d)
    for l in range(depth):
        sh_m, sc_m, gt_m, sh_f, sc_f, gt_f = [v[:, None, :] for v in jnp.split(mod[l], 6, axis=-1)]
        lam_init = 0.8 - 0.6 * math.exp(-0.3 * l)
        w_nat, w_t = _prep_in_weights(w_in[l])
        qa, ka2, va2, kb, ga, gb, qbt, vbt = _in_projection(
            x2d, sc_m, sh_m, g_mix_pre[l][None, :], w_nat, w_t, bsz, seq)
        o_a = _swa_attention(qa.reshape(bsz, seq, QA_W), ka2.reshape(bsz, seq, 2 * KA_W),
                             va2.reshape(bsz, seq, 2 * VA_W), attn_sinks[l], bias_a)
        g_b = jnp.broadcast_to(diff_subln_g[l][:, None], (DV_B, tq)).astype(F32)
        o_b = _diff_attention(qbt, kb.reshape(bsz, seq, KB_W), vbt, bias_t, diff_lambda[l], g_b, lam_init)
        x1, h2r, top_idx, top_w, rank, counts = _post_mix(
            x2d, o_a.reshape(t, QA_W), o_b.reshape(t, H_B * DV_B), ga, gb, w_out[l].astype(BF16),
            g_mix_post[l][None, :], gt_m, g_ffn_pre[l][None, :], sc_f, sh_f,
            w_router[l].T, b_router[l][:, None], seq)

        cnt = counts[:, 0].astype(I32)
        padded = (cnt + BM_EXPERT - 1) // BM_EXPERT * BM_EXPERT
        pends = jnp.cumsum(padded)
        pstarts = pends - padded
        experts = jnp.arange(N_EXPERTS, dtype=I32)
        dest = jnp.sum(jnp.where(top_idx[..., None] == experts, pstarts, 0), axis=-1) + rank
        block_start = jnp.arange(n_blocks, dtype=I32) * BM_EXPERT
        block_e = jnp.minimum(jnp.sum((pends[None, :] <= block_start[:, None]).astype(I32), axis=1),
                              N_EXPERTS - 1).astype(I32)
        n_used = (pends[-1:] // BM_EXPERT).astype(I32)

        xs = _dispatch((pstarts + cnt).astype(I32), pends.astype(I32), _dest_blocks(dest, TG_DISPATCH), h2r,
                       n_rows, TG_DISPATCH)
        ys = _experts(block_e, n_used, xs, w_gate_up, b_gate_up[:, :, None, :], w_down, b_down[:, :, None, :], l)
        x2d = _combine(_dest_blocks(dest, TC_COMBINE), top_w.T, x1, gt_f, g_ffn_post[l][None, :], ys,
                       seq, TC_COMBINE)
    return x2d.reshape(bsz, seq, d)
```

```python
import functools
import math

import numpy as np
import jax
import jax.numpy as jnp
from jax import lax
from jax.experimental import pallas as pl
from jax.experimental.pallas import tpu as pltpu

F32 = jnp.float32
BF16 = jnp.bfloat16
I32 = jnp.int32

D_MODEL = 1024
BLOCK = 128
WINDOW = 128
H_A, KV_A, DH_A = 8, 2, 64
G_A = H_A // KV_A
H_B, DH_B = 4, 64
DV_B = 2 * DH_B
DV_EXT = DV_B + 16
N_BUCKETS = 32
MAX_EXACT = N_BUCKETS // 2
REL_MAX_DIST = 128
N_EXPERTS = 32
TOP_K = 4
D_FF = D_MODEL
SWIGLU_ALPHA = 1.702
SWIGLU_LIMIT = 7.0
NORM_EPS = 1e-6
SUBLN_EPS = 1e-5
QA_W = H_A * DH_A
KA_W = KV_A * DH_A
VA_W = KV_A * DH_A
QB_W = H_B * 2 * DH_B
KB_W = H_B * 2 * DH_B
VB_W = H_B * 2 * DH_B
SPLIT_SIZES = (QA_W, KA_W, VA_W, QB_W, KB_W, VB_W, D_MODEL, D_MODEL)

LANES = 128
SUBLANES = 8
ROW_CHUNKS = D_MODEL // LANES
MASK_VALUE = -1e30

TM_PROJ = 512
TQ_DIFF = 512
TKC_DIFF = 256
KC_DIFF = 4
LOG2E = math.log2(math.e)
TM_SWA = 512
BM_EXPERT = 512
PAD_RUNS = tuple(BM_EXPERT >> s for s in range(1, BM_EXPERT.bit_length()))
TG_DISPATCH = 512
TC_COMBINE = 256
DMA_UNROLL = 8
VMEM_LIMIT = 48 * 1024 * 1024
VMEM_LIMIT_EXPERTS = 56 * 1024 * 1024


def _cparams(n_axes):
    return pltpu.CompilerParams(dimension_semantics=("arbitrary",) * n_axes,
                                vmem_limit_bytes=VMEM_LIMIT)


def _sigmoid(x):
    return 1.0 / (1.0 + jnp.exp(-x))


def _t5_bucket_np(dist):
    n = np.maximum(dist, 0)
    nf = np.maximum(n, 1).astype(np.float32)
    large = MAX_EXACT + (np.log(nf / np.float32(MAX_EXACT)) / np.float32(math.log(REL_MAX_DIST / MAX_EXACT))
                         * np.float32(N_BUCKETS - MAX_EXACT)).astype(np.int32)
    large = np.minimum(large, N_BUCKETS - 1)
    return np.where(n < MAX_EXACT, n, large)


def _mod_kernel(c_ref, w_ref, b_ref, o_ref):
    c = c_ref[...]
    ca = c * _sigmoid(c)
    o_ref[0] = jnp.dot(ca, w_ref[0], preferred_element_type=F32,
                       precision=lax.Precision.HIGHEST) + b_ref[0]


def _modulation(c, w_ada, b_ada):
    depth, d, n = w_ada.shape
    bsz = c.shape[0]
    rows = -(-bsz // SUBLANES) * SUBLANES
    c_pad = jnp.zeros((rows, d), F32).at[:bsz].set(c)
    tn = 1536
    out = pl.pallas_call(
        _mod_kernel,
        grid=(depth, n // tn),
        in_specs=[pl.BlockSpec((rows, d), lambda l, j: (0, 0)),
                  pl.BlockSpec((1, d, tn), lambda l, j: (l, 0, j)),
                  pl.BlockSpec((1, 1, tn), lambda l, j: (l, 0, j))],
        out_specs=pl.BlockSpec((1, rows, tn), lambda l, j: (l, 0, j)),
        out_shape=jax.ShapeDtypeStruct((depth, rows, n), F32),
        name="adaln_modulation",
        compiler_params=_cparams(2),
    )(c_pad, w_ada, b_ada.reshape(depth, 1, n))
    return out[:, :bsz]


N_NAT = QA_W + 2 * KA_W + 2 * VA_W + KB_W + 2 * D_MODEL
O_QA, O_KA, O_VA, O_KB, O_GA, O_GB = 0, 512, 768, 1024, 1536, 2560


def _inproj_kernel(x_ref, sc_ref, sh_ref, g_ref, w_ref, wt_ref,
                   qa_ref, ka_ref, va_ref, kb_ref, ga_ref, gb_ref, qbt_ref, vbt_ref):
    x = x_ref[...]
    ms = jnp.mean(x * x, axis=-1, keepdims=True)
    h = x * lax.rsqrt(ms + NORM_EPS) * g_ref[...]
    h = h * (1.0 + sc_ref[0]) + sh_ref[0]
    hb = h.astype(BF16)
    tm = x.shape[0]

    def proj(a, b):
        return jnp.dot(hb, w_ref[:, a:b], preferred_element_type=F32)

    qa_ref[...] = proj(O_QA, O_KA).astype(BF16)
    ka_ref[...] = proj(O_KA, O_VA).astype(BF16)
    va_ref[...] = proj(O_VA, O_KB).astype(BF16)
    kb_ref[...] = proj(O_KB, O_GA).astype(BF16)
    ga_ref[...] = _sigmoid(proj(O_GA, O_GB)).astype(BF16)
    gb_ref[...] = _sigmoid(proj(O_GB, N_NAT)).astype(BF16)
    for t_ref, base in ((qbt_ref, 0), (vbt_ref, QB_W)):
        rt = lax.dot_general(wt_ref[base: base + H_B * DV_B, :], hb,
                             (((1,), (1,)), ((), ())), preferred_element_type=F32)
        n_sub, width = t_ref.shape[2], t_ref.shape[4]
        for hh in range(H_B):
            for j in range(n_sub):
                t_ref[0, hh, j, :DV_B, :] = rt[hh * DV_B:(hh + 1) * DV_B, j * width:(j + 1) * width].astype(BF16)
    ones_shape = vbt_ref.shape[:3] + (DV_EXT - DV_B, vbt_ref.shape[4])
    vbt_ref[:, :, :, DV_B:, :] = jnp.ones(ones_shape, BF16)


def _in_projection(x2d, sc, sh, g, w_nat, w_t, bsz, seq):
    t, d = x2d.shape
    tm, tq, tkc = TM_PROJ, TQ_DIFF, TKC_DIFF
    spb = seq // tm
    row = lambda i: (i, 0)
    bat = lambda i: (i // spb, 0, 0)
    tr = lambda i: (i // spb, 0, i % spb, 0, 0)
    if tq <= tm:
        q_spec = pl.BlockSpec((1, H_B, tm // tq, DV_B, tq), tr)
    else:
        q_spec = pl.BlockSpec((1, H_B, 1, DV_B, tm),
                              lambda i: (i // spb, 0, (i % spb) // (tq // tm), 0, (i % spb) % (tq // tm)))
    outs = pl.pallas_call(
        _inproj_kernel,
        grid=(t // tm,),
        in_specs=[pl.BlockSpec((tm, d), row),
                  pl.BlockSpec((1, 1, d), bat),
                  pl.BlockSpec((1, 1, d), bat),
                  pl.BlockSpec((1, d), lambda i: (0, 0)),
                  pl.BlockSpec((d, N_NAT), lambda i: (0, 0)),
                  pl.BlockSpec((QB_W + VB_W, d), lambda i: (0, 0))],
        out_specs=[pl.BlockSpec((tm, QA_W), row),
                   pl.BlockSpec((tm, 2 * KA_W), row),
                   pl.BlockSpec((tm, 2 * VA_W), row),
                   pl.BlockSpec((tm, KB_W), row),
                   pl.BlockSpec((tm, d), row),
                   pl.BlockSpec((tm, d), row),
                   q_spec,
                   pl.BlockSpec((1, H_B, tm // tkc, DV_EXT, tkc), tr)],
        out_shape=[jax.ShapeDtypeStruct((t, QA_W), BF16),
                   jax.ShapeDtypeStruct((t, 2 * KA_W), BF16),
                   jax.ShapeDtypeStruct((t, 2 * VA_W), BF16),
                   jax.ShapeDtypeStruct((t, KB_W), BF16),
                   jax.ShapeDtypeStruct((t, d), BF16),
                   jax.ShapeDtypeStruct((t, d), BF16),
                   jax.ShapeDtypeStruct((bsz, H_B, seq // tq, DV_B, tq), BF16),
                   jax.ShapeDtypeStruct((bsz, H_B, seq // tkc, DV_EXT, tkc), BF16)],
        name="in_projection",
        compiler_params=_cparams(1),
    )(x2d, sc, sh, g, w_nat, w_t)
    return outs


def _swa_kernel(sink_ref, q_ref, kc_ref, kp_ref, vc_ref, vp_ref, bias_ref, o_ref):
    i = pl.program_id(1)
    nsub = q_ref.shape[1] // BLOCK
    lane_k = lax.broadcasted_iota(I32, (2 * BLOCK, LANES), 1)
    lane_s = lax.broadcasted_iota(I32, (BLOCK, 2 * BLOCK), 1)
    first_mask = jnp.where((i == 0) & (lane_s < BLOCK), MASK_VALUE, 0.0).astype(F32)
    zero_b = jnp.zeros((2 * BLOCK, LANES), BF16)
    for j in range(nsub):
        rows = slice(j * BLOCK, (j + 1) * BLOCK)
        if j == 0:
            kband = jnp.concatenate([kp_ref[0], kc_ref[0, rows, :]], axis=0)
            vband = jnp.concatenate([vp_ref[0], vc_ref[0, rows, :]], axis=0)
        else:
            kband = kc_ref[0, (j - 1) * BLOCK:(j + 1) * BLOCK, :]
            vband = vc_ref[0, (j - 1) * BLOCK:(j + 1) * BLOCK, :]
        for kv in range(KV_A):
            kslab = kband[:, kv * LANES:(kv + 1) * LANES]
            vslab = vband[:, kv * LANES:(kv + 1) * LANES]
            kpad = (jnp.where(lane_k < DH_A, kslab, zero_b), jnp.where(lane_k >= DH_A, kslab, zero_b))
            vpad = (jnp.where(lane_k < DH_A, vslab, zero_b), jnp.where(lane_k >= DH_A, vslab, zero_b))
            for pp in range(G_A // 2):
                pair = kv * (G_A // 2) + pp
                qslab = q_ref[0, rows, pair * LANES:(pair + 1) * LANES]
                o_slab = jnp.zeros((BLOCK, LANES), F32)
                for par in range(2):
                    head = 2 * pair + par
                    s = lax.dot_general(qslab, kpad[par], (((1,), (1,)), ((), ())),
                                        preferred_element_type=F32)
                    s = s + bias_ref[head]
                    if j == 0:
                        s = s + first_mask
                    sink = sink_ref[head]
                    m = jnp.maximum(jnp.max(s, axis=-1, keepdims=True), sink)
                    e = jnp.exp(s - m)
                    den = jnp.sum(e, axis=-1, keepdims=True) + jnp.exp(sink - m)
                    pv = jnp.dot(e.astype(BF16), vpad[par], preferred_element_type=F32)
                    o_slab = o_slab + pv / den
                o_ref[0, rows, pair * LANES:(pair + 1) * LANES] = o_slab.astype(BF16)


def _swa_attention(qa, ka2, va2, sinks, bias_a):
    bsz, seq, _ = qa.shape
    tm = TM_SWA
    nsub = tm // BLOCK
    cur = lambda b, i: (b, i, 0)
    prev = lambda b, i: (b, jnp.maximum(i * nsub - 1, 0), 0)
    return pl.pallas_call(
        _swa_kernel,
        grid=(bsz, seq // tm),
        in_specs=[pl.BlockSpec(memory_space=pltpu.SMEM),
                  pl.BlockSpec((1, tm, QA_W), cur),
                  pl.BlockSpec((1, tm, 2 * KA_W), cur),
                  pl.BlockSpec((1, BLOCK, 2 * KA_W), prev),
                  pl.BlockSpec((1, tm, 2 * VA_W), cur),
                  pl.BlockSpec((1, BLOCK, 2 * VA_W), prev),
                  pl.BlockSpec((H_A, BLOCK, 2 * BLOCK), lambda b, i: (0, 0, 0))],
        out_specs=pl.BlockSpec((1, tm, QA_W), cur),
        out_shape=jax.ShapeDtypeStruct((bsz, seq, QA_W), BF16),
        name="swa_attention",
        compiler_params=_cparams(2),
    )(sinks, qa, ka2, ka2, va2, va2, bias_a)


def _diffattn_kernel(lp_ref, qt_ref, k_ref, vt_ref, bias_ref, g_ref, o_ref, acc_ref, *s_bufs, tq, tkc, lam_init):
    kc = len(s_bufs)
    r = tq // tkc
    nqb = qt_ref.shape[2]
    last_chunk = vt_ref.shape[2] - 1
    row = lax.broadcasted_iota(I32, (DV_B, tq), 0)
    zero_q = jnp.zeros((DV_B, tq), BF16)
    lp = lp_ref[...]
    lam = (jnp.exp(jnp.sum(lp[0:1] * lp[1:2], axis=-1, keepdims=True))
           - jnp.exp(jnp.sum(lp[2:3] * lp[3:4], axis=-1, keepdims=True)) + lam_init)

    def n_groups(i):
        return ((i + 1) * r + kc - 1) // kc

    def q_both(i):
        qt = qt_ref[0, 0, i]
        return jnp.concatenate([jnp.where(row < DH_B, qt, zero_q), jnp.where(row >= DH_B, qt, zero_q)], axis=1)

    def scores_chunk(c, i, g, qb):
        blk = g * kc + c
        src = jnp.minimum(blk, last_chunk)
        kblk = k_ref[0, pl.ds(pl.multiple_of(src * tkc, tkc), tkc), :]
        s = jnp.dot(kblk, qb, preferred_element_type=F32)
        tile = bias_ref[0, jnp.clip(blk - i * r + 2, 0, r + 2)]
        s = s + jnp.concatenate([tile, tile], axis=1)
        s_bufs[c][...] = s
        return jnp.max(s, axis=0, keepdims=True)

    def finish(i):
        acc = acc_ref[:DV_B, :]
        l_fin = acc_ref[DV_B:DV_B + 1, :]
        o = acc[:, :tq] / l_fin[:, :tq] - lam * (acc[:, tq:] / l_fin[:, tq:])
        ms = jnp.mean(o * o, axis=0, keepdims=True)
        y = o * lax.rsqrt(ms + SUBLN_EPS) * g_ref[...] * (1.0 - lam_init)
        o_ref[0, pl.ds(pl.multiple_of(i * tq, tq), tq), :] = y.T.astype(BF16)

    def step(_, carry):
        i, g, m_prev, mx = carry
        last = g == n_groups(i) - 1
        i_next = jnp.minimum(jnp.where(last, i + 1, i), nqb - 1)
        g_next = jnp.where(last, 0, g + 1)
        qb_next = q_both(i_next)
        m_old = jnp.where(g == 0, MASK_VALUE, m_prev)
        m_new = jnp.maximum(m_old, mx)
        alpha = jnp.exp2(m_old - m_new)
        mx_next = jnp.full_like(mx, MASK_VALUE)
        pv = None
        for c in range(kc):
            p = jnp.exp2((s_bufs[c][...] - m_new).astype(BF16))
            vblk = vt_ref[0, 0, jnp.minimum(g * kc + c, last_chunk)]
            d = jnp.dot(vblk, p, preferred_element_type=F32)
            pv = d if pv is None else pv + d
            mx_next = jnp.maximum(mx_next, scores_chunk(c, i_next, g_next, qb_next))
        acc_ref[...] = alpha * acc_ref[...] + pv

        @pl.when(last)
        def _():
            finish(i)

        return i_next, g_next, m_new, mx_next

    acc_ref[...] = jnp.zeros_like(acc_ref)
    qb0 = q_both(0)
    mx0 = jnp.full((1, 2 * tq), MASK_VALUE, F32)
    for c in range(kc):
        mx0 = jnp.maximum(mx0, scores_chunk(c, 0, 0, qb0))
    n_steps = sum(((i + 1) * r + kc - 1) // kc for i in range(nqb))
    lax.fori_loop(0, n_steps, step, (jnp.int32(0), jnp.int32(0), jnp.full((1, 2 * tq), MASK_VALUE, F32), mx0))


def _diff_attention(qbt, kb, vbt, bias_t, lp, g_b, lam_init):
    bsz, _, nqb, _, tq = qbt.shape
    nkb, tkc = vbt.shape[2], vbt.shape[4]
    seq = nqb * tq
    n_tiles = bias_t.shape[1]
    return pl.pallas_call(
        functools.partial(_diffattn_kernel, tq=tq, tkc=tkc, lam_init=lam_init),
        grid=(bsz, H_B),
        in_specs=[pl.BlockSpec((4, DH_B), lambda b, h: (0, 0)),
                  pl.BlockSpec((1, 1, nqb, DV_B, tq), lambda b, h: (b, h, 0, 0, 0)),
                  pl.BlockSpec((1, seq, DV_B), lambda b, h: (b, 0, h)),
                  pl.BlockSpec((1, 1, nkb, DV_EXT, tkc), lambda b, h: (b, h, 0, 0, 0)),
                  pl.BlockSpec((1, n_tiles, tkc, tq), lambda b, h: (h, 0, 0, 0)),
                  pl.BlockSpec((DV_B, tq), lambda b, h: (0, 0))],
        out_specs=pl.BlockSpec((1, seq, DV_B), lambda b, h: (b, 0, h)),
        out_shape=jax.ShapeDtypeStruct((bsz, seq, H_B * DV_B), BF16),
        scratch_shapes=[pltpu.VMEM((DV_EXT, 2 * tq), F32)] + [pltpu.VMEM((tkc, 2 * tq), F32)] * KC_DIFF,
        name="diff_attention",
        compiler_params=_cparams(2),
    )(lp, qbt, kb, vbt, bias_t, g_b)


def _postmix_kernel(x_ref, oa_ref, ob_ref, ga_ref, gb_ref, wo_ref, gpost_ref, gt_ref, gpre_ref, sc_ref, sh_ref,
                    wr_ref, br_ref,
                    x1_ref, h2r_ref, idx_ref, w_ref, rank_ref, cnt_ref, carry_ref):
    step = pl.program_id(0)
    tm = x_ref.shape[0]

    @pl.when(step == 0)
    def _():
        carry_ref[...] = jnp.zeros_like(carry_ref)

    ya = jnp.dot(oa_ref[...], wo_ref[:QA_W, :], preferred_element_type=F32)
    yb = jnp.dot(ob_ref[...], wo_ref[QA_W:, :], preferred_element_type=F32)
    y = ga_ref[...].astype(F32) * ya + gb_ref[...].astype(F32) * yb
    ms = jnp.mean(y * y, axis=-1, keepdims=True)
    x1 = x_ref[...] + gt_ref[0] * (y * lax.rsqrt(ms + NORM_EPS) * gpost_ref[...])
    x1_ref[...] = x1
    ms1 = jnp.mean(x1 * x1, axis=-1, keepdims=True)
    h2 = x1 * lax.rsqrt(ms1 + NORM_EPS) * gpre_ref[...]
    h2 = h2 * (1.0 + sc_ref[0]) + sh_ref[0]
    for c in range(ROW_CHUNKS):
        h2r_ref[pl.ds(c, tm, stride=ROW_CHUNKS), :] = h2[:, c * LANES:(c + 1) * LANES]

    logits = lax.dot_general(wr_ref[...], h2, (((1,), (1,)), ((), ())), preferred_element_type=F32,
                             precision=lax.Precision.HIGHEST) + br_ref[...]
    erow = lax.broadcasted_iota(I32, logits.shape, 0)
    vals, sels = [], []
    cur = logits
    for k in range(TOP_K):
        mv = jnp.max(cur, axis=0, keepdims=True)
        idx = jnp.min(jnp.where(cur == mv, erow, N_EXPERTS), axis=0, keepdims=True)
        sel = erow == idx
        vals.append(mv)
        sels.append(sel)
        idx_ref[k:k + 1, :] = idx
        cur = jnp.where(sel, -jnp.inf, cur)
    es = [jnp.exp(v - vals[0]) for v in vals]
    den = es[0] + es[1] + es[2] + es[3]
    for k in range(TOP_K):
        w_ref[k:k + 1, :] = es[k] / den

    onehot = jnp.zeros(logits.shape, F32)
    for k in range(TOP_K):
        onehot = onehot + jnp.where(sels[k], 1.0, 0.0)
    srow = lax.broadcasted_iota(I32, (tm, tm), 0)
    scol = lax.broadcasted_iota(I32, (tm, tm), 1)
    upper = jnp.where(srow <= scol, 1.0, 0.0).astype(BF16)
    cum = jnp.dot(onehot.astype(BF16), upper, preferred_element_type=F32)
    carry = carry_ref[:, 0:1]
    excl = cum - onehot + carry
    for k in range(TOP_K):
        rank_ref[k:k + 1, :] = jnp.sum(jnp.where(sels[k], excl, 0.0), axis=0, keepdims=True).astype(I32)
    new_carry = carry + jnp.sum(onehot, axis=1, keepdims=True)
    carry_ref[...] = jnp.broadcast_to(new_carry, carry_ref.shape)
    cnt_ref[...] = jnp.broadcast_to(new_carry, cnt_ref.shape)


def _post_mix(x2d, oa, ob, ga, gb, wo, g_post, gt, g_pre, sc, sh, wr_t, br, seq):
    t, d = x2d.shape
    tm = TM_PROJ
    spb = seq // tm
    row = lambda i: (i, 0)
    bat = lambda i: (i // spb, 0, 0)
    const = lambda i: (0, 0)
    col = lambda i: (0, i)
    return pl.pallas_call(
        _postmix_kernel,
        grid=(t // tm,),
        in_specs=[pl.BlockSpec((tm, d), row),
                  pl.BlockSpec((tm, QA_W), row),
                  pl.BlockSpec((tm, H_B * DV_B), row),
                  pl.BlockSpec((tm, d), row),
                  pl.BlockSpec((tm, d), row),
                  pl.BlockSpec((QA_W + H_B * DV_B, d), const),
                  pl.BlockSpec((1, d), const),
                  pl.BlockSpec((1, 1, d), bat),
                  pl.BlockSpec((1, d), const),
                  pl.BlockSpec((1, 1, d), bat),
                  pl.BlockSpec((1, 1, d), bat),
                  pl.BlockSpec((N_EXPERTS, d), const),
                  pl.BlockSpec((N_EXPERTS, 1), const)],
        out_specs=[pl.BlockSpec((tm, d), row),
                   pl.BlockSpec((tm * ROW_CHUNKS, LANES), row),
                   pl.BlockSpec((TOP_K, tm), col),
                   pl.BlockSpec((TOP_K, tm), col),
                   pl.BlockSpec((TOP_K, tm), col),
                   pl.BlockSpec((N_EXPERTS, LANES), const)],
        out_shape=[jax.ShapeDtypeStruct((t, d), F32),
                   jax.ShapeDtypeStruct((t * ROW_CHUNKS, LANES), F32),
                   jax.ShapeDtypeStruct((TOP_K, t), I32),
                   jax.ShapeDtypeStruct((TOP_K, t), F32),
                   jax.ShapeDtypeStruct((TOP_K, t), I32),
                   jax.ShapeDtypeStruct((N_EXPERTS, LANES), F32)],
        scratch_shapes=[pltpu.VMEM((N_EXPERTS, LANES), F32)],
        name="post_mix_router",
        compiler_params=_cparams(1),
    )(x2d, oa, ob, ga, gb, wo, g_post, gt, g_pre, sc, sh, wr_t, br)


def _row_copy(src, src_row, dst, dst_row, sem):
    return pltpu.make_async_copy(src.at[pl.ds(pl.multiple_of(src_row * ROW_CHUNKS, ROW_CHUNKS), ROW_CHUNKS)],
                                 dst.at[pl.ds(pl.multiple_of(dst_row * ROW_CHUNKS, ROW_CHUNKS), ROW_CHUNKS)], sem)


def _dispatch_kernel(pad_lo_ref, pad_hi_ref, dest_ref, h2r_ref, xs_hbm, dsm, zero_rows, sem_idx, sem, sem_pad, *, tg):
    def pad_runs(op):
        def per_expert(e, carry):
            pos = pad_lo_ref[e]
            n = pad_hi_ref[e] - pos
            for run in PAD_RUNS:
                @pl.when((n & run) != 0)
                def _(pos=pos, run=run):
                    op(pos, run)
                pos = pos + (n & run)
            return carry
        lax.fori_loop(0, N_EXPERTS, per_expert, 0)

    def zero_copy(pos, run):
        return pltpu.make_async_copy(
            zero_rows.at[pl.ds(0, run * ROW_CHUNKS)],
            xs_hbm.at[pl.ds(pl.multiple_of(pos * ROW_CHUNKS, ROW_CHUNKS), run * ROW_CHUNKS)], sem_pad)

    @pl.when(pl.program_id(0) == 0)
    def _():
        zero_rows[...] = jnp.zeros_like(zero_rows)
        pad_runs(lambda pos, run: zero_copy(pos, run).start())

    idx_copy = pltpu.make_async_copy(dest_ref.at[0, 0], dsm, sem_idx)
    idx_copy.start()
    idx_copy.wait()

    def issue(it, carry):
        for u in range(DMA_UNROLL):
            tok = it * DMA_UNROLL + u
            for k in range(TOP_K):
                _row_copy(h2r_ref, tok, xs_hbm, dsm[k * tg + tok], sem).start(priority=k % 2)
        return carry

    lax.fori_loop(0, tg // DMA_UNROLL, issue, 0)

    def drain(it, carry):
        for _ in range(DMA_UNROLL * TOP_K):
            _row_copy(h2r_ref, 0, xs_hbm, 0, sem).wait()
        return carry

    lax.fori_loop(0, tg // DMA_UNROLL, drain, 0)

    @pl.when(pl.program_id(0) == pl.num_programs(0) - 1)
    def _():
        pad_runs(lambda pos, run: zero_copy(0, run).wait())


def _dispatch(pad_lo, pad_hi, dest_blocks, h2r, n_rows, tg):
    nsteps = dest_blocks.shape[0]
    grid_spec = pltpu.PrefetchScalarGridSpec(
        num_scalar_prefetch=2,
        grid=(nsteps,),
        in_specs=[pl.BlockSpec((1, 1, TOP_K * tg), lambda i, lo, hi: (i, 0, 0)),
                  pl.BlockSpec((tg * ROW_CHUNKS, LANES), lambda i, lo, hi: (i, 0))],
        out_specs=pl.BlockSpec(memory_space=pl.ANY),
        scratch_shapes=[pltpu.SMEM((TOP_K * tg,), I32), pltpu.VMEM((PAD_RUNS[0] * ROW_CHUNKS, LANES), F32),
                        pltpu.SemaphoreType.DMA, pltpu.SemaphoreType.DMA, pltpu.SemaphoreType.DMA],
    )
    return pl.pallas_call(
        functools.partial(_dispatch_kernel, tg=tg),
        grid_spec=grid_spec,
        out_shape=jax.ShapeDtypeStruct((n_rows * ROW_CHUNKS, LANES), F32),
        name="moe_dispatch",
        compiler_params=_cparams(1),
    )(pad_lo, pad_hi, dest_blocks, h2r)


def _expert_kernel(be_ref, nu_ref, xs_ref, wgu_ref, bgu_ref, wd_ref, bd_ref, ys_ref, wgu_bf, wd_bf, *, bm):
    i = pl.program_id(0)

    @pl.when((i == 0) | (be_ref[i] != be_ref[jnp.maximum(i - 1, 0)]))
    def _():
        wgu_bf[...] = wgu_ref[0, 0].astype(BF16)
        wd_bf[...] = wd_ref[0, 0].astype(BF16)

    @pl.when(i < nu_ref[0])
    def _():
        x = jnp.concatenate([xs_ref[pl.ds(c, bm, stride=ROW_CHUNKS), :] for c in range(ROW_CHUNKS)], axis=-1)
        gu = jnp.dot(x.astype(BF16), wgu_bf[...], preferred_element_type=F32) + bgu_ref[0, 0]
        gate = jnp.minimum(gu[:, :D_FF], SWIGLU_LIMIT)
        up = jnp.clip(gu[:, D_FF:], -SWIGLU_LIMIT, SWIGLU_LIMIT)
        act = gate * _sigmoid(SWIGLU_ALPHA * gate) * (up + 1.0)
        y = jnp.dot(act.astype(BF16), wd_bf[...], preferred_element_type=F32) + bd_ref[0, 0]
        for c in range(ROW_CHUNKS):
            ys_ref[pl.ds(c, bm, stride=ROW_CHUNKS), :] = y[:, c * LANES:(c + 1) * LANES]

    @pl.when(i >= nu_ref[0])
    def _():
        ys_ref[...] = jnp.zeros_like(ys_ref)


def _experts(block_e, n_used, xs, w_gate_up, b_gate_up, w_down, b_down, layer):
    bm = BM_EXPERT
    n_blocks = xs.shape[0] // (bm * ROW_CHUNKS)
    d = D_MODEL
    wsel = lambda i, be, nu: (layer, be[i], 0, 0)
    grid_spec = pltpu.PrefetchScalarGridSpec(
        num_scalar_prefetch=2,
        grid=(n_blocks,),
        in_specs=[pl.BlockSpec((bm * ROW_CHUNKS, LANES), lambda i, be, nu: (jnp.minimum(i, nu[0] - 1), 0)),
                  pl.BlockSpec((1, 1, d, 2 * D_FF), wsel),
                  pl.BlockSpec((1, 1, 1, 2 * D_FF), wsel),
                  pl.BlockSpec((1, 1, D_FF, d), wsel),
                  pl.BlockSpec((1, 1, 1, d), wsel)],
        out_specs=pl.BlockSpec((bm * ROW_CHUNKS, LANES), lambda i, be, nu: (i, 0)),
        scratch_shapes=[pltpu.VMEM((d, 2 * D_FF), BF16), pltpu.VMEM((D_FF, d), BF16)],
    )
    return pl.pallas_call(
        functools.partial(_expert_kernel, bm=bm),
        grid_spec=grid_spec,
        out_shape=jax.ShapeDtypeStruct(xs.shape, F32),
        name="moe_experts",
        compiler_params=pltpu.CompilerParams(dimension_semantics=("arbitrary",),
                                             vmem_limit_bytes=VMEM_LIMIT_EXPERTS),
    )(block_e, n_used, xs, w_gate_up, b_gate_up, w_down, b_down)


def _combine_kernel(dest_ref, w_ref, x_ref, gt_ref, g_ref, ys_hbm, o_ref, dsm, buf, sem_idx, sem, *, tc):
    idx_copy = pltpu.make_async_copy(dest_ref.at[0, 0], dsm, sem_idx)
    idx_copy.start()
    idx_copy.wait()

    def issue(it, carry):
        for u in range(DMA_UNROLL):
            tok = it * DMA_UNROLL + u
            for k in range(TOP_K):
                _row_copy(ys_hbm, dsm[k * tc + tok], buf, k * tc + tok, sem).start(priority=k % 2)
        return carry

    lax.fori_loop(0, tc // DMA_UNROLL, issue, 0)

    def drain(it, carry):
        for _ in range(DMA_UNROLL * TOP_K):
            _row_copy(ys_hbm, 0, buf, 0, sem).wait()
        return carry

    lax.fori_loop(0, tc // DMA_UNROLL, drain, 0)

    y = jnp.zeros((tc, D_MODEL), F32)
    for k in range(TOP_K):
        yk = jnp.concatenate([buf[pl.ds(k * tc * ROW_CHUNKS + c, tc, stride=ROW_CHUNKS), :]
                              for c in range(ROW_CHUNKS)], axis=-1)
        y = y + w_ref[:, k:k + 1] * yk
    ms = jnp.mean(y * y, axis=-1, keepdims=True)
    o_ref[...] = x_ref[...] + gt_ref[0] * (y * lax.rsqrt(ms + NORM_EPS) * g_ref[...])


def _combine(dest_blocks, w_tok, x1, gt, g_post, ys, seq, tc):
    t, d = x1.shape
    spb = seq // tc
    return pl.pallas_call(
        functools.partial(_combine_kernel, tc=tc),
        grid=(t // tc,),
        in_specs=[pl.BlockSpec((1, 1, TOP_K * tc), lambda i: (i, 0, 0)),
                  pl.BlockSpec((tc, TOP_K), lambda i: (i, 0)),
                  pl.BlockSpec((tc, d), lambda i: (i, 0)),
                  pl.BlockSpec((1, 1, d), lambda i: (i // spb, 0, 0)),
                  pl.BlockSpec((1, d), lambda i: (0, 0)),
                  pl.BlockSpec(memory_space=pl.ANY)],
        out_specs=pl.BlockSpec((tc, d), lambda i: (i, 0)),
        out_shape=jax.ShapeDtypeStruct((t, d), F32),
        scratch_shapes=[pltpu.SMEM((TOP_K * tc,), I32),
                        pltpu.VMEM((TOP_K * tc * ROW_CHUNKS, LANES), F32),
                        pltpu.SemaphoreType.DMA, pltpu.SemaphoreType.DMA],
        name="moe_combine",
        compiler_params=_cparams(1),
    )(dest_blocks, w_tok, x1, gt, g_post, ys)


def _dest_blocks(dest, step_tokens):
    k, t = dest.shape
    return dest.reshape(k, t // step_tokens, step_tokens).transpose(1, 0, 2).reshape(t // step_tokens, 1,
                                                                                      k * step_tokens)


def _bias_tables(rel_bias_table, tq, tkc):
    def lookup(dist, table):
        onehot = np.eye(N_BUCKETS, dtype=np.float32)[_t5_bucket_np(dist)]
        return jnp.einsum('...n,nh->h...', onehot, table, precision=lax.Precision.HIGHEST)

    qi = np.arange(BLOCK)[:, None]
    kj = np.arange(2 * BLOCK)[None, :]
    dist = qi + BLOCK - kj
    valid = (dist >= 0) & (dist < WINDOW)
    bias_a = jnp.where(valid[None], lookup(dist, rel_bias_table[:, :H_A]), MASK_VALUE)
    kk = np.arange(tkc)[:, None]
    qq = np.arange(tq)[None, :]
    table_b = rel_bias_table[:, H_A:]
    table_b = (table_b - table_b[N_BUCKETS - 1:]) * LOG2E
    tiles = [jnp.zeros((H_B, tkc, tq), F32)]
    for rel in range(-1, tq // tkc):
        d = qq - (rel * tkc + kk)
        tiles.append(jnp.where((d >= 0)[None], lookup(d, table_b), MASK_VALUE))
    tiles.append(jnp.full((H_B, tkc, tq), MASK_VALUE, F32))
    bias_t = jnp.stack(tiles, axis=1)
    return bias_a.astype(F32), bias_t.astype(F32)


def _prep_in_weights(w_in_l):
    qa, ka, va, qb, kb, vb, gza, gzb = jnp.split(w_in_l, [int(v) for v in np.cumsum(SPLIT_SIZES)[:-1]], axis=-1)

    def dup(w):
        return jnp.concatenate([w[:, :DH_A], w[:, :DH_A], w[:, DH_A:], w[:, DH_A:]], axis=-1)

    w_nat = jnp.concatenate([qa * DH_A ** -0.5, dup(ka), dup(va), kb, gza, gzb], axis=-1).astype(BF16)
    w_t = jnp.concatenate([qb * (DH_B ** -0.5 * LOG2E), vb], axis=-1).T.astype(BF16)
    return w_nat, w_t


def kernel(x, c, rel_bias_table, w_ada, b_ada, g_mix_pre, g_mix_post, g_ffn_pre, g_ffn_post, w_in, w_out,
           attn_sinks, diff_lambda, diff_subln_g, w_router, b_router, w_gate_up, b_gate_up, w_down, b_down):
    bsz, seq, d = x.shape
    depth = w_ada.shape[0]
    t = bsz * seq
    tq, tkc = TQ_DIFF, TKC_DIFF
    assert d == D_MODEL and tkc >= BLOCK and tq % tkc == 0
    assert (TM_PROJ % tq == 0 or tq % TM_PROJ == 0) and TM_PROJ % tkc == 0
    assert seq % tq == 0 and (seq // tkc) % KC_DIFF == 0
    assert seq % TM_PROJ == 0 and seq % TM_SWA == 0
    assert seq % TG_DISPATCH == 0 and seq % TC_COMBINE == 0

    mod = _modulation(c, w_ada, b_ada)
    bias_a, bias_t = _bias_tables(rel_bias_table, tq, tkc)
    n_rows = t * TOP_K + N_EXPERTS * BM_EXPERT
    n_blocks = n_rows // BM_EXPERT

    x2d = x.reshape(t, d)
    for l in range(depth):
        sh_m, sc_m, gt_m, sh_f, sc_f, gt_f = [v[:, None, :] for v in jnp.split(mod[l], 6, axis=-1)]
        lam_init = 0.8 - 0.6 * math.exp(-0.3 * l)
        w_nat, w_t = _prep_in_weights(w_in[l])
        qa, ka2, va2, kb, ga, gb, qbt, vbt = _in_projection(
            x2d, sc_m, sh_m, g_mix_pre[l][None, :], w_nat, w_t, bsz, seq)
        o_a = _swa_attention(qa.reshape(bsz, seq, QA_W), ka2.reshape(bsz, seq, 2 * KA_W),
                             va2.reshape(bsz, seq, 2 * VA_W), attn_sinks[l], bias_a)
        g_b = jnp.broadcast_to(diff_subln_g[l][:, None], (DV_B, tq)).astype(F32)
        o_b = _diff_attention(qbt, kb.reshape(bsz, seq, KB_W), vbt, bias_t, diff_lambda[l], g_b, lam_init)
        x1, h2r, top_idx, top_w, rank, counts = _post_mix(
            x2d, o_a.reshape(t, QA_W), o_b.reshape(t, H_B * DV_B), ga, gb, w_out[l].astype(BF16),
            g_mix_post[l][None, :], gt_m, g_ffn_pre[l][None, :], sc_f, sh_f,
            w_router[l].T, b_router[l][:, None], seq)

        cnt = counts[:, 0].astype(I32)
        padded = (cnt + BM_EXPERT - 1) // BM_EXPERT * BM_EXPERT
        pends = jnp.cumsum(padded)
        pstarts = pends - padded
        experts = jnp.arange(N_EXPERTS, dtype=I32)
        dest = jnp.sum(jnp.where(top_idx[..., None] == experts, pstarts, 0), axis=-1) + rank
        block_start = jnp.arange(n_blocks, dtype=I32) * BM_EXPERT
        block_e = jnp.minimum(jnp.sum((pends[None, :] <= block_start[:, None]).astype(I32), axis=1),
                              N_EXPERTS - 1).astype(I32)
        n_used = (pends[-1:] // BM_EXPERT).astype(I32)

        xs = _dispatch((pstarts + cnt).astype(I32), pends.astype(I32), _dest_blocks(dest, TG_DISPATCH), h2r,
                       n_rows, TG_DISPATCH)
        ys = _experts(block_e, n_used, xs, w_gate_up, b_gate_up[:, :, None, :], w_down, b_down[:, :, None, :], l)
        x2d = _combine(_dest_blocks(dest, TC_COMBINE), top_w.T, x1, gt_f, g_ffn_post[l][None, :], ys,
                       seq, TC_COMBINE)
    return x2d.reshape(bsz, seq, d)
```

```python
import functools
import math

import numpy as np
import jax
import jax.numpy as jnp
from jax import lax
from jax.experimental import pallas as pl
from jax.experimental.pallas import tpu as pltpu

F32 = jnp.float32
BF16 = jnp.bfloat16
I32 = jnp.int32

D_MODEL = 1024
BLOCK = 128
WINDOW = 128
H_A, KV_A, DH_A = 8, 2, 64
G_A = H_A // KV_A
H_B, DH_B = 4, 64
DV_B = 2 * DH_B
DV_EXT = DV_B + 16
N_BUCKETS = 32
MAX_EXACT = N_BUCKETS // 2
REL_MAX_DIST = 128
N_EXPERTS = 32
TOP_K = 4
D_FF = D_MODEL
SWIGLU_ALPHA = 1.702
SWIGLU_LIMIT = 7.0
NORM_EPS = 1e-6
SUBLN_EPS = 1e-5
QA_W = H_A * DH_A
KA_W = KV_A * DH_A
VA_W = KV_A * DH_A
QB_W = H_B * 2 * DH_B
KB_W = H_B * 2 * DH_B
VB_W = H_B * 2 * DH_B
SPLIT_SIZES = (QA_W, KA_W, VA_W, QB_W, KB_W, VB_W, D_MODEL, D_MODEL)

LANES = 128
SUBLANES = 8
ROW_CHUNKS = D_MODEL // LANES
MASK_VALUE = -1e30

TM_PROJ = 512
TQ_DIFF = 512
TKC_DIFF = 256
KC_DIFF = 4
LOG2E = math.log2(math.e)
TM_SWA = 512
BM_EXPERT = 512
PAD_RUNS = tuple(BM_EXPERT >> s for s in range(1, BM_EXPERT.bit_length()))
TG_DISPATCH = 512
TC_COMBINE = 256
DMA_UNROLL = 8
MOD_COLUMN_STEPS = 4
VMEM_LIMIT = 48 * 1024 * 1024
_EXPERT_W_ELEMS = 3 * D_MODEL * D_FF
VMEM_LIMIT_EXPERTS = (2 * 4 * _EXPERT_W_ELEMS + 2 * _EXPERT_W_ELEMS
                      + 4 * 4 * BM_EXPERT * D_MODEL + 4 * BM_EXPERT * (4 * D_FF + 2 * D_MODEL))


def _cparams(n_axes):
    return pltpu.CompilerParams(dimension_semantics=("arbitrary",) * n_axes,
                                vmem_limit_bytes=VMEM_LIMIT)


def _sigmoid(x):
    return 1.0 / (1.0 + jnp.exp(-x))


def _t5_bucket_np(dist):
    n = np.maximum(dist, 0)
    nf = np.maximum(n, 1).astype(np.float32)
    large = MAX_EXACT + (np.log(nf / np.float32(MAX_EXACT)) / np.float32(math.log(REL_MAX_DIST / MAX_EXACT))
                         * np.float32(N_BUCKETS - MAX_EXACT)).astype(np.int32)
    large = np.minimum(large, N_BUCKETS - 1)
    return np.where(n < MAX_EXACT, n, large)


def _mod_kernel(c_ref, w_ref, b_ref, o_ref):
    c = c_ref[...]
    ca = c * _sigmoid(c)
    o_ref[0] = jnp.dot(ca, w_ref[0], preferred_element_type=F32,
                       precision=lax.Precision.HIGHEST) + b_ref[0]


def _modulation(c, w_ada, b_ada):
    depth, d, n = w_ada.shape
    bsz = c.shape[0]
    rows = -(-bsz // SUBLANES) * SUBLANES
    c_pad = jnp.zeros((rows, d), F32).at[:bsz].set(c)
    tn = n // MOD_COLUMN_STEPS
    out = pl.pallas_call(
        _mod_kernel,
        grid=(depth, n // tn),
        in_specs=[pl.BlockSpec((rows, d), lambda l, j: (0, 0)),
                  pl.BlockSpec((1, d, tn), lambda l, j: (l, 0, j)),
                  pl.BlockSpec((1, 1, tn), lambda l, j: (l, 0, j))],
        out_specs=pl.BlockSpec((1, rows, tn), lambda l, j: (l, 0, j)),
        out_shape=jax.ShapeDtypeStruct((depth, rows, n), F32),
        name="adaln_modulation",
        compiler_params=_cparams(2),
    )(c_pad, w_ada, b_ada.reshape(depth, 1, n))
    return out[:, :bsz]


O_QA = 0
O_KA = O_QA + QA_W
O_VA = O_KA + 2 * KA_W
O_KB = O_VA + 2 * VA_W
O_GA = O_KB + KB_W
O_GB = O_GA + D_MODEL
N_NAT = O_GB + D_MODEL


def _inproj_kernel(x_ref, sc_ref, sh_ref, g_ref, w_ref, wt_ref,
                   qa_ref, ka_ref, va_ref, kb_ref, ga_ref, gb_ref, qbt_ref, vbt_ref):
    x = x_ref[...]
    ms = jnp.mean(x * x, axis=-1, keepdims=True)
    h = x * lax.rsqrt(ms + NORM_EPS) * g_ref[...]
    h = h * (1.0 + sc_ref[0]) + sh_ref[0]
    hb = h.astype(BF16)
    tm = x.shape[0]

    def proj(a, b):
        return jnp.dot(hb, w_ref[:, a:b], preferred_element_type=F32)

    qa_ref[...] = proj(O_QA, O_KA).astype(BF16)
    ka_ref[...] = proj(O_KA, O_VA).astype(BF16)
    va_ref[...] = proj(O_VA, O_KB).astype(BF16)
    kb_ref[...] = proj(O_KB, O_GA).astype(BF16)
    ga_ref[...] = _sigmoid(proj(O_GA, O_GB)).astype(BF16)
    gb_ref[...] = _sigmoid(proj(O_GB, N_NAT)).astype(BF16)
    for t_ref, base in ((qbt_ref, 0), (vbt_ref, QB_W)):
        rt = lax.dot_general(wt_ref[base: base + H_B * DV_B, :], hb,
                             (((1,), (1,)), ((), ())), preferred_element_type=F32)
        n_sub, width = t_ref.shape[2], t_ref.shape[4]
        for hh in range(H_B):
            for j in range(n_sub):
                t_ref[0, hh, j, :DV_B, :] = rt[hh * DV_B:(hh + 1) * DV_B, j * width:(j + 1) * width].astype(BF16)
    ones_shape = vbt_ref.shape[:3] + (DV_EXT - DV_B, vbt_ref.shape[4])
    vbt_ref[:, :, :, DV_B:, :] = jnp.ones(ones_shape, BF16)


def _in_projection(x2d, sc, sh, g, w_nat, w_t, bsz, seq):
    t, d = x2d.shape
    tm, tq, tkc = TM_PROJ, TQ_DIFF, TKC_DIFF
    spb = seq // tm
    row = lambda i: (i, 0)
    bat = lambda i: (i // spb, 0, 0)
    tr = lambda i: (i // spb, 0, i % spb, 0, 0)
    if tq <= tm:
        q_spec = pl.BlockSpec((1, H_B, tm // tq, DV_B, tq), tr)
    else:
        q_spec = pl.BlockSpec((1, H_B, 1, DV_B, tm),
                              lambda i: (i // spb, 0, (i % spb) // (tq // tm), 0, (i % spb) % (tq // tm)))
    outs = pl.pallas_call(
        _inproj_kernel,
        grid=(t // tm,),
        in_specs=[pl.BlockSpec((tm, d), row),
                  pl.BlockSpec((1, 1, d), bat),
                  pl.BlockSpec((1, 1, d), bat),
                  pl.BlockSpec((1, d), lambda i: (0, 0)),
                  pl.BlockSpec((d, N_NAT), lambda i: (0, 0)),
                  pl.BlockSpec((QB_W + VB_W, d), lambda i: (0, 0))],
        out_specs=[pl.BlockSpec((tm, QA_W), row),
                   pl.BlockSpec((tm, 2 * KA_W), row),
                   pl.BlockSpec((tm, 2 * VA_W), row),
                   pl.BlockSpec((tm, KB_W), row),
                   pl.BlockSpec((tm, d), row),
                   pl.BlockSpec((tm, d), row),
                   q_spec,
                   pl.BlockSpec((1, H_B, tm // tkc, DV_EXT, tkc), tr)],
        out_shape=[jax.ShapeDtypeStruct((t, QA_W), BF16),
                   jax.ShapeDtypeStruct((t, 2 * KA_W), BF16),
                   jax.ShapeDtypeStruct((t, 2 * VA_W), BF16),
                   jax.ShapeDtypeStruct((t, KB_W), BF16),
                   jax.ShapeDtypeStruct((t, d), BF16),
                   jax.ShapeDtypeStruct((t, d), BF16),
                   jax.ShapeDtypeStruct((bsz, H_B, seq // tq, DV_B, tq), BF16),
                   jax.ShapeDtypeStruct((bsz, H_B, seq // tkc, DV_EXT, tkc), BF16)],
        name="in_projection",
        compiler_params=_cparams(1),
    )(x2d, sc, sh, g, w_nat, w_t)
    return outs


def _swa_kernel(sink_ref, q_ref, kc_ref, kp_ref, vc_ref, vp_ref, bias_ref, o_ref):
    i = pl.program_id(1)
    nsub = q_ref.shape[1] // BLOCK
    lane_k = lax.broadcasted_iota(I32, (2 * BLOCK, LANES), 1)
    lane_s = lax.broadcasted_iota(I32, (BLOCK, 2 * BLOCK), 1)
    first_mask = jnp.where((i == 0) & (lane_s < BLOCK), MASK_VALUE, 0.0).astype(F32)
    zero_b = jnp.zeros((2 * BLOCK, LANES), BF16)
    for j in range(nsub):
        rows = slice(j * BLOCK, (j + 1) * BLOCK)
        if j == 0:
            kband = jnp.concatenate([kp_ref[0], kc_ref[0, rows, :]], axis=0)
            vband = jnp.concatenate([vp_ref[0], vc_ref[0, rows, :]], axis=0)
        else:
            kband = kc_ref[0, (j - 1) * BLOCK:(j + 1) * BLOCK, :]
            vband = vc_ref[0, (j - 1) * BLOCK:(j + 1) * BLOCK, :]
        for kv in range(KV_A):
            kslab = kband[:, kv * LANES:(kv + 1) * LANES]
            vslab = vband[:, kv * LANES:(kv + 1) * LANES]
            kpad = (jnp.where(lane_k < DH_A, kslab, zero_b), jnp.where(lane_k >= DH_A, kslab, zero_b))
            vpad = (jnp.where(lane_k < DH_A, vslab, zero_b), jnp.where(lane_k >= DH_A, vslab, zero_b))
            for pp in range(G_A // 2):
                pair = kv * (G_A // 2) + pp
                qslab = q_ref[0, rows, pair * LANES:(pair + 1) * LANES]
                o_slab = jnp.zeros((BLOCK, LANES), F32)
                for par in range(2):
                    head = 2 * pair + par
                    s = lax.dot_general(qslab, kpad[par], (((1,), (1,)), ((), ())),
                                        preferred_element_type=F32)
                    s = s + bias_ref[head]
                    if j == 0:
                        s = s + first_mask
                    sink = sink_ref[head]
                    m = jnp.maximum(jnp.max(s, axis=-1, keepdims=True), sink)
                    e = jnp.exp(s - m)
                    den = jnp.sum(e, axis=-1, keepdims=True) + jnp.exp(sink - m)
                    pv = jnp.dot(e.astype(BF16), vpad[par], preferred_element_type=F32)
                    o_slab = o_slab + pv / den
                o_ref[0, rows, pair * LANES:(pair + 1) * LANES] = o_slab.astype(BF16)


def _swa_attention(qa, ka2, va2, sinks, bias_a):
    bsz, seq, _ = qa.shape
    tm = TM_SWA
    nsub = tm // BLOCK
    cur = lambda b, i: (b, i, 0)
    prev = lambda b, i: (b, jnp.maximum(i * nsub - 1, 0), 0)
    return pl.pallas_call(
        _swa_kernel,
        grid=(bsz, seq // tm),
        in_specs=[pl.BlockSpec(memory_space=pltpu.SMEM),
                  pl.BlockSpec((1, tm, QA_W), cur),
                  pl.BlockSpec((1, tm, 2 * KA_W), cur),
                  pl.BlockSpec((1, BLOCK, 2 * KA_W), prev),
                  pl.BlockSpec((1, tm, 2 * VA_W), cur),
                  pl.BlockSpec((1, BLOCK, 2 * VA_W), prev),
                  pl.BlockSpec((H_A, BLOCK, 2 * BLOCK), lambda b, i: (0, 0, 0))],
        out_specs=pl.BlockSpec((1, tm, QA_W), cur),
        out_shape=jax.ShapeDtypeStruct((bsz, seq, QA_W), BF16),
        name="swa_attention",
        compiler_params=_cparams(2),
    )(sinks, qa, ka2, ka2, va2, va2, bias_a)


def _diffattn_kernel(lp_ref, qt_ref, k_ref, vt_ref, bias_ref, g_ref, o_ref, acc_ref, *s_bufs, tq, tkc, lam_init):
    kc = len(s_bufs)
    r = tq // tkc
    nqb = qt_ref.shape[2]
    last_chunk = vt_ref.shape[2] - 1
    row = lax.broadcasted_iota(I32, (DV_B, tq), 0)
    zero_q = jnp.zeros((DV_B, tq), BF16)
    lp = lp_ref[...]
    lam = (jnp.exp(jnp.sum(lp[0:1] * lp[1:2], axis=-1, keepdims=True))
           - jnp.exp(jnp.sum(lp[2:3] * lp[3:4], axis=-1, keepdims=True)) + lam_init)

    def n_groups(i):
        return ((i + 1) * r + kc - 1) // kc

    def q_both(i):
        qt = qt_ref[0, 0, i]
        return jnp.concatenate([jnp.where(row < DH_B, qt, zero_q), jnp.where(row >= DH_B, qt, zero_q)], axis=1)

    def scores_chunk(c, i, g, qb):
        blk = g * kc + c
        src = jnp.minimum(blk, last_chunk)
        kblk = k_ref[0, pl.ds(pl.multiple_of(src * tkc, tkc), tkc), :]
        s = jnp.dot(kblk, qb, preferred_element_type=F32)
        tile = bias_ref[0, jnp.clip(blk - i * r + 2, 0, r + 2)]
        s = s + jnp.concatenate([tile, tile], axis=1)
        s_bufs[c][...] = s
        return jnp.max(s, axis=0, keepdims=True)

    def finish(i):
        acc = acc_ref[:DV_B, :]
        l_fin = acc_ref[DV_B:DV_B + 1, :]
        o = acc[:, :tq] / l_fin[:, :tq] - lam * (acc[:, tq:] / l_fin[:, tq:])
        ms = jnp.mean(o * o, axis=0, keepdims=True)
        y = o * lax.rsqrt(ms + SUBLN_EPS) * g_ref[...] * (1.0 - lam_init)
        o_ref[0, pl.ds(pl.multiple_of(i * tq, tq), tq), :] = y.T.astype(BF16)

    def step(_, carry):
        i, g, m_prev, mx = carry
        last = g == n_groups(i) - 1
        i_next = jnp.minimum(jnp.where(last, i + 1, i), nqb - 1)
        g_next = jnp.where(last, 0, g + 1)
        qb_next = q_both(i_next)
        m_old = jnp.where(g == 0, MASK_VALUE, m_prev)
        m_new = jnp.maximum(m_old, mx)
        alpha = jnp.exp2(m_old - m_new)
        mx_next = jnp.full_like(mx, MASK_VALUE)
        pv = None
        for c in range(kc):
            p = jnp.exp2((s_bufs[c][...] - m_new).astype(BF16))
            vblk = vt_ref[0, 0, jnp.minimum(g * kc + c, last_chunk)]
            d = jnp.dot(vblk, p, preferred_element_type=F32)
            pv = d if pv is None else pv + d
            mx_next = jnp.maximum(mx_next, scores_chunk(c, i_next, g_next, qb_next))
        acc_ref[...] = alpha * acc_ref[...] + pv

        @pl.when(last)
        def _():
            finish(i)

        return i_next, g_next, m_new, mx_next

    acc_ref[...] = jnp.zeros_like(acc_ref)
    qb0 = q_both(0)
    mx0 = jnp.full((1, 2 * tq), MASK_VALUE, F32)
    for c in range(kc):
        mx0 = jnp.maximum(mx0, scores_chunk(c, 0, 0, qb0))
    n_steps = sum(((i + 1) * r + kc - 1) // kc for i in range(nqb))
    lax.fori_loop(0, n_steps, step, (jnp.int32(0), jnp.int32(0), jnp.full((1, 2 * tq), MASK_VALUE, F32), mx0))


def _diff_attention(qbt, kb, vbt, bias_t, lp, g_b, lam_init):
    bsz, _, nqb, _, tq = qbt.shape
    nkb, tkc = vbt.shape[2], vbt.shape[4]
    seq = nqb * tq
    n_tiles = bias_t.shape[1]
    return pl.pallas_call(
        functools.partial(_diffattn_kernel, tq=tq, tkc=tkc, lam_init=lam_init),
        grid=(bsz, H_B),
        in_specs=[pl.BlockSpec((4, DH_B), lambda b, h: (0, 0)),
                  pl.BlockSpec((1, 1, nqb, DV_B, tq), lambda b, h: (b, h, 0, 0, 0)),
                  pl.BlockSpec((1, seq, DV_B), lambda b, h: (b, 0, h)),
                  pl.BlockSpec((1, 1, nkb, DV_EXT, tkc), lambda b, h: (b, h, 0, 0, 0)),
                  pl.BlockSpec((1, n_tiles, tkc, tq), lambda b, h: (h, 0, 0, 0)),
                  pl.BlockSpec((DV_B, tq), lambda b, h: (0, 0))],
        out_specs=pl.BlockSpec((1, seq, DV_B), lambda b, h: (b, 0, h)),
        out_shape=jax.ShapeDtypeStruct((bsz, seq, H_B * DV_B), BF16),
        scratch_shapes=[pltpu.VMEM((DV_EXT, 2 * tq), F32)] + [pltpu.VMEM((tkc, 2 * tq), F32)] * KC_DIFF,
        name="diff_attention",
        compiler_params=_cparams(2),
    )(lp, qbt, kb, vbt, bias_t, g_b)


def _postmix_kernel(x_ref, oa_ref, ob_ref, ga_ref, gb_ref, wo_ref, gpost_ref, gt_ref, gpre_ref, sc_ref, sh_ref,
                    wr_ref, br_ref,
                    x1_ref, h2r_ref, idx_ref, w_ref, rank_ref, cnt_ref, carry_ref):
    step = pl.program_id(0)
    tm = x_ref.shape[0]

    @pl.when(step == 0)
    def _():
        carry_ref[...] = jnp.zeros_like(carry_ref)

    ya = jnp.dot(oa_ref[...], wo_ref[:QA_W, :], preferred_element_type=F32)
    yb = jnp.dot(ob_ref[...], wo_ref[QA_W:, :], preferred_element_type=F32)
    y = ga_ref[...].astype(F32) * ya + gb_ref[...].astype(F32) * yb
    ms = jnp.mean(y * y, axis=-1, keepdims=True)
    x1 = x_ref[...] + gt_ref[0] * (y * lax.rsqrt(ms + NORM_EPS) * gpost_ref[...])
    x1_ref[...] = x1
    ms1 = jnp.mean(x1 * x1, axis=-1, keepdims=True)
    h2 = x1 * lax.rsqrt(ms1 + NORM_EPS) * gpre_ref[...]
    h2 = h2 * (1.0 + sc_ref[0]) + sh_ref[0]
    for c in range(ROW_CHUNKS):
        h2r_ref[pl.ds(c, tm, stride=ROW_CHUNKS), :] = h2[:, c * LANES:(c + 1) * LANES]

    logits = lax.dot_general(wr_ref[...], h2, (((1,), (1,)), ((), ())), preferred_element_type=F32,
                             precision=lax.Precision.HIGHEST) + br_ref[...]
    erow = lax.broadcasted_iota(I32, logits.shape, 0)
    vals, sels = [], []
    cur = logits
    for k in range(TOP_K):
        mv = jnp.max(cur, axis=0, keepdims=True)
        idx = jnp.min(jnp.where(cur == mv, erow, N_EXPERTS), axis=0, keepdims=True)
        sel = erow == idx
        vals.append(mv)
        sels.append(sel)
        idx_ref[k:k + 1, :] = idx
        cur = jnp.where(sel, -jnp.inf, cur)
    es = [jnp.exp(v - vals[0]) for v in vals]
    den = es[0] + es[1] + es[2] + es[3]
    for k in range(TOP_K):
        w_ref[k:k + 1, :] = es[k] / den

    onehot = jnp.zeros(logits.shape, F32)
    for k in range(TOP_K):
        onehot = onehot + jnp.where(sels[k], 1.0, 0.0)
    srow = lax.broadcasted_iota(I32, (tm, tm), 0)
    scol = lax.broadcasted_iota(I32, (tm, tm), 1)
    upper = jnp.where(srow <= scol, 1.0, 0.0).astype(BF16)
    cum = jnp.dot(onehot.astype(BF16), upper, preferred_element_type=F32)
    carry = carry_ref[:, 0:1]
    excl = cum - onehot + carry
    for k in range(TOP_K):
        rank_ref[k:k + 1, :] = jnp.sum(jnp.where(sels[k], excl, 0.0), axis=0, keepdims=True).astype(I32)
    new_carry = carry + jnp.sum(onehot, axis=1, keepdims=True)
    carry_ref[...] = jnp.broadcast_to(new_carry, carry_ref.shape)
    cnt_ref[...] = jnp.broadcast_to(new_carry, cnt_ref.shape)


def _post_mix(x2d, oa, ob, ga, gb, wo, g_post, gt, g_pre, sc, sh, wr_t, br, seq):
    t, d = x2d.shape
    tm = TM_PROJ
    spb = seq // tm
    row = lambda i: (i, 0)
    bat = lambda i: (i // spb, 0, 0)
    const = lambda i: (0, 0)
    col = lambda i: (0, i)
    return pl.pallas_call(
        _postmix_kernel,
        grid=(t // tm,),
        in_specs=[pl.BlockSpec((tm, d), row),
                  pl.BlockSpec((tm, QA_W), row),
                  pl.BlockSpec((tm, H_B * DV_B), row),
                  pl.BlockSpec((tm, d), row),
                  pl.BlockSpec((tm, d), row),
                  pl.BlockSpec((QA_W + H_B * DV_B, d), const),
                  pl.BlockSpec((1, d), const),
                  pl.BlockSpec((1, 1, d), bat),
                  pl.BlockSpec((1, d), const),
                  pl.BlockSpec((1, 1, d), bat),
                  pl.BlockSpec((1, 1, d), bat),
                  pl.BlockSpec((N_EXPERTS, d), const),
                  pl.BlockSpec((N_EXPERTS, 1), const)],
        out_specs=[pl.BlockSpec((tm, d), row),
                   pl.BlockSpec((tm * ROW_CHUNKS, LANES), row),
                   pl.BlockSpec((TOP_K, tm), col),
                   pl.BlockSpec((TOP_K, tm), col),
                   pl.BlockSpec((TOP_K, tm), col),
                   pl.BlockSpec((N_EXPERTS, LANES), const)],
        out_shape=[jax.ShapeDtypeStruct((t, d), F32),
                   jax.ShapeDtypeStruct((t * ROW_CHUNKS, LANES), F32),
                   jax.ShapeDtypeStruct((TOP_K, t), I32),
                   jax.ShapeDtypeStruct((TOP_K, t), F32),
                   jax.ShapeDtypeStruct((TOP_K, t), I32),
                   jax.ShapeDtypeStruct((N_EXPERTS, LANES), F32)],
        scratch_shapes=[pltpu.VMEM((N_EXPERTS, LANES), F32)],
        name="post_mix_router",
        compiler_params=_cparams(1),
    )(x2d, oa, ob, ga, gb, wo, g_post, gt, g_pre, sc, sh, wr_t, br)


def _row_copy(src, src_row, dst, dst_row, sem):
    return pltpu.make_async_copy(src.at[pl.ds(pl.multiple_of(src_row * ROW_CHUNKS, ROW_CHUNKS), ROW_CHUNKS)],
                                 dst.at[pl.ds(pl.multiple_of(dst_row * ROW_CHUNKS, ROW_CHUNKS), ROW_CHUNKS)], sem)


def _dispatch_kernel(pad_lo_ref, pad_hi_ref, dest_ref, h2r_ref, xs_hbm, dsm, zero_rows, sem_idx, sem, sem_pad, *, tg):
    def pad_runs(op):
        def per_expert(e, carry):
            pos = pad_lo_ref[e]
            n = pad_hi_ref[e] - pos
            for run in PAD_RUNS:
                @pl.when((n & run) != 0)
                def _(pos=pos, run=run):
                    op(pos, run)
                pos = pos + (n & run)
            return carry
        lax.fori_loop(0, N_EXPERTS, per_expert, 0)

    def zero_copy(pos, run):
        return pltpu.make_async_copy(
            zero_rows.at[pl.ds(0, run * ROW_CHUNKS)],
            xs_hbm.at[pl.ds(pl.multiple_of(pos * ROW_CHUNKS, ROW_CHUNKS), run * ROW_CHUNKS)], sem_pad)

    @pl.when(pl.program_id(0) == 0)
    def _():
        zero_rows[...] = jnp.zeros_like(zero_rows)
        pad_runs(lambda pos, run: zero_copy(pos, run).start())

    idx_copy = pltpu.make_async_copy(dest_ref.at[0, 0], dsm, sem_idx)
    idx_copy.start()
    idx_copy.wait()

    def issue(it, carry):
        for u in range(DMA_UNROLL):
            tok = it * DMA_UNROLL + u
            for k in range(TOP_K):
                _row_copy(h2r_ref, tok, xs_hbm, dsm[k * tg + tok], sem).start(priority=k % 2)
        return carry

    lax.fori_loop(0, tg // DMA_UNROLL, issue, 0)

    def drain(it, carry):
        for _ in range(DMA_UNROLL * TOP_K):
            _row_copy(h2r_ref, 0, xs_hbm, 0, sem).wait()
        return carry

    lax.fori_loop(0, tg // DMA_UNROLL, drain, 0)

    @pl.when(pl.program_id(0) == pl.num_programs(0) - 1)
    def _():
        pad_runs(lambda pos, run: zero_copy(0, run).wait())


def _dispatch(pad_lo, pad_hi, dest_blocks, h2r, n_rows, tg):
    nsteps = dest_blocks.shape[0]
    grid_spec = pltpu.PrefetchScalarGridSpec(
        num_scalar_prefetch=2,
        grid=(nsteps,),
        in_specs=[pl.BlockSpec((1, 1, TOP_K * tg), lambda i, lo, hi: (i, 0, 0)),
                  pl.BlockSpec((tg * ROW_CHUNKS, LANES), lambda i, lo, hi: (i, 0))],
        out_specs=pl.BlockSpec(memory_space=pl.ANY),
        scratch_shapes=[pltpu.SMEM((TOP_K * tg,), I32), pltpu.VMEM((PAD_RUNS[0] * ROW_CHUNKS, LANES), F32),
                        pltpu.SemaphoreType.DMA, pltpu.SemaphoreType.DMA, pltpu.SemaphoreType.DMA],
    )
    return pl.pallas_call(
        functools.partial(_dispatch_kernel, tg=tg),
        grid_spec=grid_spec,
        out_shape=jax.ShapeDtypeStruct((n_rows * ROW_CHUNKS, LANES), F32),
        name="moe_dispatch",
        compiler_params=_cparams(1),
    )(pad_lo, pad_hi, dest_blocks, h2r)


def _expert_kernel(be_ref, nu_ref, xs_ref, wgu_ref, bgu_ref, wd_ref, bd_ref, ys_ref, wgu_bf, wd_bf, *, bm):
    i = pl.program_id(0)

    @pl.when((i == 0) | (be_ref[i] != be_ref[jnp.maximum(i - 1, 0)]))
    def _():
        wgu_bf[...] = wgu_ref[0, 0].astype(BF16)
        wd_bf[...] = wd_ref[0, 0].astype(BF16)

    @pl.when(i < nu_ref[0])
    def _():
        x = jnp.concatenate([xs_ref[pl.ds(c, bm, stride=ROW_CHUNKS), :] for c in range(ROW_CHUNKS)], axis=-1)
        gu = jnp.dot(x.astype(BF16), wgu_bf[...], preferred_element_type=F32) + bgu_ref[0, 0]
        gate = jnp.minimum(gu[:, :D_FF], SWIGLU_LIMIT)
        up = jnp.clip(gu[:, D_FF:], -SWIGLU_LIMIT, SWIGLU_LIMIT)
        act = gate * _sigmoid(SWIGLU_ALPHA * gate) * (up + 1.0)
        y = jnp.dot(act.astype(BF16), wd_bf[...], preferred_element_type=F32) + bd_ref[0, 0]
        for c in range(ROW_CHUNKS):
            ys_ref[pl.ds(c, bm, stride=ROW_CHUNKS), :] = y[:, c * LANES:(c + 1) * LANES]

    @pl.when(i >= nu_ref[0])
    def _():
        ys_ref[...] = jnp.zeros_like(ys_ref)


def _experts(block_e, n_used, xs, w_gate_up, b_gate_up, w_down, b_down, layer):
    bm = BM_EXPERT
    n_blocks = xs.shape[0] // (bm * ROW_CHUNKS)
    d = D_MODEL
    wsel = lambda i, be, nu: (layer, be[i], 0, 0)
    grid_spec = pltpu.PrefetchScalarGridSpec(
        num_scalar_prefetch=2,
        grid=(n_blocks,),
        in_specs=[pl.BlockSpec((bm * ROW_CHUNKS, LANES), lambda i, be, nu: (jnp.minimum(i, nu[0] - 1), 0)),
                  pl.BlockSpec((1, 1, d, 2 * D_FF), wsel),
                  pl.BlockSpec((1, 1, 1, 2 * D_FF), wsel),
                  pl.BlockSpec((1, 1, D_FF, d), wsel),
                  pl.BlockSpec((1, 1, 1, d), wsel)],
        out_specs=pl.BlockSpec((bm * ROW_CHUNKS, LANES), lambda i, be, nu: (i, 0)),
        scratch_shapes=[pltpu.VMEM((d, 2 * D_FF), BF16), pltpu.VMEM((D_FF, d), BF16)],
    )
    return pl.pallas_call(
        functools.partial(_expert_kernel, bm=bm),
        grid_spec=grid_spec,
        out_shape=jax.ShapeDtypeStruct(xs.shape, F32),
        name="moe_experts",
        compiler_params=pltpu.CompilerParams(dimension_semantics=("arbitrary",),
                                             vmem_limit_bytes=VMEM_LIMIT_EXPERTS),
    )(block_e, n_used, xs, w_gate_up, b_gate_up, w_down, b_down)


def _combine_kernel(dest_ref, w_ref, x_ref, gt_ref, g_ref, ys_hbm, o_ref, dsm, buf, sem_idx, sem, *, tc):
    idx_copy = pltpu.make_async_copy(dest_ref.at[0, 0], dsm, sem_idx)
    idx_copy.start()
    idx_copy.wait()

    def issue(it, carry):
        for u in range(DMA_UNROLL):
            tok = it * DMA_UNROLL + u
            for k in range(TOP_K):
                _row_copy(ys_hbm, dsm[k * tc + tok], buf, k * tc + tok, sem).start(priority=k % 2)
        return carry

    lax.fori_loop(0, tc // DMA_UNROLL, issue, 0)

    def drain(it, carry):
        for _ in range(DMA_UNROLL * TOP_K):
            _row_copy(ys_hbm, 0, buf, 0, sem).wait()
        return carry

    lax.fori_loop(0, tc // DMA_UNROLL, drain, 0)

    y = jnp.zeros((tc, D_MODEL), F32)
    for k in range(TOP_K):
        yk = jnp.concatenate([buf[pl.ds(k * tc * ROW_CHUNKS + c, tc, stride=ROW_CHUNKS), :]
                              for c in range(ROW_CHUNKS)], axis=-1)
        y = y + w_ref[:, k:k + 1] * yk
    ms = jnp.mean(y * y, axis=-1, keepdims=True)
    o_ref[...] = x_ref[...] + gt_ref[0] * (y * lax.rsqrt(ms + NORM_EPS) * g_ref[...])


def _combine(dest_blocks, w_tok, x1, gt, g_post, ys, seq, tc):
    t, d = x1.shape
    spb = seq // tc
    return pl.pallas_call(
        functools.partial(_combine_kernel, tc=tc),
        grid=(t // tc,),
        in_specs=[pl.BlockSpec((1, 1, TOP_K * tc), lambda i: (i, 0, 0)),
                  pl.BlockSpec((tc, TOP_K), lambda i: (i, 0)),
                  pl.BlockSpec((tc, d), lambda i: (i, 0)),
                  pl.BlockSpec((1, 1, d), lambda i: (i // spb, 0, 0)),
                  pl.BlockSpec((1, d), lambda i: (0, 0)),
                  pl.BlockSpec(memory_space=pl.ANY)],
        out_specs=pl.BlockSpec((tc, d), lambda i: (i, 0)),
        out_shape=jax.ShapeDtypeStruct((t, d), F32),
        scratch_shapes=[pltpu.SMEM((TOP_K * tc,), I32),
                        pltpu.VMEM((TOP_K * tc * ROW_CHUNKS, LANES), F32),
                        pltpu.SemaphoreType.DMA, pltpu.SemaphoreType.DMA],
        name="moe_combine",
        compiler_params=_cparams(1),
    )(dest_blocks, w_tok, x1, gt, g_post, ys)


def _dest_blocks(dest, step_tokens):
    k, t = dest.shape
    return dest.reshape(k, t // step_tokens, step_tokens).transpose(1, 0, 2).reshape(t // step_tokens, 1,
                                                                                      k * step_tokens)


def _bias_tables(rel_bias_table, tq, tkc):
    def lookup(dist, table):
        onehot = np.eye(N_BUCKETS, dtype=np.float32)[_t5_bucket_np(dist)]
        return jnp.einsum('...n,nh->h...', onehot, table, precision=lax.Precision.HIGHEST)

    qi = np.arange(BLOCK)[:, None]
    kj = np.arange(2 * BLOCK)[None, :]
    dist = qi + BLOCK - kj
    valid = (dist >= 0) & (dist < WINDOW)
    bias_a = jnp.where(valid[None], lookup(dist, rel_bias_table[:, :H_A]), MASK_VALUE)
    kk = np.arange(tkc)[:, None]
    qq = np.arange(tq)[None, :]
    table_b = rel_bias_table[:, H_A:]
    table_b = (table_b - table_b[N_BUCKETS - 1:]) * LOG2E
    tiles = [jnp.zeros((H_B, tkc, tq), F32)]
    for rel in range(-1, tq // tkc):
        d = qq - (rel * tkc + kk)
        tiles.append(jnp.where((d >= 0)[None], lookup(d, table_b), MASK_VALUE))
    tiles.append(jnp.full((H_B, tkc, tq), MASK_VALUE, F32))
    bias_t = jnp.stack(tiles, axis=1)
    return bias_a.astype(F32), bias_t.astype(F32)


def _prep_in_weights(w_in_l):
    qa, ka, va, qb, kb, vb, gza, gzb = jnp.split(w_in_l, [int(v) for v in np.cumsum(SPLIT_SIZES)[:-1]], axis=-1)

    def dup(w):
        return jnp.concatenate([w[:, :DH_A], w[:, :DH_A], w[:, DH_A:], w[:, DH_A:]], axis=-1)

    w_nat = jnp.concatenate([qa * DH_A ** -0.5, dup(ka), dup(va), kb, gza, gzb], axis=-1).astype(BF16)
    w_t = jnp.concatenate([qb * (DH_B ** -0.5 * LOG2E), vb], axis=-1).T.astype(BF16)
    return w_nat, w_t


def kernel(x, c, rel_bias_table, w_ada, b_ada, g_mix_pre, g_mix_post, g_ffn_pre, g_ffn_post, w_in, w_out,
           attn_sinks, diff_lambda, diff_subln_g, w_router, b_router, w_gate_up, b_gate_up, w_down, b_down):
    bsz, seq, d = x.shape
    depth = w_ada.shape[0]
    t = bsz * seq
    tq, tkc = TQ_DIFF, TKC_DIFF
    assert d == D_MODEL and tkc >= BLOCK and tq % tkc == 0
    assert (TM_PROJ % tq == 0 or tq % TM_PROJ == 0) and TM_PROJ % tkc == 0
    assert seq % tq == 0 and (seq // tkc) % KC_DIFF == 0
    assert seq % TM_PROJ == 0 and seq % TM_SWA == 0
    assert seq % TG_DISPATCH == 0 and seq % TC_COMBINE == 0

    mod = _modulation(c, w_ada, b_ada)
    bias_a, bias_t = _bias_tables(rel_bias_table, tq, tkc)
    n_rows = t * TOP_K + N_EXPERTS * BM_EXPERT
    n_blocks = n_rows // BM_EXPERT

    x2d = x.reshape(t, d)
    for l in range(depth):
        sh_m, sc_m, gt_m, sh_f, sc_f, gt_f = [v[:, None, :] for v in jnp.split(mod[l], 6, axis=-1)]
        lam_init = 0.8 - 0.6 * math.exp(-0.3 * l)
        w_nat, w_t = _prep_in_weights(w_in[l])
        qa, ka2, va2, kb, ga, gb, qbt, vbt = _in_projection(
            x2d, sc_m, sh_m, g_mix_pre[l][None, :], w_nat, w_t, bsz, seq)
        o_a = _swa_attention(qa.reshape(bsz, seq, QA_W), ka2.reshape(bsz, seq, 2 * KA_W),
                             va2.reshape(bsz, seq, 2 * VA_W), attn_sinks[l], bias_a)
        g_b = jnp.broadcast_to(diff_subln_g[l][:, None], (DV_B, tq)).astype(F32)
        o_b = _diff_attention(qbt, kb.reshape(bsz, seq, KB_W), vbt, bias_t, diff_lambda[l], g_b, lam_init)
        x1, h2r, top_idx, top_w, rank, counts = _post_mix(
            x2d, o_a.reshape(t, QA_W), o_b.reshape(t, H_B * DV_B), ga, gb, w_out[l].astype(BF16),
            g_mix_post[l][None, :], gt_m, g_ffn_pre[l][None, :], sc_f, sh_f,
            w_router[l].T, b_router[l][:, None], seq)

        cnt = counts[:, 0].astype(I32)
        padded = (cnt + BM_EXPERT - 1) // BM_EXPERT * BM_EXPERT
        pends = jnp.cumsum(padded)
        pstarts = pends - padded
        experts = jnp.arange(N_EXPERTS, dtype=I32)
        dest = jnp.sum(jnp.where(top_idx[..., None] == experts, pstarts, 0), axis=-1) + rank
        block_start = jnp.arange(n_blocks, dtype=I32) * BM_EXPERT
        block_e = jnp.minimum(jnp.sum((pends[None, :] <= block_start[:, None]).astype(I32), axis=1),
                              N_EXPERTS - 1).astype(I32)
        n_used = (pends[-1:] // BM_EXPERT).astype(I32)

        xs = _dispatch((pstarts + cnt).astype(I32), pends.astype(I32), _dest_blocks(dest, TG_DISPATCH), h2r,
                       n_rows, TG_DISPATCH)
        ys = _experts(block_e, n_used, xs, w_gate_up, b_gate_up[:, :, None, :], w_down, b_down[:, :, None, :], l)
        x2d = _combine(_dest_blocks(dest, TC_COMBINE), top_w.T, x1, gt_f, g_ffn_post[l][None, :], ys,
                       seq, TC_COMBINE)
    return x2d.reshape(bsz, seq, d)
```

```python
import functools
import math

import numpy as np
import jax
import jax.numpy as jnp
from jax import lax
from jax.experimental import pallas as pl
from jax.experimental.pallas import tpu as pltpu

F32 = jnp.float32
BF16 = jnp.bfloat16
I32 = jnp.int32

D_MODEL = 1024
BLOCK = 128
WINDOW = 128
H_A, KV_A, DH_A = 8, 2, 64
G_A = H_A // KV_A
H_B, DH_B = 4, 64
DV_B = 2 * DH_B
DV_EXT = DV_B + 16
N_BUCKETS = 32
MAX_EXACT = N_BUCKETS // 2
REL_MAX_DIST = 128
N_EXPERTS = 32
TOP_K = 4
D_FF = D_MODEL
SWIGLU_ALPHA = 1.702
SWIGLU_LIMIT = 7.0
NORM_EPS = 1e-6
SUBLN_EPS = 1e-5
QA_W = H_A * DH_A
KA_W = KV_A * DH_A
VA_W = KV_A * DH_A
QB_W = H_B * 2 * DH_B
KB_W = H_B * 2 * DH_B
VB_W = H_B * 2 * DH_B
SPLIT_SIZES = (QA_W, KA_W, VA_W, QB_W, KB_W, VB_W, D_MODEL, D_MODEL)

LANES = 128
SUBLANES = 8
ROW_CHUNKS = D_MODEL // LANES
MASK_VALUE = -1e30

TM_PROJ = 512
TQ_DIFF = 512
TKC_DIFF = 256
KC_DIFF = 4
LOG2E = math.log2(math.e)
TM_SWA = 512
BM_EXPERT = 512
PAD_RUNS = tuple(BM_EXPERT >> s for s in range(1, BM_EXPERT.bit_length()))
TG_DISPATCH = 2048
TC_COMBINE = 1024
DMA_UNROLL = 8
MOD_COLUMN_STEPS = 4
VMEM_LIMIT = 48 * 1024 * 1024
_EXPERT_W_ELEMS = 3 * D_MODEL * D_FF
VMEM_LIMIT_EXPERTS = (2 * 4 * _EXPERT_W_ELEMS + 2 * _EXPERT_W_ELEMS
                      + 4 * 4 * BM_EXPERT * D_MODEL + 4 * BM_EXPERT * (4 * D_FF + 2 * D_MODEL))


def _cparams(n_axes):
    return pltpu.CompilerParams(dimension_semantics=("arbitrary",) * n_axes,
                                vmem_limit_bytes=VMEM_LIMIT)


def _sigmoid(x):
    return 1.0 / (1.0 + jnp.exp(-x))


def _t5_bucket_np(dist):
    n = np.maximum(dist, 0)
    nf = np.maximum(n, 1).astype(np.float32)
    large = MAX_EXACT + (np.log(nf / np.float32(MAX_EXACT)) / np.float32(math.log(REL_MAX_DIST / MAX_EXACT))
                         * np.float32(N_BUCKETS - MAX_EXACT)).astype(np.int32)
    large = np.minimum(large, N_BUCKETS - 1)
    return np.where(n < MAX_EXACT, n, large)


def _mod_kernel(c_ref, w_ref, b_ref, o_ref):
    c = c_ref[...]
    ca = c * _sigmoid(c)
    o_ref[0] = jnp.dot(ca, w_ref[0], preferred_element_type=F32,
                       precision=lax.Precision.HIGHEST) + b_ref[0]


def _modulation(c, w_ada, b_ada):
    depth, d, n = w_ada.shape
    bsz = c.shape[0]
    rows = -(-bsz // SUBLANES) * SUBLANES
    c_pad = jnp.zeros((rows, d), F32).at[:bsz].set(c)
    tn = n // MOD_COLUMN_STEPS
    out = pl.pallas_call(
        _mod_kernel,
        grid=(depth, n // tn),
        in_specs=[pl.BlockSpec((rows, d), lambda l, j: (0, 0)),
                  pl.BlockSpec((1, d, tn), lambda l, j: (l, 0, j)),
                  pl.BlockSpec((1, 1, tn), lambda l, j: (l, 0, j))],
        out_specs=pl.BlockSpec((1, rows, tn), lambda l, j: (l, 0, j)),
        out_shape=jax.ShapeDtypeStruct((depth, rows, n), F32),
        name="adaln_modulation",
        compiler_params=_cparams(2),
    )(c_pad, w_ada, b_ada.reshape(depth, 1, n))
    return out[:, :bsz]


O_QA = 0
O_KA = O_QA + QA_W
O_VA = O_KA + 2 * KA_W
O_KB = O_VA + 2 * VA_W
O_GA = O_KB + KB_W
O_GB = O_GA + D_MODEL
N_NAT = O_GB + D_MODEL


def _inproj_kernel(x_ref, sc_ref, sh_ref, g_ref, w_ref, wt_ref,
                   qa_ref, ka_ref, va_ref, kb_ref, ga_ref, gb_ref, qbt_ref, vbt_ref):
    x = x_ref[...]
    ms = jnp.mean(x * x, axis=-1, keepdims=True)
    h = x * lax.rsqrt(ms + NORM_EPS) * g_ref[...]
    h = h * (1.0 + sc_ref[0]) + sh_ref[0]
    hb = h.astype(BF16)
    tm = x.shape[0]

    def proj(a, b):
        return jnp.dot(hb, w_ref[:, a:b], preferred_element_type=F32)

    qa_ref[...] = proj(O_QA, O_KA).astype(BF16)
    ka_ref[...] = proj(O_KA, O_VA).astype(BF16)
    va_ref[...] = proj(O_VA, O_KB).astype(BF16)
    kb_ref[...] = proj(O_KB, O_GA).astype(BF16)
    ga_ref[...] = _sigmoid(proj(O_GA, O_GB)).astype(BF16)
    gb_ref[...] = _sigmoid(proj(O_GB, N_NAT)).astype(BF16)
    for t_ref, base in ((qbt_ref, 0), (vbt_ref, QB_W)):
        rt = lax.dot_general(wt_ref[base: base + H_B * DV_B, :], hb,
                             (((1,), (1,)), ((), ())), preferred_element_type=F32)
        n_sub, width = t_ref.shape[2], t_ref.shape[4]
        for hh in range(H_B):
            for j in range(n_sub):
                t_ref[0, hh, j, :DV_B, :] = rt[hh * DV_B:(hh + 1) * DV_B, j * width:(j + 1) * width].astype(BF16)
    ones_shape = vbt_ref.shape[:3] + (DV_EXT - DV_B, vbt_ref.shape[4])
    vbt_ref[:, :, :, DV_B:, :] = jnp.ones(ones_shape, BF16)


def _in_projection(x2d, sc, sh, g, w_nat, w_t, bsz, seq):
    t, d = x2d.shape
    tm, tq, tkc = TM_PROJ, TQ_DIFF, TKC_DIFF
    spb = seq // tm
    row = lambda i: (i, 0)
    bat = lambda i: (i // spb, 0, 0)
    tr = lambda i: (i // spb, 0, i % spb, 0, 0)
    if tq <= tm:
        q_spec = pl.BlockSpec((1, H_B, tm // tq, DV_B, tq), tr)
    else:
        q_spec = pl.BlockSpec((1, H_B, 1, DV_B, tm),
                              lambda i: (i // spb, 0, (i % spb) // (tq // tm), 0, (i % spb) % (tq // tm)))
    outs = pl.pallas_call(
        _inproj_kernel,
        grid=(t // tm,),
        in_specs=[pl.BlockSpec((tm, d), row),
                  pl.BlockSpec((1, 1, d), bat),
                  pl.BlockSpec((1, 1, d), bat),
                  pl.BlockSpec((1, d), lambda i: (0, 0)),
                  pl.BlockSpec((d, N_NAT), lambda i: (0, 0)),
                  pl.BlockSpec((QB_W + VB_W, d), lambda i: (0, 0))],
        out_specs=[pl.BlockSpec((tm, QA_W), row),
                   pl.BlockSpec((tm, 2 * KA_W), row),
                   pl.BlockSpec((tm, 2 * VA_W), row),
                   pl.BlockSpec((tm, KB_W), row),
                   pl.BlockSpec((tm, d), row),
                   pl.BlockSpec((tm, d), row),
                   q_spec,
                   pl.BlockSpec((1, H_B, tm // tkc, DV_EXT, tkc), tr)],
        out_shape=[jax.ShapeDtypeStruct((t, QA_W), BF16),
                   jax.ShapeDtypeStruct((t, 2 * KA_W), BF16),
                   jax.ShapeDtypeStruct((t, 2 * VA_W), BF16),
                   jax.ShapeDtypeStruct((t, KB_W), BF16),
                   jax.ShapeDtypeStruct((t, d), BF16),
                   jax.ShapeDtypeStruct((t, d), BF16),
                   jax.ShapeDtypeStruct((bsz, H_B, seq // tq, DV_B, tq), BF16),
                   jax.ShapeDtypeStruct((bsz, H_B, seq // tkc, DV_EXT, tkc), BF16)],
        name="in_projection",
        compiler_params=_cparams(1),
    )(x2d, sc, sh, g, w_nat, w_t)
    return outs


def _swa_kernel(sink_ref, q_ref, kc_ref, kp_ref, vc_ref, vp_ref, bias_ref, o_ref):
    i = pl.program_id(1)
    nsub = q_ref.shape[1] // BLOCK
    lane_k = lax.broadcasted_iota(I32, (2 * BLOCK, LANES), 1)
    lane_s = lax.broadcasted_iota(I32, (BLOCK, 2 * BLOCK), 1)
    first_mask = jnp.where((i == 0) & (lane_s < BLOCK), MASK_VALUE, 0.0).astype(F32)
    zero_b = jnp.zeros((2 * BLOCK, LANES), BF16)
    for j in range(nsub):
        rows = slice(j * BLOCK, (j + 1) * BLOCK)
        if j == 0:
            kband = jnp.concatenate([kp_ref[0], kc_ref[0, rows, :]], axis=0)
            vband = jnp.concatenate([vp_ref[0], vc_ref[0, rows, :]], axis=0)
        else:
            kband = kc_ref[0, (j - 1) * BLOCK:(j + 1) * BLOCK, :]
            vband = vc_ref[0, (j - 1) * BLOCK:(j + 1) * BLOCK, :]
        for kv in range(KV_A):
            kslab = kband[:, kv * LANES:(kv + 1) * LANES]
            vslab = vband[:, kv * LANES:(kv + 1) * LANES]
            kpad = (jnp.where(lane_k < DH_A, kslab, zero_b), jnp.where(lane_k >= DH_A, kslab, zero_b))
            vpad = (jnp.where(lane_k < DH_A, vslab, zero_b), jnp.where(lane_k >= DH_A, vslab, zero_b))
            for pp in range(G_A // 2):
                pair = kv * (G_A // 2) + pp
                qslab = q_ref[0, rows, pair * LANES:(pair + 1) * LANES]
                o_slab = jnp.zeros((BLOCK, LANES), F32)
                for par in range(2):
                    head = 2 * pair + par
                    s = lax.dot_general(qslab, kpad[par], (((1,), (1,)), ((), ())),
                                        preferred_element_type=F32)
                    s = s + bias_ref[head]
                    if j == 0:
                        s = s + first_mask
                    sink = sink_ref[head]
                    m = jnp.maximum(jnp.max(s, axis=-1, keepdims=True), sink)
                    e = jnp.exp(s - m)
                    den = jnp.sum(e, axis=-1, keepdims=True) + jnp.exp(sink - m)
                    pv = jnp.dot(e.astype(BF16), vpad[par], preferred_element_type=F32)
                    o_slab = o_slab + pv / den
                o_ref[0, rows, pair * LANES:(pair + 1) * LANES] = o_slab.astype(BF16)


def _swa_attention(qa, ka2, va2, sinks, bias_a):
    bsz, seq, _ = qa.shape
    tm = TM_SWA
    nsub = tm // BLOCK
    cur = lambda b, i: (b, i, 0)
    prev = lambda b, i: (b, jnp.maximum(i * nsub - 1, 0), 0)
    return pl.pallas_call(
        _swa_kernel,
        grid=(bsz, seq // tm),
        in_specs=[pl.BlockSpec(memory_space=pltpu.SMEM),
                  pl.BlockSpec((1, tm, QA_W), cur),
                  pl.BlockSpec((1, tm, 2 * KA_W), cur),
                  pl.BlockSpec((1, BLOCK, 2 * KA_W), prev),
                  pl.BlockSpec((1, tm, 2 * VA_W), cur),
                  pl.BlockSpec((1, BLOCK, 2 * VA_W), prev),
                  pl.BlockSpec((H_A, BLOCK, 2 * BLOCK), lambda b, i: (0, 0, 0))],
        out_specs=pl.BlockSpec((1, tm, QA_W), cur),
        out_shape=jax.ShapeDtypeStruct((bsz, seq, QA_W), BF16),
        name="swa_attention",
        compiler_params=_cparams(2),
    )(sinks, qa, ka2, ka2, va2, va2, bias_a)


def _diffattn_kernel(lp_ref, qt_ref, k_ref, vt_ref, bias_ref, g_ref, o_ref, acc_ref, *s_bufs, tq, tkc, lam_init):
    kc = len(s_bufs)
    r = tq // tkc
    nqb = qt_ref.shape[2]
    last_chunk = vt_ref.shape[2] - 1
    row = lax.broadcasted_iota(I32, (DV_B, tq), 0)
    zero_q = jnp.zeros((DV_B, tq), BF16)
    lp = lp_ref[...]
    lam = (jnp.exp(jnp.sum(lp[0:1] * lp[1:2], axis=-1, keepdims=True))
           - jnp.exp(jnp.sum(lp[2:3] * lp[3:4], axis=-1, keepdims=True)) + lam_init)

    def n_groups(i):
        return ((i + 1) * r + kc - 1) // kc

    def q_both(i):
        qt = qt_ref[0, 0, i]
        return jnp.concatenate([jnp.where(row < DH_B, qt, zero_q), jnp.where(row >= DH_B, qt, zero_q)], axis=1)

    def scores_chunk(c, i, g, qb):
        blk = g * kc + c
        src = jnp.minimum(blk, last_chunk)
        kblk = k_ref[0, pl.ds(pl.multiple_of(src * tkc, tkc), tkc), :]
        s = jnp.dot(kblk, qb, preferred_element_type=F32)
        tile = bias_ref[0, jnp.clip(blk - i * r + 2, 0, r + 2)]
        s = s + jnp.concatenate([tile, tile], axis=1)
        s_bufs[c][...] = s
        return jnp.max(s, axis=0, keepdims=True)

    def finish(i):
        acc = acc_ref[:DV_B, :]
        l_fin = acc_ref[DV_B:DV_B + 1, :]
        o = acc[:, :tq] / l_fin[:, :tq] - lam * (acc[:, tq:] / l_fin[:, tq:])
        ms = jnp.mean(o * o, axis=0, keepdims=True)
        y = o * lax.rsqrt(ms + SUBLN_EPS) * g_ref[...] * (1.0 - lam_init)
        o_ref[0, pl.ds(pl.multiple_of(i * tq, tq), tq), :] = y.T.astype(BF16)

    def step(_, carry):
        i, g, m_prev, mx = carry
        last = g == n_groups(i) - 1
        i_next = jnp.minimum(jnp.where(last, i + 1, i), nqb - 1)
        g_next = jnp.where(last, 0, g + 1)
        qb_next = q_both(i_next)
        m_old = jnp.where(g == 0, MASK_VALUE, m_prev)
        m_new = jnp.maximum(m_old, mx)
        alpha = jnp.exp2(m_old - m_new)
        mx_next = jnp.full_like(mx, MASK_VALUE)
        pv = None
        for c in range(kc):
            p = jnp.exp2((s_bufs[c][...] - m_new).astype(BF16))
            vblk = vt_ref[0, 0, jnp.minimum(g * kc + c, last_chunk)]
            d = jnp.dot(vblk, p, preferred_element_type=F32)
            pv = d if pv is None else pv + d
            mx_next = jnp.maximum(mx_next, scores_chunk(c, i_next, g_next, qb_next))
        acc_ref[...] = alpha * acc_ref[...] + pv

        @pl.when(last)
        def _():
            finish(i)

        return i_next, g_next, m_new, mx_next

    acc_ref[...] = jnp.zeros_like(acc_ref)
    qb0 = q_both(0)
    mx0 = jnp.full((1, 2 * tq), MASK_VALUE, F32)
    for c in range(kc):
        mx0 = jnp.maximum(mx0, scores_chunk(c, 0, 0, qb0))
    n_steps = sum(((i + 1) * r + kc - 1) // kc for i in range(nqb))
    lax.fori_loop(0, n_steps, step, (jnp.int32(0), jnp.int32(0), jnp.full((1, 2 * tq), MASK_VALUE, F32), mx0))


def _diff_attention(qbt, kb, vbt, bias_t, lp, g_b, lam_init):
    bsz, _, nqb, _, tq = qbt.shape
    nkb, tkc = vbt.shape[2], vbt.shape[4]
    seq = nqb * tq
    n_tiles = bias_t.shape[1]
    return pl.pallas_call(
        functools.partial(_diffattn_kernel, tq=tq, tkc=tkc, lam_init=lam_init),
        grid=(bsz, H_B),
        in_specs=[pl.BlockSpec((4, DH_B), lambda b, h: (0, 0)),
                  pl.BlockSpec((1, 1, nqb, DV_B, tq), lambda b, h: (b, h, 0, 0, 0)),
                  pl.BlockSpec((1, seq, DV_B), lambda b, h: (b, 0, h)),
                  pl.BlockSpec((1, 1, nkb, DV_EXT, tkc), lambda b, h: (b, h, 0, 0, 0)),
                  pl.BlockSpec((1, n_tiles, tkc, tq), lambda b, h: (h, 0, 0, 0)),
                  pl.BlockSpec((DV_B, tq), lambda b, h: (0, 0))],
        out_specs=pl.BlockSpec((1, seq, DV_B), lambda b, h: (b, 0, h)),
        out_shape=jax.ShapeDtypeStruct((bsz, seq, H_B * DV_B), BF16),
        scratch_shapes=[pltpu.VMEM((DV_EXT, 2 * tq), F32)] + [pltpu.VMEM((tkc, 2 * tq), F32)] * KC_DIFF,
        name="diff_attention",
        compiler_params=_cparams(2),
    )(lp, qbt, kb, vbt, bias_t, g_b)


def _postmix_kernel(x_ref, oa_ref, ob_ref, ga_ref, gb_ref, wo_ref, gpost_ref, gt_ref, gpre_ref, sc_ref, sh_ref,
                    wr_ref, br_ref,
                    x1_ref, h2r_ref, idx_ref, w_ref, rank_ref, cnt_ref, carry_ref):
    step = pl.program_id(0)
    tm = x_ref.shape[0]

    @pl.when(step == 0)
    def _():
        carry_ref[...] = jnp.zeros_like(carry_ref)

    ya = jnp.dot(oa_ref[...], wo_ref[:QA_W, :], preferred_element_type=F32)
    yb = jnp.dot(ob_ref[...], wo_ref[QA_W:, :], preferred_element_type=F32)
    y = ga_ref[...].astype(F32) * ya + gb_ref[...].astype(F32) * yb
    ms = jnp.mean(y * y, axis=-1, keepdims=True)
    x1 = x_ref[...] + gt_ref[0] * (y * lax.rsqrt(ms + NORM_EPS) * gpost_ref[...])
    x1_ref[...] = x1
    ms1 = jnp.mean(x1 * x1, axis=-1, keepdims=True)
    h2 = x1 * lax.rsqrt(ms1 + NORM_EPS) * gpre_ref[...]
    h2 = h2 * (1.0 + sc_ref[0]) + sh_ref[0]
    for c in range(ROW_CHUNKS):
        h2r_ref[pl.ds(c, tm, stride=ROW_CHUNKS), :] = h2[:, c * LANES:(c + 1) * LANES]

    logits = lax.dot_general(wr_ref[...], h2, (((1,), (1,)), ((), ())), preferred_element_type=F32,
                             precision=lax.Precision.HIGHEST) + br_ref[...]
    erow = lax.broadcasted_iota(I32, logits.shape, 0)
    vals, sels = [], []
    cur = logits
    for k in range(TOP_K):
        mv = jnp.max(cur, axis=0, keepdims=True)
        idx = jnp.min(jnp.where(cur == mv, erow, N_EXPERTS), axis=0, keepdims=True)
        sel = erow == idx
        vals.append(mv)
        sels.append(sel)
        idx_ref[k:k + 1, :] = idx
        cur = jnp.where(sel, -jnp.inf, cur)
    es = [jnp.exp(v - vals[0]) for v in vals]
    den = es[0] + es[1] + es[2] + es[3]
    for k in range(TOP_K):
        w_ref[k:k + 1, :] = es[k] / den

    onehot = jnp.zeros(logits.shape, F32)
    for k in range(TOP_K):
        onehot = onehot + jnp.where(sels[k], 1.0, 0.0)
    srow = lax.broadcasted_iota(I32, (tm, tm), 0)
    scol = lax.broadcasted_iota(I32, (tm, tm), 1)
    upper = jnp.where(srow <= scol, 1.0, 0.0).astype(BF16)
    cum = jnp.dot(onehot.astype(BF16), upper, preferred_element_type=F32)
    carry = carry_ref[:, 0:1]
    excl = cum - onehot + carry
    for k in range(TOP_K):
        rank_ref[k:k + 1, :] = jnp.sum(jnp.where(sels[k], excl, 0.0), axis=0, keepdims=True).astype(I32)
    new_carry = carry + jnp.sum(onehot, axis=1, keepdims=True)
    carry_ref[...] = jnp.broadcast_to(new_carry, carry_ref.shape)
    cnt_ref[...] = jnp.broadcast_to(new_carry, cnt_ref.shape)


def _post_mix(x2d, oa, ob, ga, gb, wo, g_post, gt, g_pre, sc, sh, wr_t, br, seq):
    t, d = x2d.shape
    tm = TM_PROJ
    spb = seq // tm
    row = lambda i: (i, 0)
    bat = lambda i: (i // spb, 0, 0)
    const = lambda i: (0, 0)
    col = lambda i: (0, i)
    return pl.pallas_call(
        _postmix_kernel,
        grid=(t // tm,),
        in_specs=[pl.BlockSpec((tm, d), row),
                  pl.BlockSpec((tm, QA_W), row),
                  pl.BlockSpec((tm, H_B * DV_B), row),
                  pl.BlockSpec((tm, d), row),
                  pl.BlockSpec((tm, d), row),
                  pl.BlockSpec((QA_W + H_B * DV_B, d), const),
                  pl.BlockSpec((1, d), const),
                  pl.BlockSpec((1, 1, d), bat),
                  pl.BlockSpec((1, d), const),
                  pl.BlockSpec((1, 1, d), bat),
                  pl.BlockSpec((1, 1, d), bat),
                  pl.BlockSpec((N_EXPERTS, d), const),
                  pl.BlockSpec((N_EXPERTS, 1), const)],
        out_specs=[pl.BlockSpec((tm, d), row),
                   pl.BlockSpec((tm * ROW_CHUNKS, LANES), row),
                   pl.BlockSpec((TOP_K, tm), col),
                   pl.BlockSpec((TOP_K, tm), col),
                   pl.BlockSpec((TOP_K, tm), col),
                   pl.BlockSpec((N_EXPERTS, LANES), const)],
        out_shape=[jax.ShapeDtypeStruct((t, d), F32),
                   jax.ShapeDtypeStruct((t * ROW_CHUNKS, LANES), F32),
                   jax.ShapeDtypeStruct((TOP_K, t), I32),
                   jax.ShapeDtypeStruct((TOP_K, t), F32),
                   jax.ShapeDtypeStruct((TOP_K, t), I32),
                   jax.ShapeDtypeStruct((N_EXPERTS, LANES), F32)],
        scratch_shapes=[pltpu.VMEM((N_EXPERTS, LANES), F32)],
        name="post_mix_router",
        compiler_params=_cparams(1),
    )(x2d, oa, ob, ga, gb, wo, g_post, gt, g_pre, sc, sh, wr_t, br)


def _row_copy(src, src_row, dst, dst_row, sem):
    return pltpu.make_async_copy(src.at[pl.ds(pl.multiple_of(src_row * ROW_CHUNKS, ROW_CHUNKS), ROW_CHUNKS)],
                                 dst.at[pl.ds(pl.multiple_of(dst_row * ROW_CHUNKS, ROW_CHUNKS), ROW_CHUNKS)], sem)


def _dispatch_kernel(pad_lo_ref, pad_hi_ref, dest_ref, h2r_ref, xs_hbm, dsm, zero_rows, sem_idx, sem, sem_pad, *, tg):
    def pad_runs(op):
        def per_expert(e, carry):
            pos = pad_lo_ref[e]
            n = pad_hi_ref[e] - pos
            for run in PAD_RUNS:
                @pl.when((n & run) != 0)
                def _(pos=pos, run=run):
                    op(pos, run)
                pos = pos + (n & run)
            return carry
        lax.fori_loop(0, N_EXPERTS, per_expert, 0)

    def zero_copy(pos, run):
        return pltpu.make_async_copy(
            zero_rows.at[pl.ds(0, run * ROW_CHUNKS)],
            xs_hbm.at[pl.ds(pl.multiple_of(pos * ROW_CHUNKS, ROW_CHUNKS), run * ROW_CHUNKS)], sem_pad)

    @pl.when(pl.program_id(0) == 0)
    def _():
        zero_rows[...] = jnp.zeros_like(zero_rows)
        pad_runs(lambda pos, run: zero_copy(pos, run).start())

    idx_copy = pltpu.make_async_copy(dest_ref.at[0, 0], dsm, sem_idx)
    idx_copy.start()
    idx_copy.wait()

    def issue(it, carry):
        for u in range(DMA_UNROLL):
            tok = it * DMA_UNROLL + u
            for k in range(TOP_K):
                _row_copy(h2r_ref, tok, xs_hbm, dsm[k * tg + tok], sem).start(priority=k % 2)
        return carry

    lax.fori_loop(0, tg // DMA_UNROLL, issue, 0)

    def drain(it, carry):
        for _ in range(DMA_UNROLL * TOP_K):
            _row_copy(h2r_ref, 0, xs_hbm, 0, sem).wait()
        return carry

    lax.fori_loop(0, tg // DMA_UNROLL, drain, 0)

    @pl.when(pl.program_id(0) == pl.num_programs(0) - 1)
    def _():
        pad_runs(lambda pos, run: zero_copy(0, run).wait())


def _dispatch(pad_lo, pad_hi, dest_blocks, h2r, n_rows, tg):
    nsteps = dest_blocks.shape[0]
    grid_spec = pltpu.PrefetchScalarGridSpec(
        num_scalar_prefetch=2,
        grid=(nsteps,),
        in_specs=[pl.BlockSpec((1, 1, TOP_K * tg), lambda i, lo, hi: (i, 0, 0)),
                  pl.BlockSpec((tg * ROW_CHUNKS, LANES), lambda i, lo, hi: (i, 0))],
        out_specs=pl.BlockSpec(memory_space=pl.ANY),
        scratch_shapes=[pltpu.SMEM((TOP_K * tg,), I32), pltpu.VMEM((PAD_RUNS[0] * ROW_CHUNKS, LANES), F32),
                        pltpu.SemaphoreType.DMA, pltpu.SemaphoreType.DMA, pltpu.SemaphoreType.DMA],
    )
    return pl.pallas_call(
        functools.partial(_dispatch_kernel, tg=tg),
        grid_spec=grid_spec,
        out_shape=jax.ShapeDtypeStruct((n_rows * ROW_CHUNKS, LANES), F32),
        name="moe_dispatch",
        compiler_params=_cparams(1),
    )(pad_lo, pad_hi, dest_blocks, h2r)


def _expert_kernel(be_ref, nu_ref, xs_ref, wgu_ref, bgu_ref, wd_ref, bd_ref, ys_ref, wgu_bf, wd_bf, *, bm):
    i = pl.program_id(0)

    @pl.when((i == 0) | (be_ref[i] != be_ref[jnp.maximum(i - 1, 0)]))
    def _():
        wgu_bf[...] = wgu_ref[0, 0].astype(BF16)
        wd_bf[...] = wd_ref[0, 0].astype(BF16)

    @pl.when(i < nu_ref[0])
    def _():
        x = jnp.concatenate([xs_ref[pl.ds(c, bm, stride=ROW_CHUNKS), :] for c in range(ROW_CHUNKS)], axis=-1)
        gu = jnp.dot(x.astype(BF16), wgu_bf[...], preferred_element_type=F32) + bgu_ref[0, 0]
        gate = jnp.minimum(gu[:, :D_FF], SWIGLU_LIMIT)
        up = jnp.clip(gu[:, D_FF:], -SWIGLU_LIMIT, SWIGLU_LIMIT)
        act = gate * _sigmoid(SWIGLU_ALPHA * gate) * (up + 1.0)
        y = jnp.dot(act.astype(BF16), wd_bf[...], preferred_element_type=F32) + bd_ref[0, 0]
        for c in range(ROW_CHUNKS):
            ys_ref[pl.ds(c, bm, stride=ROW_CHUNKS), :] = y[:, c * LANES:(c + 1) * LANES]

    @pl.when(i >= nu_ref[0])
    def _():
        ys_ref[...] = jnp.zeros_like(ys_ref)


def _experts(block_e, n_used, xs, w_gate_up, b_gate_up, w_down, b_down, layer):
    bm = BM_EXPERT
    n_blocks = xs.shape[0] // (bm * ROW_CHUNKS)
    d = D_MODEL
    wsel = lambda i, be, nu: (layer, be[i], 0, 0)
    grid_spec = pltpu.PrefetchScalarGridSpec(
        num_scalar_prefetch=2,
        grid=(n_blocks,),
        in_specs=[pl.BlockSpec((bm * ROW_CHUNKS, LANES), lambda i, be, nu: (jnp.minimum(i, nu[0] - 1), 0)),
                  pl.BlockSpec((1, 1, d, 2 * D_FF), wsel),
                  pl.BlockSpec((1, 1, 1, 2 * D_FF), wsel),
                  pl.BlockSpec((1, 1, D_FF, d), wsel),
                  pl.BlockSpec((1, 1, 1, d), wsel)],
        out_specs=pl.BlockSpec((bm * ROW_CHUNKS, LANES), lambda i, be, nu: (i, 0)),
        scratch_shapes=[pltpu.VMEM((d, 2 * D_FF), BF16), pltpu.VMEM((D_FF, d), BF16)],
    )
    return pl.pallas_call(
        functools.partial(_expert_kernel, bm=bm),
        grid_spec=grid_spec,
        out_shape=jax.ShapeDtypeStruct(xs.shape, F32),
        name="moe_experts",
        compiler_params=pltpu.CompilerParams(dimension_semantics=("arbitrary",),
                                             vmem_limit_bytes=VMEM_LIMIT_EXPERTS),
    )(block_e, n_used, xs, w_gate_up, b_gate_up, w_down, b_down)


def _combine_kernel(dest_ref, w_ref, x_ref, gt_ref, g_ref, ys_hbm, o_ref, dsm, buf, sem_idx, sem, *, tc):
    idx_copy = pltpu.make_async_copy(dest_ref.at[0, 0], dsm, sem_idx)
    idx_copy.start()
    idx_copy.wait()

    def issue(it, carry):
        for u in range(DMA_UNROLL):
            tok = it * DMA_UNROLL + u
            for k in range(TOP_K):
                _row_copy(ys_hbm, dsm[k * tc + tok], buf, k * tc + tok, sem).start(priority=k % 2)
        return carry

    lax.fori_loop(0, tc // DMA_UNROLL, issue, 0)

    def drain(it, carry):
        for _ in range(DMA_UNROLL * TOP_K):
            _row_copy(ys_hbm, 0, buf, 0, sem).wait()
        return carry

    lax.fori_loop(0, tc // DMA_UNROLL, drain, 0)

    y = jnp.zeros((tc, D_MODEL), F32)
    for k in range(TOP_K):
        yk = jnp.concatenate([buf[pl.ds(k * tc * ROW_CHUNKS + c, tc, stride=ROW_CHUNKS), :]
                              for c in range(ROW_CHUNKS)], axis=-1)
        y = y + w_ref[:, k:k + 1] * yk
    ms = jnp.mean(y * y, axis=-1, keepdims=True)
    o_ref[...] = x_ref[...] + gt_ref[0] * (y * lax.rsqrt(ms + NORM_EPS) * g_ref[...])


def _combine(dest_blocks, w_tok, x1, gt, g_post, ys, seq, tc):
    t, d = x1.shape
    spb = seq // tc
    return pl.pallas_call(
        functools.partial(_combine_kernel, tc=tc),
        grid=(t // tc,),
        in_specs=[pl.BlockSpec((1, 1, TOP_K * tc), lambda i: (i, 0, 0)),
                  pl.BlockSpec((tc, TOP_K), lambda i: (i, 0)),
                  pl.BlockSpec((tc, d), lambda i: (i, 0)),
                  pl.BlockSpec((1, 1, d), lambda i: (i // spb, 0, 0)),
                  pl.BlockSpec((1, d), lambda i: (0, 0)),
                  pl.BlockSpec(memory_space=pl.ANY)],
        out_specs=pl.BlockSpec((tc, d), lambda i: (i, 0)),
        out_shape=jax.ShapeDtypeStruct((t, d), F32),
        scratch_shapes=[pltpu.SMEM((TOP_K * tc,), I32),
                        pltpu.VMEM((TOP_K * tc * ROW_CHUNKS, LANES), F32),
                        pltpu.SemaphoreType.DMA, pltpu.SemaphoreType.DMA],
        name="moe_combine",
        compiler_params=_cparams(1),
    )(dest_blocks, w_tok, x1, gt, g_post, ys)


def _dest_blocks(dest, step_tokens):
    k, t = dest.shape
    return dest.reshape(k, t // step_tokens, step_tokens).transpose(1, 0, 2).reshape(t // step_tokens, 1,
                                                                                      k * step_tokens)


def _bias_tables(rel_bias_table, tq, tkc):
    def lookup(dist, table):
        onehot = np.eye(N_BUCKETS, dtype=np.float32)[_t5_bucket_np(dist)]
        return jnp.einsum('...n,nh->h...', onehot, table, precision=lax.Precision.HIGHEST)

    qi = np.arange(BLOCK)[:, None]
    kj = np.arange(2 * BLOCK)[None, :]
    dist = qi + BLOCK - kj
    valid = (dist >= 0) & (dist < WINDOW)
    bias_a = jnp.where(valid[None], lookup(dist, rel_bias_table[:, :H_A]), MASK_VALUE)
    kk = np.arange(tkc)[:, None]
    qq = np.arange(tq)[None, :]
    table_b = rel_bias_table[:, H_A:]
    table_b = (table_b - table_b[N_BUCKETS - 1:]) * LOG2E
    tiles = [jnp.zeros((H_B, tkc, tq), F32)]
    for rel in range(-1, tq // tkc):
        d = qq - (rel * tkc + kk)
        tiles.append(jnp.where((d >= 0)[None], lookup(d, table_b), MASK_VALUE))
    tiles.append(jnp.full((H_B, tkc, tq), MASK_VALUE, F32))
    bias_t = jnp.stack(tiles, axis=1)
    return bias_a.astype(F32), bias_t.astype(F32)


def _prep_in_weights(w_in_l):
    qa, ka, va, qb, kb, vb, gza, gzb = jnp.split(w_in_l, [int(v) for v in np.cumsum(SPLIT_SIZES)[:-1]], axis=-1)

    def dup(w):
        return jnp.concatenate([w[:, :DH_A], w[:, :DH_A], w[:, DH_A:], w[:, DH_A:]], axis=-1)

    w_nat = jnp.concatenate([qa * DH_A ** -0.5, dup(ka), dup(va), kb, gza, gzb], axis=-1).astype(BF16)
    w_t = jnp.concatenate([qb * (DH_B ** -0.5 * LOG2E), vb], axis=-1).T.astype(BF16)
    return w_nat, w_t


def kernel(x, c, rel_bias_table, w_ada, b_ada, g_mix_pre, g_mix_post, g_ffn_pre, g_ffn_post, w_in, w_out,
           attn_sinks, diff_lambda, diff_subln_g, w_router, b_router, w_gate_up, b_gate_up, w_down, b_down):
    bsz, seq, d = x.shape
    depth = w_ada.shape[0]
    t = bsz * seq
    tq, tkc = TQ_DIFF, TKC_DIFF
    assert d == D_MODEL and tkc >= BLOCK and tq % tkc == 0
    assert (TM_PROJ % tq == 0 or tq % TM_PROJ == 0) and TM_PROJ % tkc == 0
    assert seq % tq == 0 and (seq // tkc) % KC_DIFF == 0
    assert seq % TM_PROJ == 0 and seq % TM_SWA == 0
    assert seq % TG_DISPATCH == 0 and seq % TC_COMBINE == 0

    mod = _modulation(c, w_ada, b_ada)
    bias_a, bias_t = _bias_tables(rel_bias_table, tq, tkc)
    n_rows = t * TOP_K + N_EXPERTS * BM_EXPERT
    n_blocks = n_rows // BM_EXPERT

    x2d = x.reshape(t, d)
    for l in range(depth):
        sh_m, sc_m, gt_m, sh_f, sc_f, gt_f = [v[:, None, :] for v in jnp.split(mod[l], 6, axis=-1)]
        lam_init = 0.8 - 0.6 * math.exp(-0.3 * l)
        w_nat, w_t = _prep_in_weights(w_in[l])
        qa, ka2, va2, kb, ga, gb, qbt, vbt = _in_projection(
            x2d, sc_m, sh_m, g_mix_pre[l][None, :], w_nat, w_t, bsz, seq)
        o_a = _swa_attention(qa.reshape(bsz, seq, QA_W), ka2.reshape(bsz, seq, 2 * KA_W),
                             va2.reshape(bsz, seq, 2 * VA_W), attn_sinks[l], bias_a)
        g_b = jnp.broadcast_to(diff_subln_g[l][:, None], (DV_B, tq)).astype(F32)
        o_b = _diff_attention(qbt, kb.reshape(bsz, seq, KB_W), vbt, bias_t, diff_lambda[l], g_b, lam_init)
        x1, h2r, top_idx, top_w, rank, counts = _post_mix(
            x2d, o_a.reshape(t, QA_W), o_b.reshape(t, H_B * DV_B), ga, gb, w_out[l].astype(BF16),
            g_mix_post[l][None, :], gt_m, g_ffn_pre[l][None, :], sc_f, sh_f,
            w_router[l].T, b_router[l][:, None], seq)

        cnt = counts[:, 0].astype(I32)
        padded = (cnt + BM_EXPERT - 1) // BM_EXPERT * BM_EXPERT
        pends = jnp.cumsum(padded)
        pstarts = pends - padded
        experts = jnp.arange(N_EXPERTS, dtype=I32)
        dest = jnp.sum(jnp.where(top_idx[..., None] == experts, pstarts, 0), axis=-1) + rank
        block_start = jnp.arange(n_blocks, dtype=I32) * BM_EXPERT
        block_e = jnp.minimum(jnp.sum((pends[None, :] <= block_start[:, None]).astype(I32), axis=1),
                              N_EXPERTS - 1).astype(I32)
        n_used = (pends[-1:] // BM_EXPERT).astype(I32)

        xs = _dispatch((pstarts + cnt).astype(I32), pends.astype(I32), _dest_blocks(dest, TG_DISPATCH), h2r,
                       n_rows, TG_DISPATCH)
        ys = _experts(block_e, n_used, xs, w_gate_up, b_gate_up[:, :, None, :], w_down, b_down[:, :, None, :], l)
        x2d = _combine(_dest_blocks(dest, TC_COMBINE), top_w.T, x1, gt_f, g_ffn_post[l][None, :], ys,
                       seq, TC_COMBINE)
    return x2d.reshape(bsz, seq, d)
```

```python
import functools
import math

import numpy as np
import jax
import jax.numpy as jnp
from jax import lax
from jax.experimental import pallas as pl
from jax.experimental.pallas import tpu as pltpu

F32 = jnp.float32
BF16 = jnp.bfloat16
I32 = jnp.int32

D_MODEL = 1024
BLOCK = 128
WINDOW = 128
H_A, KV_A, DH_A = 8, 2, 64
G_A = H_A // KV_A
H_B, DH_B = 4, 64
DV_B = 2 * DH_B
DV_EXT = DV_B + 16
N_BUCKETS = 32
MAX_EXACT = N_BUCKETS // 2
REL_MAX_DIST = 128
N_EXPERTS = 32
TOP_K = 4
D_FF = D_MODEL
SWIGLU_ALPHA = 1.702
SWIGLU_LIMIT = 7.0
NORM_EPS = 1e-6
SUBLN_EPS = 1e-5
QA_W = H_A * DH_A
KA_W = KV_A * DH_A
VA_W = KV_A * DH_A
QB_W = H_B * 2 * DH_B
KB_W = H_B * 2 * DH_B
VB_W = H_B * 2 * DH_B
SPLIT_SIZES = (QA_W, KA_W, VA_W, QB_W, KB_W, VB_W, D_MODEL, D_MODEL)

LANES = 128
SUBLANES = 8
ROW_CHUNKS = D_MODEL // LANES
MASK_VALUE = -1e30

TM_PROJ = 512
TQ_DIFF = 512
TKC_DIFF = 256
KC_DIFF = 4
LOG2E = math.log2(math.e)
TM_SWA = 512
BM_EXPERT = 512
PAD_RUNS = tuple(BM_EXPERT >> s for s in range(1, BM_EXPERT.bit_length()))
TG_DISPATCH = 2048
TC_COMBINE = 1024
COMBINE_PARTS = 4
DMA_UNROLL = 8
MOD_COLUMN_STEPS = 4
VMEM_LIMIT = 48 * 1024 * 1024
_EXPERT_W_ELEMS = 3 * D_MODEL * D_FF
VMEM_LIMIT_EXPERTS = (2 * 4 * _EXPERT_W_ELEMS + 2 * _EXPERT_W_ELEMS
                      + 4 * 4 * BM_EXPERT * D_MODEL + 4 * BM_EXPERT * (4 * D_FF + 2 * D_MODEL))


def _cparams(n_axes):
    return pltpu.CompilerParams(dimension_semantics=("arbitrary",) * n_axes,
                                vmem_limit_bytes=VMEM_LIMIT)


def _sigmoid(x):
    return 1.0 / (1.0 + jnp.exp(-x))


def _t5_bucket_np(dist):
    n = np.maximum(dist, 0)
    nf = np.maximum(n, 1).astype(np.float32)
    large = MAX_EXACT + (np.log(nf / np.float32(MAX_EXACT)) / np.float32(math.log(REL_MAX_DIST / MAX_EXACT))
                         * np.float32(N_BUCKETS - MAX_EXACT)).astype(np.int32)
    large = np.minimum(large, N_BUCKETS - 1)
    return np.where(n < MAX_EXACT, n, large)


def _mod_kernel(c_ref, w_ref, b_ref, o_ref):
    c = c_ref[...]
    ca = c * _sigmoid(c)
    o_ref[0] = jnp.dot(ca, w_ref[0], preferred_element_type=F32,
                       precision=lax.Precision.HIGHEST) + b_ref[0]


def _modulation(c, w_ada, b_ada):
    depth, d, n = w_ada.shape
    bsz = c.shape[0]
    rows = -(-bsz // SUBLANES) * SUBLANES
    c_pad = jnp.zeros((rows, d), F32).at[:bsz].set(c)
    tn = n // MOD_COLUMN_STEPS
    out = pl.pallas_call(
        _mod_kernel,
        grid=(depth, n // tn),
        in_specs=[pl.BlockSpec((rows, d), lambda l, j: (0, 0)),
                  pl.BlockSpec((1, d, tn), lambda l, j: (l, 0, j)),
                  pl.BlockSpec((1, 1, tn), lambda l, j: (l, 0, j))],
        out_specs=pl.BlockSpec((1, rows, tn), lambda l, j: (l, 0, j)),
        out_shape=jax.ShapeDtypeStruct((depth, rows, n), F32),
        name="adaln_modulation",
        compiler_params=_cparams(2),
    )(c_pad, w_ada, b_ada.reshape(depth, 1, n))
    return out[:, :bsz]


O_QA = 0
O_KA = O_QA + QA_W
O_VA = O_KA + 2 * KA_W
O_KB = O_VA + 2 * VA_W
O_GA = O_KB + KB_W
O_GB = O_GA + D_MODEL
N_NAT = O_GB + D_MODEL


def _inproj_kernel(x_ref, sc_ref, sh_ref, g_ref, w_ref, wt_ref,
                   qa_ref, ka_ref, va_ref, kb_ref, ga_ref, gb_ref, qbt_ref, vbt_ref):
    x = x_ref[...]
    ms = jnp.mean(x * x, axis=-1, keepdims=True)
    h = x * lax.rsqrt(ms + NORM_EPS) * g_ref[...]
    h = h * (1.0 + sc_ref[0]) + sh_ref[0]
    hb = h.astype(BF16)
    tm = x.shape[0]

    def proj(a, b):
        return jnp.dot(hb, w_ref[:, a:b], preferred_element_type=F32)

    qa_ref[...] = proj(O_QA, O_KA).astype(BF16)
    ka_ref[...] = proj(O_KA, O_VA).astype(BF16)
    va_ref[...] = proj(O_VA, O_KB).astype(BF16)
    kb_ref[...] = proj(O_KB, O_GA).astype(BF16)
    ga_ref[...] = _sigmoid(proj(O_GA, O_GB)).astype(BF16)
    gb_ref[...] = _sigmoid(proj(O_GB, N_NAT)).astype(BF16)
    for t_ref, base in ((qbt_ref, 0), (vbt_ref, QB_W)):
        rt = lax.dot_general(wt_ref[base: base + H_B * DV_B, :], hb,
                             (((1,), (1,)), ((), ())), preferred_element_type=F32)
        n_sub, width = t_ref.shape[2], t_ref.shape[4]
        for hh in range(H_B):
            for j in range(n_sub):
                t_ref[0, hh, j, :DV_B, :] = rt[hh * DV_B:(hh + 1) * DV_B, j * width:(j + 1) * width].astype(BF16)
    ones_shape = vbt_ref.shape[:3] + (DV_EXT - DV_B, vbt_ref.shape[4])
    vbt_ref[:, :, :, DV_B:, :] = jnp.ones(ones_shape, BF16)


def _in_projection(x2d, sc, sh, g, w_nat, w_t, bsz, seq):
    t, d = x2d.shape
    tm, tq, tkc = TM_PROJ, TQ_DIFF, TKC_DIFF
    spb = seq // tm
    row = lambda i: (i, 0)
    bat = lambda i: (i // spb, 0, 0)
    tr = lambda i: (i // spb, 0, i % spb, 0, 0)
    if tq <= tm:
        q_spec = pl.BlockSpec((1, H_B, tm // tq, DV_B, tq), tr)
    else:
        q_spec = pl.BlockSpec((1, H_B, 1, DV_B, tm),
                              lambda i: (i // spb, 0, (i % spb) // (tq // tm), 0, (i % spb) % (tq // tm)))
    outs = pl.pallas_call(
        _inproj_kernel,
        grid=(t // tm,),
        in_specs=[pl.BlockSpec((tm, d), row),
                  pl.BlockSpec((1, 1, d), bat),
                  pl.BlockSpec((1, 1, d), bat),
                  pl.BlockSpec((1, d), lambda i: (0, 0)),
                  pl.BlockSpec((d, N_NAT), lambda i: (0, 0)),
                  pl.BlockSpec((QB_W + VB_W, d), lambda i: (0, 0))],
        out_specs=[pl.BlockSpec((tm, QA_W), row),
                   pl.BlockSpec((tm, 2 * KA_W), row),
                   pl.BlockSpec((tm, 2 * VA_W), row),
                   pl.BlockSpec((tm, KB_W), row),
                   pl.BlockSpec((tm, d), row),
                   pl.BlockSpec((tm, d), row),
                   q_spec,
                   pl.BlockSpec((1, H_B, tm // tkc, DV_EXT, tkc), tr)],
        out_shape=[jax.ShapeDtypeStruct((t, QA_W), BF16),
                   jax.ShapeDtypeStruct((t, 2 * KA_W), BF16),
                   jax.ShapeDtypeStruct((t, 2 * VA_W), BF16),
                   jax.ShapeDtypeStruct((t, KB_W), BF16),
                   jax.ShapeDtypeStruct((t, d), BF16),
                   jax.ShapeDtypeStruct((t, d), BF16),
                   jax.ShapeDtypeStruct((bsz, H_B, seq // tq, DV_B, tq), BF16),
                   jax.ShapeDtypeStruct((bsz, H_B, seq // tkc, DV_EXT, tkc), BF16)],
        name="in_projection",
        compiler_params=_cparams(1),
    )(x2d, sc, sh, g, w_nat, w_t)
    return outs


def _swa_kernel(sink_ref, q_ref, kc_ref, kp_ref, vc_ref, vp_ref, bias_ref, o_ref):
    i = pl.program_id(1)
    nsub = q_ref.shape[1] // BLOCK
    lane_k = lax.broadcasted_iota(I32, (2 * BLOCK, LANES), 1)
    lane_s = lax.broadcasted_iota(I32, (BLOCK, 2 * BLOCK), 1)
    first_mask = jnp.where((i == 0) & (lane_s < BLOCK), MASK_VALUE, 0.0).astype(F32)
    zero_b = jnp.zeros((2 * BLOCK, LANES), BF16)
    for j in range(nsub):
        rows = slice(j * BLOCK, (j + 1) * BLOCK)
        if j == 0:
            kband = jnp.concatenate([kp_ref[0], kc_ref[0, rows, :]], axis=0)
            vband = jnp.concatenate([vp_ref[0], vc_ref[0, rows, :]], axis=0)
        else:
            kband = kc_ref[0, (j - 1) * BLOCK:(j + 1) * BLOCK, :]
            vband = vc_ref[0, (j - 1) * BLOCK:(j + 1) * BLOCK, :]
        for kv in range(KV_A):
            kslab = kband[:, kv * LANES:(kv + 1) * LANES]
            vslab = vband[:, kv * LANES:(kv + 1) * LANES]
            kpad = (jnp.where(lane_k < DH_A, kslab, zero_b), jnp.where(lane_k >= DH_A, kslab, zero_b))
            vpad = (jnp.where(lane_k < DH_A, vslab, zero_b), jnp.where(lane_k >= DH_A, vslab, zero_b))
            for pp in range(G_A // 2):
                pair = kv * (G_A // 2) + pp
                qslab = q_ref[0, rows, pair * LANES:(pair + 1) * LANES]
                o_slab = jnp.zeros((BLOCK, LANES), F32)
                for par in range(2):
                    head = 2 * pair + par
                    s = lax.dot_general(qslab, kpad[par], (((1,), (1,)), ((), ())),
                                        preferred_element_type=F32)
                    s = s + bias_ref[head]
                    if j == 0:
                        s = s + first_mask
                    sink = sink_ref[head]
                    m = jnp.maximum(jnp.max(s, axis=-1, keepdims=True), sink)
                    e = jnp.exp(s - m)
                    den = jnp.sum(e, axis=-1, keepdims=True) + jnp.exp(sink - m)
                    pv = jnp.dot(e.astype(BF16), vpad[par], preferred_element_type=F32)
                    o_slab = o_slab + pv / den
                o_ref[0, rows, pair * LANES:(pair + 1) * LANES] = o_slab.astype(BF16)


def _swa_attention(qa, ka2, va2, sinks, bias_a):
    bsz, seq, _ = qa.shape
    tm = TM_SWA
    nsub = tm // BLOCK
    cur = lambda b, i: (b, i, 0)
    prev = lambda b, i: (b, jnp.maximum(i * nsub - 1, 0), 0)
    return pl.pallas_call(
        _swa_kernel,
        grid=(bsz, seq // tm),
        in_specs=[pl.BlockSpec(memory_space=pltpu.SMEM),
                  pl.BlockSpec((1, tm, QA_W), cur),
                  pl.BlockSpec((1, tm, 2 * KA_W), cur),
                  pl.BlockSpec((1, BLOCK, 2 * KA_W), prev),
                  pl.BlockSpec((1, tm, 2 * VA_W), cur),
                  pl.BlockSpec((1, BLOCK, 2 * VA_W), prev),
                  pl.BlockSpec((H_A, BLOCK, 2 * BLOCK), lambda b, i: (0, 0, 0))],
        out_specs=pl.BlockSpec((1, tm, QA_W), cur),
        out_shape=jax.ShapeDtypeStruct((bsz, seq, QA_W), BF16),
        name="swa_attention",
        compiler_params=_cparams(2),
    )(sinks, qa, ka2, ka2, va2, va2, bias_a)


def _diffattn_kernel(lp_ref, qt_ref, k_ref, vt_ref, bias_ref, g_ref, o_ref, acc_ref, *s_bufs, tq, tkc, lam_init):
    kc = len(s_bufs)
    r = tq // tkc
    nqb = qt_ref.shape[2]
    last_chunk = vt_ref.shape[2] - 1
    row = lax.broadcasted_iota(I32, (DV_B, tq), 0)
    zero_q = jnp.zeros((DV_B, tq), BF16)
    lp = lp_ref[...]
    lam = (jnp.exp(jnp.sum(lp[0:1] * lp[1:2], axis=-1, keepdims=True))
           - jnp.exp(jnp.sum(lp[2:3] * lp[3:4], axis=-1, keepdims=True)) + lam_init)

    def n_groups(i):
        return ((i + 1) * r + kc - 1) // kc

    def q_both(i):
        qt = qt_ref[0, 0, i]
        return jnp.concatenate([jnp.where(row < DH_B, qt, zero_q), jnp.where(row >= DH_B, qt, zero_q)], axis=1)

    def scores_chunk(c, i, g, qb):
        blk = g * kc + c
        src = jnp.minimum(blk, last_chunk)
        kblk = k_ref[0, pl.ds(pl.multiple_of(src * tkc, tkc), tkc), :]
        s = jnp.dot(kblk, qb, preferred_element_type=F32)
        tile = bias_ref[0, jnp.clip(blk - i * r + 2, 0, r + 2)]
        s = s + jnp.concatenate([tile, tile], axis=1)
        s_bufs[c][...] = s
        return jnp.max(s, axis=0, keepdims=True)

    def finish(i):
        acc = acc_ref[:DV_B, :]
        l_fin = acc_ref[DV_B:DV_B + 1, :]
        o = acc[:, :tq] / l_fin[:, :tq] - lam * (acc[:, tq:] / l_fin[:, tq:])
        ms = jnp.mean(o * o, axis=0, keepdims=True)
        y = o * lax.rsqrt(ms + SUBLN_EPS) * g_ref[...] * (1.0 - lam_init)
        o_ref[0, pl.ds(pl.multiple_of(i * tq, tq), tq), :] = y.T.astype(BF16)

    def step(_, carry):
        i, g, m_prev, mx = carry
        last = g == n_groups(i) - 1
        i_next = jnp.minimum(jnp.where(last, i + 1, i), nqb - 1)
        g_next = jnp.where(last, 0, g + 1)
        qb_next = q_both(i_next)
        m_old = jnp.where(g == 0, MASK_VALUE, m_prev)
        m_new = jnp.maximum(m_old, mx)
        alpha = jnp.exp2(m_old - m_new)
        mx_next = jnp.full_like(mx, MASK_VALUE)
        pv = None
        for c in range(kc):
            p = jnp.exp2((s_bufs[c][...] - m_new).astype(BF16))
            vblk = vt_ref[0, 0, jnp.minimum(g * kc + c, last_chunk)]
            d = jnp.dot(vblk, p, preferred_element_type=F32)
            pv = d if pv is None else pv + d
            mx_next = jnp.maximum(mx_next, scores_chunk(c, i_next, g_next, qb_next))
        acc_ref[...] = alpha * acc_ref[...] + pv

        @pl.when(last)
        def _():
            finish(i)

        return i_next, g_next, m_new, mx_next

    acc_ref[...] = jnp.zeros_like(acc_ref)
    qb0 = q_both(0)
    mx0 = jnp.full((1, 2 * tq), MASK_VALUE, F32)
    for c in range(kc):
        mx0 = jnp.maximum(mx0, scores_chunk(c, 0, 0, qb0))
    n_steps = sum(((i + 1) * r + kc - 1) // kc for i in range(nqb))
    lax.fori_loop(0, n_steps, step, (jnp.int32(0), jnp.int32(0), jnp.full((1, 2 * tq), MASK_VALUE, F32), mx0))


def _diff_attention(qbt, kb, vbt, bias_t, lp, g_b, lam_init):
    bsz, _, nqb, _, tq = qbt.shape
    nkb, tkc = vbt.shape[2], vbt.shape[4]
    seq = nqb * tq
    n_tiles = bias_t.shape[1]
    return pl.pallas_call(
        functools.partial(_diffattn_kernel, tq=tq, tkc=tkc, lam_init=lam_init),
        grid=(bsz, H_B),
        in_specs=[pl.BlockSpec((4, DH_B), lambda b, h: (0, 0)),
                  pl.BlockSpec((1, 1, nqb, DV_B, tq), lambda b, h: (b, h, 0, 0, 0)),
                  pl.BlockSpec((1, seq, DV_B), lambda b, h: (b, 0, h)),
                  pl.BlockSpec((1, 1, nkb, DV_EXT, tkc), lambda b, h: (b, h, 0, 0, 0)),
                  pl.BlockSpec((1, n_tiles, tkc, tq), lambda b, h: (h, 0, 0, 0)),
                  pl.BlockSpec((DV_B, tq), lambda b, h: (0, 0))],
        out_specs=pl.BlockSpec((1, seq, DV_B), lambda b, h: (b, 0, h)),
        out_shape=jax.ShapeDtypeStruct((bsz, seq, H_B * DV_B), BF16),
        scratch_shapes=[pltpu.VMEM((DV_EXT, 2 * tq), F32)] + [pltpu.VMEM((tkc, 2 * tq), F32)] * KC_DIFF,
        name="diff_attention",
        compiler_params=_cparams(2),
    )(lp, qbt, kb, vbt, bias_t, g_b)


def _postmix_kernel(x_ref, oa_ref, ob_ref, ga_ref, gb_ref, wo_ref, gpost_ref, gt_ref, gpre_ref, sc_ref, sh_ref,
                    wr_ref, br_ref,
                    x1_ref, h2r_ref, idx_ref, w_ref, rank_ref, cnt_ref, carry_ref):
    step = pl.program_id(0)
    tm = x_ref.shape[0]

    @pl.when(step == 0)
    def _():
        carry_ref[...] = jnp.zeros_like(carry_ref)

    ya = jnp.dot(oa_ref[...], wo_ref[:QA_W, :], preferred_element_type=F32)
    yb = jnp.dot(ob_ref[...], wo_ref[QA_W:, :], preferred_element_type=F32)
    y = ga_ref[...].astype(F32) * ya + gb_ref[...].astype(F32) * yb
    ms = jnp.mean(y * y, axis=-1, keepdims=True)
    x1 = x_ref[...] + gt_ref[0] * (y * lax.rsqrt(ms + NORM_EPS) * gpost_ref[...])
    x1_ref[...] = x1
    ms1 = jnp.mean(x1 * x1, axis=-1, keepdims=True)
    h2 = x1 * lax.rsqrt(ms1 + NORM_EPS) * gpre_ref[...]
    h2 = h2 * (1.0 + sc_ref[0]) + sh_ref[0]
    for c in range(ROW_CHUNKS):
        h2r_ref[pl.ds(c, tm, stride=ROW_CHUNKS), :] = h2[:, c * LANES:(c + 1) * LANES]

    logits = lax.dot_general(wr_ref[...], h2, (((1,), (1,)), ((), ())), preferred_element_type=F32,
                             precision=lax.Precision.HIGHEST) + br_ref[...]
    erow = lax.broadcasted_iota(I32, logits.shape, 0)
    vals, sels = [], []
    cur = logits
    for k in range(TOP_K):
        mv = jnp.max(cur, axis=0, keepdims=True)
        idx = jnp.min(jnp.where(cur == mv, erow, N_EXPERTS), axis=0, keepdims=True)
        sel = erow == idx
        vals.append(mv)
        sels.append(sel)
        idx_ref[k:k + 1, :] = idx
        cur = jnp.where(sel, -jnp.inf, cur)
    es = [jnp.exp(v - vals[0]) for v in vals]
    den = es[0] + es[1] + es[2] + es[3]
    for k in range(TOP_K):
        w_ref[k:k + 1, :] = es[k] / den

    onehot = jnp.zeros(logits.shape, F32)
    for k in range(TOP_K):
        onehot = onehot + jnp.where(sels[k], 1.0, 0.0)
    srow = lax.broadcasted_iota(I32, (tm, tm), 0)
    scol = lax.broadcasted_iota(I32, (tm, tm), 1)
    upper = jnp.where(srow <= scol, 1.0, 0.0).astype(BF16)
    cum = jnp.dot(onehot.astype(BF16), upper, preferred_element_type=F32)
    carry = carry_ref[:, 0:1]
    excl = cum - onehot + carry
    for k in range(TOP_K):
        rank_ref[k:k + 1, :] = jnp.sum(jnp.where(sels[k], excl, 0.0), axis=0, keepdims=True).astype(I32)
    new_carry = carry + jnp.sum(onehot, axis=1, keepdims=True)
    carry_ref[...] = jnp.broadcast_to(new_carry, carry_ref.shape)
    cnt_ref[...] = jnp.broadcast_to(new_carry, cnt_ref.shape)


def _post_mix(x2d, oa, ob, ga, gb, wo, g_post, gt, g_pre, sc, sh, wr_t, br, seq):
    t, d = x2d.shape
    tm = TM_PROJ
    spb = seq // tm
    row = lambda i: (i, 0)
    bat = lambda i: (i // spb, 0, 0)
    const = lambda i: (0, 0)
    col = lambda i: (0, i)
    return pl.pallas_call(
        _postmix_kernel,
        grid=(t // tm,),
        in_specs=[pl.BlockSpec((tm, d), row),
                  pl.BlockSpec((tm, QA_W), row),
                  pl.BlockSpec((tm, H_B * DV_B), row),
                  pl.BlockSpec((tm, d), row),
                  pl.BlockSpec((tm, d), row),
                  pl.BlockSpec((QA_W + H_B * DV_B, d), const),
                  pl.BlockSpec((1, d), const),
                  pl.BlockSpec((1, 1, d), bat),
                  pl.BlockSpec((1, d), const),
                  pl.BlockSpec((1, 1, d), bat),
                  pl.BlockSpec((1, 1, d), bat),
                  pl.BlockSpec((N_EXPERTS, d), const),
                  pl.BlockSpec((N_EXPERTS, 1), const)],
        out_specs=[pl.BlockSpec((tm, d), row),
                   pl.BlockSpec((tm * ROW_CHUNKS, LANES), row),
                   pl.BlockSpec((TOP_K, tm), col),
                   pl.BlockSpec((TOP_K, tm), col),
                   pl.BlockSpec((TOP_K, tm), col),
                   pl.BlockSpec((N_EXPERTS, LANES), const)],
        out_shape=[jax.ShapeDtypeStruct((t, d), F32),
                   jax.ShapeDtypeStruct((t * ROW_CHUNKS, LANES), F32),
                   jax.ShapeDtypeStruct((TOP_K, t), I32),
                   jax.ShapeDtypeStruct((TOP_K, t), F32),
                   jax.ShapeDtypeStruct((TOP_K, t), I32),
                   jax.ShapeDtypeStruct((N_EXPERTS, LANES), F32)],
        scratch_shapes=[pltpu.VMEM((N_EXPERTS, LANES), F32)],
        name="post_mix_router",
        compiler_params=_cparams(1),
    )(x2d, oa, ob, ga, gb, wo, g_post, gt, g_pre, sc, sh, wr_t, br)


def _row_copy(src, src_row, dst, dst_row, sem):
    return pltpu.make_async_copy(src.at[pl.ds(pl.multiple_of(src_row * ROW_CHUNKS, ROW_CHUNKS), ROW_CHUNKS)],
                                 dst.at[pl.ds(pl.multiple_of(dst_row * ROW_CHUNKS, ROW_CHUNKS), ROW_CHUNKS)], sem)


def _dispatch_kernel(pad_lo_ref, pad_hi_ref, dest_ref, h2r_ref, xs_hbm, dsm, zero_rows, sem_idx, sem, sem_pad, *, tg):
    def pad_runs(op):
        def per_expert(e, carry):
            pos = pad_lo_ref[e]
            n = pad_hi_ref[e] - pos
            for run in PAD_RUNS:
                @pl.when((n & run) != 0)
                def _(pos=pos, run=run):
                    op(pos, run)
                pos = pos + (n & run)
            return carry
        lax.fori_loop(0, N_EXPERTS, per_expert, 0)

    def zero_copy(pos, run):
        return pltpu.make_async_copy(
            zero_rows.at[pl.ds(0, run * ROW_CHUNKS)],
            xs_hbm.at[pl.ds(pl.multiple_of(pos * ROW_CHUNKS, ROW_CHUNKS), run * ROW_CHUNKS)], sem_pad)

    @pl.when(pl.program_id(0) == 0)
    def _():
        zero_rows[...] = jnp.zeros_like(zero_rows)
        pad_runs(lambda pos, run: zero_copy(pos, run).start())

    idx_copy = pltpu.make_async_copy(dest_ref.at[0, 0], dsm, sem_idx)
    idx_copy.start()
    idx_copy.wait()

    def issue(it, carry):
        for u in range(DMA_UNROLL):
            tok = it * DMA_UNROLL + u
            for k in range(TOP_K):
                _row_copy(h2r_ref, tok, xs_hbm, dsm[k * tg + tok], sem).start(priority=k % 2)
        return carry

    lax.fori_loop(0, tg // DMA_UNROLL, issue, 0)

    def drain(it, carry):
        for _ in range(DMA_UNROLL * TOP_K):
            _row_copy(h2r_ref, 0, xs_hbm, 0, sem).wait()
        return carry

    lax.fori_loop(0, tg // DMA_UNROLL, drain, 0)

    @pl.when(pl.program_id(0) == pl.num_programs(0) - 1)
    def _():
        pad_runs(lambda pos, run: zero_copy(0, run).wait())


def _dispatch(pad_lo, pad_hi, dest_blocks, h2r, n_rows, tg):
    nsteps = dest_blocks.shape[0]
    grid_spec = pltpu.PrefetchScalarGridSpec(
        num_scalar_prefetch=2,
        grid=(nsteps,),
        in_specs=[pl.BlockSpec((1, 1, TOP_K * tg), lambda i, lo, hi: (i, 0, 0)),
                  pl.BlockSpec((tg * ROW_CHUNKS, LANES), lambda i, lo, hi: (i, 0))],
        out_specs=pl.BlockSpec(memory_space=pl.ANY),
        scratch_shapes=[pltpu.SMEM((TOP_K * tg,), I32), pltpu.VMEM((PAD_RUNS[0] * ROW_CHUNKS, LANES), F32),
                        pltpu.SemaphoreType.DMA, pltpu.SemaphoreType.DMA, pltpu.SemaphoreType.DMA],
    )
    return pl.pallas_call(
        functools.partial(_dispatch_kernel, tg=tg),
        grid_spec=grid_spec,
        out_shape=jax.ShapeDtypeStruct((n_rows * ROW_CHUNKS, LANES), F32),
        name="moe_dispatch",
        compiler_params=_cparams(1),
    )(pad_lo, pad_hi, dest_blocks, h2r)


def _expert_kernel(be_ref, nu_ref, xs_ref, wgu_ref, bgu_ref, wd_ref, bd_ref, ys_ref, wgu_bf, wd_bf, *, bm):
    i = pl.program_id(0)

    @pl.when((i == 0) | (be_ref[i] != be_ref[jnp.maximum(i - 1, 0)]))
    def _():
        wgu_bf[...] = wgu_ref[0, 0].astype(BF16)
        wd_bf[...] = wd_ref[0, 0].astype(BF16)

    @pl.when(i < nu_ref[0])
    def _():
        x = jnp.concatenate([xs_ref[pl.ds(c, bm, stride=ROW_CHUNKS), :] for c in range(ROW_CHUNKS)], axis=-1)
        gu = jnp.dot(x.astype(BF16), wgu_bf[...], preferred_element_type=F32) + bgu_ref[0, 0]
        gate = jnp.minimum(gu[:, :D_FF], SWIGLU_LIMIT)
        up = jnp.clip(gu[:, D_FF:], -SWIGLU_LIMIT, SWIGLU_LIMIT)
        act = gate * _sigmoid(SWIGLU_ALPHA * gate) * (up + 1.0)
        y = jnp.dot(act.astype(BF16), wd_bf[...], preferred_element_type=F32) + bd_ref[0, 0]
        for c in range(ROW_CHUNKS):
            ys_ref[pl.ds(c, bm, stride=ROW_CHUNKS), :] = y[:, c * LANES:(c + 1) * LANES]

    @pl.when(i >= nu_ref[0])
    def _():
        ys_ref[...] = jnp.zeros_like(ys_ref)


def _experts(block_e, n_used, xs, w_gate_up, b_gate_up, w_down, b_down, layer):
    bm = BM_EXPERT
    n_blocks = xs.shape[0] // (bm * ROW_CHUNKS)
    d = D_MODEL
    wsel = lambda i, be, nu: (layer, be[i], 0, 0)
    grid_spec = pltpu.PrefetchScalarGridSpec(
        num_scalar_prefetch=2,
        grid=(n_blocks,),
        in_specs=[pl.BlockSpec((bm * ROW_CHUNKS, LANES), lambda i, be, nu: (jnp.minimum(i, nu[0] - 1), 0)),
                  pl.BlockSpec((1, 1, d, 2 * D_FF), wsel),
                  pl.BlockSpec((1, 1, 1, 2 * D_FF), wsel),
                  pl.BlockSpec((1, 1, D_FF, d), wsel),
                  pl.BlockSpec((1, 1, 1, d), wsel)],
        out_specs=pl.BlockSpec((bm * ROW_CHUNKS, LANES), lambda i, be, nu: (i, 0)),
        scratch_shapes=[pltpu.VMEM((d, 2 * D_FF), BF16), pltpu.VMEM((D_FF, d), BF16)],
    )
    return pl.pallas_call(
        functools.partial(_expert_kernel, bm=bm),
        grid_spec=grid_spec,
        out_shape=jax.ShapeDtypeStruct(xs.shape, F32),
        name="moe_experts",
        compiler_params=pltpu.CompilerParams(dimension_semantics=("arbitrary",),
                                             vmem_limit_bytes=VMEM_LIMIT_EXPERTS),
    )(block_e, n_used, xs, w_gate_up, b_gate_up, w_down, b_down)


def _combine_kernel(dest_ref, w_ref, x_ref, gt_ref, g_ref, ys_hbm, o_ref, dsm, buf, sem_idx, sems, *, tc):
    idx_copy = pltpu.make_async_copy(dest_ref.at[0, 0], dsm, sem_idx)
    idx_copy.start()
    idx_copy.wait()
    tp = tc // COMBINE_PARTS

    def buf_row(part, k, tok):
        return (part * TOP_K + k) * tp + tok

    for part in range(COMBINE_PARTS):
        def issue(it, carry, part=part):
            for u in range(DMA_UNROLL):
                tok = it * DMA_UNROLL + u
                for k in range(TOP_K):
                    _row_copy(ys_hbm, dsm[k * tc + part * tp + tok], buf, buf_row(part, k, tok),
                              sems.at[part]).start(priority=k % 2)
            return carry

        lax.fori_loop(0, tp // DMA_UNROLL, issue, 0)

    for part in range(COMBINE_PARTS):
        def drain(it, carry, part=part):
            for _ in range(DMA_UNROLL * TOP_K):
                _row_copy(ys_hbm, 0, buf, 0, sems.at[part]).wait()
            return carry

        lax.fori_loop(0, tp // DMA_UNROLL, drain, 0)

        rows = slice(part * tp, (part + 1) * tp)
        y = jnp.zeros((tp, D_MODEL), F32)
        for k in range(TOP_K):
            yk = jnp.concatenate([buf[pl.ds(buf_row(part, k, 0) * ROW_CHUNKS + c, tp, stride=ROW_CHUNKS), :]
                                  for c in range(ROW_CHUNKS)], axis=-1)
            y = y + w_ref[rows, k:k + 1] * yk
        ms = jnp.mean(y * y, axis=-1, keepdims=True)
        o_ref[rows, :] = x_ref[rows, :] + gt_ref[0] * (y * lax.rsqrt(ms + NORM_EPS) * g_ref[...])


def _combine(dest_blocks, w_tok, x1, gt, g_post, ys, seq, tc):
    t, d = x1.shape
    spb = seq // tc
    return pl.pallas_call(
        functools.partial(_combine_kernel, tc=tc),
        grid=(t // tc,),
        in_specs=[pl.BlockSpec((1, 1, TOP_K * tc), lambda i: (i, 0, 0)),
                  pl.BlockSpec((tc, TOP_K), lambda i: (i, 0)),
                  pl.BlockSpec((tc, d), lambda i: (i, 0)),
                  pl.BlockSpec((1, 1, d), lambda i: (i // spb, 0, 0)),
                  pl.BlockSpec((1, d), lambda i: (0, 0)),
                  pl.BlockSpec(memory_space=pl.ANY)],
        out_specs=pl.BlockSpec((tc, d), lambda i: (i, 0)),
        out_shape=jax.ShapeDtypeStruct((t, d), F32),
        scratch_shapes=[pltpu.SMEM((TOP_K * tc,), I32),
                        pltpu.VMEM((TOP_K * tc * ROW_CHUNKS, LANES), F32),
                        pltpu.SemaphoreType.DMA, pltpu.SemaphoreType.DMA((COMBINE_PARTS,))],
        name="moe_combine",
        compiler_params=_cparams(1),
    )(dest_blocks, w_tok, x1, gt, g_post, ys)


def _dest_blocks(dest, step_tokens):
    k, t = dest.shape
    return dest.reshape(k, t // step_tokens, step_tokens).transpose(1, 0, 2).reshape(t // step_tokens, 1,
                                                                                      k * step_tokens)


def _bias_tables(rel_bias_table, tq, tkc):
    def lookup(dist, table):
        onehot = np.eye(N_BUCKETS, dtype=np.float32)[_t5_bucket_np(dist)]
        return jnp.einsum('...n,nh->h...', onehot, table, precision=lax.Precision.HIGHEST)

    qi = np.arange(BLOCK)[:, None]
    kj = np.arange(2 * BLOCK)[None, :]
    dist = qi + BLOCK - kj
    valid = (dist >= 0) & (dist < WINDOW)
    bias_a = jnp.where(valid[None], lookup(dist, rel_bias_table[:, :H_A]), MASK_VALUE)
    kk = np.arange(tkc)[:, None]
    qq = np.arange(tq)[None, :]
    table_b = rel_bias_table[:, H_A:]
    table_b = (table_b - table_b[N_BUCKETS - 1:]) * LOG2E
    tiles = [jnp.zeros((H_B, tkc, tq), F32)]
    for rel in range(-1, tq // tkc):
        d = qq - (rel * tkc + kk)
        tiles.append(jnp.where((d >= 0)[None], lookup(d, table_b), MASK_VALUE))
    tiles.append(jnp.full((H_B, tkc, tq), MASK_VALUE, F32))
    bias_t = jnp.stack(tiles, axis=1)
    return bias_a.astype(F32), bias_t.astype(F32)


def _prep_in_weights(w_in_l):
    qa, ka, va, qb, kb, vb, gza, gzb = jnp.split(w_in_l, [int(v) for v in np.cumsum(SPLIT_SIZES)[:-1]], axis=-1)

    def dup(w):
        return jnp.concatenate([w[:, :DH_A], w[:, :DH_A], w[:, DH_A:], w[:, DH_A:]], axis=-1)

    w_nat = jnp.concatenate([qa * DH_A ** -0.5, dup(ka), dup(va), kb, gza, gzb], axis=-1).astype(BF16)
    w_t = jnp.concatenate([qb * (DH_B ** -0.5 * LOG2E), vb], axis=-1).T.astype(BF16)
    return w_nat, w_t


def kernel(x, c, rel_bias_table, w_ada, b_ada, g_mix_pre, g_mix_post, g_ffn_pre, g_ffn_post, w_in, w_out,
           attn_sinks, diff_lambda, diff_subln_g, w_router, b_router, w_gate_up, b_gate_up, w_down, b_down):
    bsz, seq, d = x.shape
    depth = w_ada.shape[0]
    t = bsz * seq
    tq, tkc = TQ_DIFF, TKC_DIFF
    assert d == D_MODEL and tkc >= BLOCK and tq % tkc == 0
    assert (TM_PROJ % tq == 0 or tq % TM_PROJ == 0) and TM_PROJ % tkc == 0
    assert seq % tq == 0 and (seq // tkc) % KC_DIFF == 0
    assert seq % TM_PROJ == 0 and seq % TM_SWA == 0
    assert seq % TG_DISPATCH == 0 and seq % TC_COMBINE == 0

    mod = _modulation(c, w_ada, b_ada)
    bias_a, bias_t = _bias_tables(rel_bias_table, tq, tkc)
    n_rows = t * TOP_K + N_EXPERTS * BM_EXPERT
    n_blocks = n_rows // BM_EXPERT

    x2d = x.reshape(t, d)
    for l in range(depth):
        sh_m, sc_m, gt_m, sh_f, sc_f, gt_f = [v[:, None, :] for v in jnp.split(mod[l], 6, axis=-1)]
        lam_init = 0.8 - 0.6 * math.exp(-0.3 * l)
        w_nat, w_t = _prep_in_weights(w_in[l])
        qa, ka2, va2, kb, ga, gb, qbt, vbt = _in_projection(
            x2d, sc_m, sh_m, g_mix_pre[l][None, :], w_nat, w_t, bsz, seq)
        o_a = _swa_attention(qa.reshape(bsz, seq, QA_W), ka2.reshape(bsz, seq, 2 * KA_W),
                             va2.reshape(bsz, seq, 2 * VA_W), attn_sinks[l], bias_a)
        g_b = jnp.broadcast_to(diff_subln_g[l][:, None], (DV_B, tq)).astype(F32)
        o_b = _diff_attention(qbt, kb.reshape(bsz, seq, KB_W), vbt, bias_t, diff_lambda[l], g_b, lam_init)
        x1, h2r, top_idx, top_w, rank, counts = _post_mix(
            x2d, o_a.reshape(t, QA_W), o_b.reshape(t, H_B * DV_B), ga, gb, w_out[l].astype(BF16),
            g_mix_post[l][None, :], gt_m, g_ffn_pre[l][None, :], sc_f, sh_f,
            w_router[l].T, b_router[l][:, None], seq)

        cnt = counts[:, 0].astype(I32)
        padded = (cnt + BM_EXPERT - 1) // BM_EXPERT * BM_EXPERT
        pends = jnp.cumsum(padded)
        pstarts = pends - padded
        experts = jnp.arange(N_EXPERTS, dtype=I32)
        dest = jnp.sum(jnp.where(top_idx[..., None] == experts, pstarts, 0), axis=-1) + rank
        block_start = jnp.arange(n_blocks, dtype=I32) * BM_EXPERT
        block_e = jnp.minimum(jnp.sum((pends[None, :] <= block_start[:, None]).astype(I32), axis=1),
                              N_EXPERTS - 1).astype(I32)
        n_used = (pends[-1:] // BM_EXPERT).astype(I32)

        xs = _dispatch((pstarts + cnt).astype(I32), pends.astype(I32), _dest_blocks(dest, TG_DISPATCH), h2r,
                       n_rows, TG_DISPATCH)
        ys = _experts(block_e, n_used, xs, w_gate_up, b_gate_up[:, :, None, :], w_down, b_down[:, :, None, :], l)
        x2d = _combine(_dest_blocks(dest, TC_COMBINE), top_w.T, x1, gt_f, g_ffn_post[l][None, :], ys,
                       seq, TC_COMBINE)
    return x2d.reshape(bsz, seq, d)
```
